```python
import math
import jax
import jax.numpy as jnp
from jax import lax
import numpy as np

D_MODEL = 4096
BATCH = 8
SEQ = 2048
DEPTH = 2

D_MIX = D_MODEL
DN_HEAD_DIM = 128
DN_HEADS = (3 * D_MODEL) // (8 * DN_HEAD_DIM)
DN_WIDTH = DN_HEADS * DN_HEAD_DIM
DN_CONV = 5
ML_QK_DIM = 128
ML_V_DIM = 256
ML_HEADS = (3 * D_MODEL) // (8 * ML_V_DIM)
ML_WIDTH = ML_HEADS * ML_V_DIM
SC_WIDTH = D_MIX - DN_WIDTH - ML_WIDTH
SC_CONV = 3
CHUNK = 64
IN_SPLITS = (3 * DN_WIDTH, DN_WIDTH, 2 * DN_HEADS, 2 * DN_HEADS,
             ML_HEADS * ML_QK_DIM, ML_HEADS * ML_QK_DIM, ML_WIDTH, ML_WIDTH, 2 * ML_HEADS, 2 * ML_HEADS,
             SC_WIDTH, SC_WIDTH, SC_WIDTH)
IN_COLS = 4 * DN_WIDTH + 4 * DN_HEADS + 2 * ML_HEADS * ML_QK_DIM + 2 * ML_WIDTH + 4 * ML_HEADS + 3 * SC_WIDTH
N_GROUPS = 4
EXPERTS_PER_GROUP = 8
N_EXPERTS = N_GROUPS * EXPERTS_PER_GROUP
TOP_K = 2
D_FF_EXPERT = (3 * D_MODEL) // 16
MOE_BLOCK = 256
NORM_EPS = 1e-6

kernel_name = 'hybrid_gdn_mlstm_shortconv_hmoe_encoder'


def _rmsnorm(t, w):
    t32 = t.astype(jnp.float32)
    y = t32 * lax.rsqrt(jnp.mean(t32 * t32, axis=-1, keepdims=True) + NORM_EPS)
    return (y * w.astype(jnp.float32)).astype(t.dtype)


def _l2norm(t):
    return t * lax.rsqrt(jnp.sum(t * t, axis=-1, keepdims=True) + NORM_EPS)


def _dwconv_centred(t, w):
    width, ch = w.shape
    return lax.conv_general_dilated(t, w[:, None, :].astype(t.dtype), (1,), [(width // 2, width // 2)],
                                    dimension_numbers=('NWC', 'WIO', 'NWC'), feature_group_count=ch)


def _heads(t, n_heads):
    b, s, _ = t.shape
    return t.reshape(b, s, n_heads, -1).transpose(0, 2, 1, 3)


def _dir_heads(t, n_heads):
    b, s, _ = t.shape
    return t.reshape(b, s, 2, n_heads).transpose(2, 0, 3, 1)


def _to_chunks(t):
    b, h, s = t.shape[:3]
    return jnp.moveaxis(t.reshape(b, h, s // CHUNK, CHUNK, *t.shape[3:]), 2, 0)


def _from_chunks(t):
    t = jnp.moveaxis(t, 0, 2)
    return t.reshape(t.shape[0], t.shape[1], -1, *t.shape[4:])


def _bidirectional(scan_fn, shared, gates):
    flip = lambda t: jnp.flip(t, axis=2)
    fwd = scan_fn(*shared, *[gt[0] for gt in gates])
    bwd = scan_fn(*[flip(t) for t in shared], *[flip(gt[1]) for gt in gates])
    return fwd + flip(bwd)


def _gated_delta_chunked(q, k, v, g, beta):
    bsz, nh, _, dk = q.shape
    dv = v.shape[-1]
    idx = jnp.arange(CHUNK)
    incl = idx[:, None] >= idx[None, :]
    strict = idx[:, None] > idx[None, :]
    eye = jnp.eye(CHUNK, dtype=jnp.float32)

    def step(state, xs):
        qc, kc, vc, gc, bc = xs
        gcum = jnp.cumsum(gc, axis=-1)
        decay = jnp.exp(jnp.where(incl, gcum[..., :, None] - gcum[..., None, :], -jnp.inf))
        a = jnp.where(strict, bc[..., :, None] * jnp.einsum('bhid,bhjd->bhij', kc, kc) * decay, 0.0)
        rhs = jnp.concatenate([bc[..., None] * vc, (bc * jnp.exp(gcum))[..., None] * kc], axis=-1)
        sol = lax.linalg.triangular_solve(a + eye, rhs, left_side=True, lower=True, unit_diagonal=True)
        u, w = sol[..., :dv], sol[..., dv:]
        v_new = u - jnp.einsum('bhck,bhkv->bhcv', w, state)
        attn = jnp.einsum('bhik,bhjk->bhij', qc, kc) * decay
        out = (jnp.einsum('bhck,bhkv->bhcv', qc * jnp.exp(gcum)[..., None], state)
               + jnp.einsum('bhij,bhjv->bhiv', attn, v_new))
        g_last = gcum[..., -1:]
        state = (jnp.exp(g_last)[..., None] * state
                 + jnp.einsum('bhck,bhcv->bhkv', kc * jnp.exp(g_last - gcum)[..., None], v_new))
        return state, out

    state0 = jnp.zeros((bsz, nh, dk, dv), jnp.float32)
    xs = (_to_chunks(q), _to_chunks(k), _to_chunks(v), _to_chunks(g), _to_chunks(beta))
    _, out = lax.scan(step, state0, xs)
    return _from_chunks(out)


def _mlstm_chunked(q, k, v, ig, lf):
    bsz, nh, _, dk = q.shape
    dv = v.shape[-1]
    idx = jnp.arange(CHUNK)
    incl = idx[:, None] >= idx[None, :]

    def step(carry, xs):
        c_state, n_state, m_state = carry
        qc, kc, vc, igc, lfc = xs
        b = jnp.cumsum(lfc, axis=-1)
        dmat = jnp.where(incl, b[..., :, None] - b[..., None, :] + igc[..., None, :], -jnp.inf)
        m_inter = b + m_state[..., None]
        m_pos = jnp.maximum(jnp.max(dmat, axis=-1), m_inter)
        scores = jnp.einsum('bhjd,bhsd->bhjs', qc, kc) * jnp.exp(dmat - m_pos[..., None])
        inter_scale = jnp.exp(m_inter - m_pos)
        num = (jnp.einsum('bhjs,bhsv->bhjv', scores, vc)
               + inter_scale[..., None] * jnp.einsum('bhjd,bhdv->bhjv', qc, c_state))
        den = jnp.sum(scores, axis=-1) + inter_scale * jnp.einsum('bhjd,bhd->bhj', qc, n_state)
        h = num / jnp.maximum(jnp.abs(den), jnp.exp(-m_pos))[..., None]
        b_last = b[..., -1]
        d_end = b_last[..., None] - b + igc
        m_new = jnp.maximum(b_last + m_state, jnp.max(d_end, axis=-1))
        carry_scale = jnp.exp(b_last + m_state - m_new)
        wk = kc * jnp.exp(d_end - m_new[..., None])[..., None]
        c_state = carry_scale[..., None, None] * c_state + jnp.einsum('bhsd,bhsv->bhdv', wk, vc)
        n_state = carry_scale[..., None] * n_state + jnp.sum(wk, axis=2)
        return (c_state, n_state, m_new), h

    carry0 = (jnp.zeros((bsz, nh, dk, dv), jnp.float32), jnp.zeros((bsz, nh, dk), jnp.float32),
              jnp.zeros((bsz, nh), jnp.float32))
    xs = (_to_chunks(q), _to_chunks(k), _to_chunks(v), _to_chunks(ig), _to_chunks(lf))
    _, out = lax.scan(step, carry0, xs)
    return _from_chunks(out)


def _hybrid_mixer(h, w_in, dn_conv_w, dn_a_log, dn_dt_bias, dn_norm_w, ml_igate_b, ml_fgate_b, ml_norm_w,
                  sc_conv_w, w_out):
    f32 = jnp.float32
    bsz, seq, _ = h.shape
    proj = jnp.einsum('bsd,dc->bsc', h, w_in)
    split_at = np.cumsum(IN_SPLITS)[:-1].tolist()
    (dn_qkv, dn_z, dn_b, dn_a, ml_q, ml_k, ml_v, ml_o, ml_i, ml_f,
     sc_b, sc_c, sc_x) = jnp.split(proj, split_at, axis=-1)

    qkv = jax.nn.silu(_dwconv_centred(dn_qkv, dn_conv_w)).astype(f32)
    q, k, v = jnp.split(qkv, 3, axis=-1)
    q = _l2norm(_heads(q, DN_HEADS)) * DN_HEAD_DIM ** -0.5
    k = _l2norm(_heads(k, DN_HEADS))
    v = _heads(v, DN_HEADS)
    beta = jax.nn.sigmoid(_dir_heads(dn_b, DN_HEADS).astype(f32))
    g = -jnp.exp(dn_a_log.astype(f32))[:, None, :, None] * jax.nn.softplus(
        _dir_heads(dn_a, DN_HEADS).astype(f32) + dn_dt_bias.astype(f32)[:, None, :, None])
    dn = jnp.swapaxes(_bidirectional(_gated_delta_chunked, (q, k, v), (g, beta)), 1, 2)
    dn_z = dn_z.reshape(bsz, seq, DN_HEADS, DN_HEAD_DIM).astype(f32)
    dn_out = (_rmsnorm(dn, dn_norm_w) * jax.nn.silu(dn_z)).reshape(bsz, seq, DN_WIDTH)

    mq = _heads(ml_q.astype(f32), ML_HEADS) * ML_QK_DIM ** -0.5
    mk = _heads(ml_k.astype(f32), ML_HEADS)
    mv = _heads(ml_v.astype(f32), ML_HEADS)
    ig = _dir_heads(ml_i, ML_HEADS).astype(f32) + ml_igate_b.astype(f32)[:, None, :, None]
    lf = jax.nn.log_sigmoid(_dir_heads(ml_f, ML_HEADS).astype(f32) + ml_fgate_b.astype(f32)[:, None, :, None])
    hm = jnp.swapaxes(_bidirectional(_mlstm_chunked, (mq, mk, mv), (ig, lf)), 1, 2)
    ml_gate = jax.nn.sigmoid(ml_o.astype(f32)).reshape(bsz, seq, ML_HEADS, ML_V_DIM)
    ml_out = (_rmsnorm(hm, ml_norm_w.reshape(ML_HEADS, ML_V_DIM)) * ml_gate).reshape(bsz, seq, ML_WIDTH)

    sc_out = sc_b * _dwconv_centred(sc_c * sc_x, sc_conv_w)

    mixed = jnp.concatenate([dn_out.astype(h.dtype), ml_out.astype(h.dtype), sc_out], axis=-1)
    return jnp.einsum('bsc,cd->bsd', mixed, w_out)


def _hier_moe(h, rg_w, rg_b, re_w, re_b, w_gate, w_up, w_down):
    bsz, seq, d = h.shape
    n_tok = bsz * seq
    xf = h.reshape(n_tok, d)
    g_prob = jax.nn.softmax((xf @ rg_w + rg_b).astype(jnp.float32), axis=-1)
    g_idx = jnp.argmax(g_prob, axis=-1)
    g_w = jnp.take_along_axis(g_prob, g_idx[:, None], axis=-1)
    e_logits = (xf @ re_w + re_b).astype(jnp.float32).reshape(n_tok, N_GROUPS, EXPERTS_PER_GROUP)
    e_logits = jnp.take_along_axis(e_logits, g_idx[:, None, None], axis=1)[:, 0]
    top_p, top_e = lax.top_k(jax.nn.softmax(e_logits, axis=-1), TOP_K)
    comb_w = g_w * top_p / jnp.sum(top_p, axis=-1, keepdims=True)
    expert_id = (g_idx[:, None] * EXPERTS_PER_GROUP + top_e).reshape(-1).astype(jnp.int32)
    tok_id = jnp.repeat(jnp.arange(n_tok, dtype=jnp.int32), TOP_K)
    flat_w = comb_w.reshape(-1)
    n_assign = n_tok * TOP_K
    order = jnp.argsort(expert_id)
    s_exp, s_tok, s_w = expert_id[order], tok_id[order], flat_w[order]
    counts = jnp.bincount(expert_id, length=N_EXPERTS)
    start = jnp.cumsum(counts) - counts
    padded = (counts + MOE_BLOCK - 1) // MOE_BLOCK * MOE_BLOCK
    padded_end = jnp.cumsum(padded)
    slot = padded_end[s_exp] - padded[s_exp] + jnp.arange(n_assign, dtype=jnp.int32) - start[s_exp]
    n_blocks = -(-(n_assign + N_EXPERTS * (MOE_BLOCK - 1)) // MOE_BLOCK)
    n_slots = n_blocks * MOE_BLOCK
    slot_tok = jnp.full((n_slots,), n_tok, jnp.int32).at[slot].set(s_tok)
    slot_w = jnp.zeros((n_slots,), jnp.float32).at[slot].set(s_w)
    block_exp = jnp.minimum(jnp.searchsorted(padded_end, jnp.arange(n_blocks, dtype=jnp.int32) * MOE_BLOCK,
                                             side='right'), N_EXPERTS - 1)
    x_pad = jnp.concatenate([xf, jnp.zeros((1, d), xf.dtype)], axis=0)

    def expert_block(args):
        toks, e = args
        xb = x_pad[toks]
        return (jax.nn.silu(xb @ w_gate[e]) * (xb @ w_up[e])) @ w_down[e]

    y_slots = lax.map(expert_block, (slot_tok.reshape(n_blocks, MOE_BLOCK), block_exp))
    y = jax.ops.segment_sum(y_slots.reshape(n_slots, d).astype(jnp.float32) * slot_w[:, None], slot_tok,
                            num_segments=n_tok + 1)
    return y[:n_tok].astype(h.dtype).reshape(bsz, seq, d)


def setup_inputs(seed: int = 0) -> dict:
    key = jax.random.key(seed)
    ks = jax.random.split(key, 24)
    f32 = jnp.float32
    nl = DEPTH

    def nrm(k, shape, scale):
        return jax.random.normal(k, shape, f32) * scale

    def gain(k, shape):
        return 1.0 + 0.02 * jax.random.normal(k, shape, f32)

    x = nrm(ks[0], (BATCH, SEQ, D_MODEL), 1.0)
    norm_mix_w = gain(ks[1], (nl, D_MODEL))
    w_in = nrm(ks[2], (nl, D_MODEL, IN_COLS), D_MODEL ** -0.5)
    dn_conv_w = nrm(ks[3], (nl, DN_CONV, 3 * DN_WIDTH), DN_CONV ** -0.5)
    dn_a_log = jnp.log(jax.random.uniform(ks[4], (nl, 2, DN_HEADS), f32, 1.0, 16.0))
    dt = jnp.exp(jax.random.uniform(ks[5], (nl, 2, DN_HEADS), f32, math.log(1e-3), math.log(1e-1)))
    dn_dt_bias = dt + jnp.log(-jnp.expm1(-dt))
    dn_norm_w = gain(ks[6], (nl, DN_HEAD_DIM))
    ml_igate_b = nrm(ks[7], (nl, 2, ML_HEADS), 0.1)
    ml_fgate_b = jnp.linspace(3.0, 6.0, ML_HEADS, dtype=f32) + nrm(ks[8], (nl, 2, ML_HEADS), 0.1)
    ml_norm_w = gain(ks[9], (nl, ML_WIDTH))
    sc_conv_w = nrm(ks[10], (nl, SC_CONV, SC_WIDTH), SC_CONV ** -0.5)
    w_out = nrm(ks[11], (nl, D_MIX, D_MODEL), D_MIX ** -0.5)
    norm_ffn_w = gain(ks[12], (nl, D_MODEL))
    router_group_w = nrm(ks[13], (nl, D_MODEL, N_GROUPS), D_MODEL ** -0.5)
    router_group_b = nrm(ks[14], (nl, N_GROUPS), 0.01)
    router_expert_w = nrm(ks[15], (nl, D_MODEL, N_EXPERTS), D_MODEL ** -0.5)
    router_expert_b = nrm(ks[16], (nl, N_EXPERTS), 0.01)
    expert_w_gate = nrm(ks[17], (nl, N_EXPERTS, D_MODEL, D_FF_EXPERT), D_MODEL ** -0.5)
    expert_w_up = nrm(ks[18], (nl, N_EXPERTS, D_MODEL, D_FF_EXPERT), D_MODEL ** -0.5)
    expert_w_down = nrm(ks[19], (nl, N_EXPERTS, D_FF_EXPERT, D_MODEL), D_FF_EXPERT ** -0.5)
    final_norm_w = gain(ks[20], (D_MODEL,))
    return dict(x=x, norm_mix_w=norm_mix_w, w_in=w_in, dn_conv_w=dn_conv_w, dn_a_log=dn_a_log,
                dn_dt_bias=dn_dt_bias, dn_norm_w=dn_norm_w, ml_igate_b=ml_igate_b, ml_fgate_b=ml_fgate_b,
                ml_norm_w=ml_norm_w, sc_conv_w=sc_conv_w, w_out=w_out, norm_ffn_w=norm_ffn_w,
                router_group_w=router_group_w, router_group_b=router_group_b,
                router_expert_w=router_expert_w, router_expert_b=router_expert_b,
                expert_w_gate=expert_w_gate, expert_w_up=expert_w_up, expert_w_down=expert_w_down,
                final_norm_w=final_norm_w)


def reference(x, norm_mix_w, w_in, dn_conv_w, dn_a_log, dn_dt_bias, dn_norm_w, ml_igate_b, ml_fgate_b,
              ml_norm_w, sc_conv_w, w_out, norm_ffn_w, router_group_w, router_group_b, router_expert_w,
              router_expert_b, expert_w_gate, expert_w_up, expert_w_down, final_norm_w):
    for layer in range(DEPTH):
        h = _rmsnorm(x, norm_mix_w[layer])
        x = x + _hybrid_mixer(h, w_in[layer], dn_conv_w[layer], dn_a_log[layer], dn_dt_bias[layer],
                              dn_norm_w[layer], ml_igate_b[layer], ml_fgate_b[layer], ml_norm_w[layer],
                              sc_conv_w[layer], w_out[layer])
        h = _rmsnorm(x, norm_ffn_w[layer])
        x = x + _hier_moe(h, router_group_w[layer], router_group_b[layer], router_expert_w[layer],
                          router_expert_b[layer], expert_w_gate[layer], expert_w_up[layer], expert_w_down[layer])
    return _rmsnorm(x, final_norm_w)
```

```python
import functools

import numpy as np
import jax
import jax.numpy as jnp
from jax import lax
from jax.experimental import pallas as pl
from jax.experimental.pallas import tpu as pltpu

F32 = jnp.float32
BF16 = jnp.bfloat16
I32 = jnp.int32
U32 = jnp.uint32

D_MODEL = 4096
DN_HEADS = 12
DN_DIM = 128
DN_WIDTH = DN_HEADS * DN_DIM
DN_CONV = 5
ML_HEADS = 6
ML_QK = 128
ML_V = 256
ML_WIDTH = ML_HEADS * ML_V
SC_WIDTH = 1024
SC_CONV = 3
N_GROUPS = 4
EXPERTS_PER_GROUP = 8
N_EXPERTS = N_GROUPS * EXPERTS_PER_GROUP
D_FF = 768
EPS = 1e-6

LANES = 128
SUBLANES = 8
VMEM_LIMIT = 52 * 1024 * 1024

CH = 64
TL = 256
CPT = TL // CH
N_HEADS_ALL = DN_HEADS + ML_HEADS
MOE_BLK = 256
NEG = -1e30

OFF_DN_Q, OFF_DN_K, OFF_DN_V, OFF_DN_Z = 0, 1536, 3072, 4608
OFF_ML_Q, OFF_ML_K, OFF_ML_V, OFF_ML_O = 6144, 6912, 7680, 9216
OFF_SC_B, OFF_SC_C, OFF_SC_X = 10752, 11776, 12800
N_MAIN = 13824


def _cp(sem, vmem=VMEM_LIMIT):
    return pltpu.CompilerParams(dimension_semantics=sem, vmem_limit_bytes=vmem)


def _mm(a, b):
    return jnp.dot(a.astype(BF16), b.astype(BF16), preferred_element_type=F32)


def _mm_nt(a, b):
    return lax.dot_general(a.astype(BF16), b.astype(BF16), (((1,), (1,)), ((), ())),
                           preferred_element_type=F32)


def _mm_f32(a, b):
    return jnp.dot(a, b, preferred_element_type=F32, precision=lax.Precision.HIGHEST)


def _sigmoid(x):
    return 1.0 / (1.0 + jnp.exp(-x))


def _softplus(x):
    return jnp.maximum(x, 0.0) + jnp.log(1.0 + jnp.exp(-jnp.abs(x)))


def _rms_kernel(x_ref, w_ref, o_ref):
    x = x_ref[...]
    ms = jnp.mean(x * x, axis=-1, keepdims=True)
    o_ref[...] = (x * lax.rsqrt(ms + EPS) * w_ref[...]).astype(o_ref.dtype)


def _rmsnorm(x2d, w, out_dtype, tm=256):
    t, d = x2d.shape
    return pl.pallas_call(
        _rms_kernel,
        out_shape=jax.ShapeDtypeStruct((t, d), out_dtype),
        grid=(t // tm,),
        in_specs=[pl.BlockSpec((tm, d), lambda i: (i, 0)), pl.BlockSpec((1, d), lambda i: (0, 0))],
        out_specs=pl.BlockSpec((tm, d), lambda i: (i, 0)),
        compiler_params=_cp(("arbitrary",)),
        name="rmsnorm",
    )(x2d, w.reshape(1, d))


def _inproj_kernel(a_ref, w_ref, wg_ref, o_ref, g_ref):
    a = a_ref[...]
    o_ref[...] = jnp.dot(a, w_ref[...], preferred_element_type=F32).astype(o_ref.dtype)

    @pl.when(pl.program_id(1) == 0)
    def _():
        g_ref[...] = jnp.dot(a, wg_ref[...], preferred_element_type=F32)


def _inproj(h, w_main, w_gate, tm=1024, tn=512):
    t, d = h.shape
    n = w_main.shape[1]
    ng = w_gate.shape[1]
    return pl.pallas_call(
        _inproj_kernel,
        out_shape=(jax.ShapeDtypeStruct((t, n), BF16), jax.ShapeDtypeStruct((t, ng), F32)),
        grid=(t // tm, n // tn),
        in_specs=[pl.BlockSpec((tm, d), lambda i, j: (i, 0)),
                  pl.BlockSpec((d, tn), lambda i, j: (0, j)),
                  pl.BlockSpec((d, ng), lambda i, j: (0, 0))],
        out_specs=(pl.BlockSpec((tm, tn), lambda i, j: (i, j)),
                   pl.BlockSpec((tm, ng), lambda i, j: (i, 0))),
        compiler_params=_cp(("arbitrary", "arbitrary")),
        name="inproj",
    )(h, w_main, w_gate)


def _gate_kernel(raw_ref, p_ref, gc_ref, gr_ref):
    x = raw_ref[...]
    lane = lax.broadcasted_iota(I32, x.shape, 1)
    chn = lane & 7
    is_ml = lane >= LANES
    a_log, dt_b, ig_b, fg_b = p_ref[0:1, :], p_ref[1:2, :], p_ref[2:3, :], p_ref[3:4, :]
    beta = _sigmoid(x)
    g = -jnp.exp(a_log) * _softplus(x + dt_b)
    ig = x + ig_b
    lf = -_softplus(-(x + fg_b))
    lo = chn < 2
    val = jnp.where(is_ml, jnp.where(lo, ig, lf), jnp.where(lo, beta, g))
    val = jnp.where(chn < 4, val, 0.0)
    ri = lax.broadcasted_iota(I32, (TL, TL), 0)
    ci = lax.broadcasted_iota(I32, (TL, TL), 1)
    same = (ri >> 6) == (ci >> 6)
    m_f = jnp.where(same & (ri >= ci), 1.0, 0.0)
    m_b = jnp.where(same & (ri <= ci), 1.0, 0.0)
    m_t = jnp.where(same, 1.0, 0.0)
    cum_f = _mm_f32(m_f, val)
    cum_b = _mm_f32(m_b, val)
    tot = pltpu.roll(_mm_f32(m_t, val), 2, axis=1)
    out = jnp.where(chn == 2, cum_f, jnp.where(chn == 3, cum_b, jnp.where((chn == 4) | (chn == 5), tot, val)))
    gr_ref[0] = out.T
    for hh in range(N_HEADS_ALL):
        grp, h = (0, hh) if hh < DN_HEADS else (1, hh - DN_HEADS)
        v = out[:, grp * LANES:(grp + 1) * LANES]
        sh = (LANES - 8 * h) % LANES
        gc_ref[hh] = pltpu.roll(v, sh, axis=1) if sh else v


def _gate_prep(raw, params):
    t = raw.shape[0]
    nt = t // TL
    return pl.pallas_call(
        _gate_kernel,
        out_shape=(jax.ShapeDtypeStruct((N_HEADS_ALL, t, LANES), F32),
                   jax.ShapeDtypeStruct((nt, 2 * LANES, TL), F32)),
        grid=(nt,),
        in_specs=[pl.BlockSpec((TL, 2 * LANES), lambda i: (i, 0)),
                  pl.BlockSpec((8, 2 * LANES), lambda i: (0, 0))],
        out_specs=(pl.BlockSpec((N_HEADS_ALL, TL, LANES), lambda i: (0, i, 0)),
                   pl.BlockSpec((1, 2 * LANES, TL), lambda i: (i, 0, 0))),
        compiler_params=_cp(("arbitrary",)),
        name="gate_prep",
    )(raw, params)


def _conv_centred(x_ref, w_ref, pad_ref, width):
    s = x_ref.shape[0]
    half = width // 2
    pad_ref[0:8, :] = jnp.zeros((8, pad_ref.shape[1]), F32)
    pad_ref[s + 8:s + 16, :] = jnp.zeros((8, pad_ref.shape[1]), F32)
    pad_ref[8:s + 8, :] = x_ref[...].astype(F32)
    acc = None
    for i in range(width):
        term = w_ref[i:i + 1, :] * pad_ref[8 - half + i:8 - half + i + s, :]
        acc = term if acc is None else acc + term
    return acc


def _tile_masks():
    ri = lax.broadcasted_iota(I32, (TL, TL), 0)
    ci = lax.broadcasted_iota(I32, (TL, TL), 1)
    return ri, ci


def _unit_tri_inverse(a, same16, same32, eye):
    a16 = jnp.where(same16, a, 0.0)
    b1 = _mm(a16, a16)
    b2 = _mm(b1, b1)
    b3 = _mm(b2, b2)
    p = eye - a16
    p = p + _mm(p, b1)
    p = p + _mm(p, b2)
    p = p + _mm(p, b3)
    a32 = jnp.where(same32 & jnp.logical_not(same16), a, 0.0)
    p = p - _mm(_mm(p, a32), p)
    a64 = jnp.where(same32, 0.0, a)
    p = p - _mm(_mm(p, a64), p)
    return p


def _dn_kernel(q_ref, k_ref, v_ref, z_ref, cq_ref, ck_ref, cv_ref, gc_ref, gr_ref, nw_ref, o_ref,
               pad_ref, qn_ref, kn_ref, vv_ref, u_ref, wq_ref, ak_ref, eg_ref, oo_ref):
    s = q_ref.shape[0]
    nt = s // TL
    nc = s // CH
    scale = DN_DIM ** -0.5

    q = _conv_centred(q_ref, cq_ref, pad_ref, DN_CONV)
    q = q * _sigmoid(q)
    qn_ref[...] = q * (lax.rsqrt(jnp.sum(q * q, axis=-1, keepdims=True) + EPS) * scale)
    k = _conv_centred(k_ref, ck_ref, pad_ref, DN_CONV)
    k = k * _sigmoid(k)
    kn_ref[...] = k * lax.rsqrt(jnp.sum(k * k, axis=-1, keepdims=True) + EPS)
    v = _conv_centred(v_ref, cv_ref, pad_ref, DN_CONV)
    vv_ref[...] = v * _sigmoid(v)

    def tile_body(i, carry):
        r0 = pl.multiple_of(i * TL, TL)
        ri, ci = _tile_masks()
        same64 = (ri >> 6) == (ci >> 6)
        same32 = (ri >> 5) == (ci >> 5)
        same16 = (ri >> 4) == (ci >> 4)
        eye = jnp.where(ri == ci, 1.0, 0.0)
        qn = qn_ref[pl.ds(r0, TL), :]
        kn = kn_ref[pl.ds(r0, TL), :]
        vv = vv_ref[pl.ds(r0, TL), :]
        kk = _mm_nt(kn, kn)
        qk = _mm_nt(qn, kn)
        gc = gc_ref[0, pl.ds(r0, TL), :]
        gr = gr_ref[i]
        for d in range(2):
            beta_c = gc[:, d:d + 1]
            g_c = gc[:, 2 + d:3 + d]
            gt_c = gc[:, 4 + d:5 + d]
            g_r = gr[2 + d:3 + d, :]
            incl = same64 & ((ri >= ci) if d == 0 else (ri <= ci))
            strict = same64 & ((ri > ci) if d == 0 else (ri < ci))
            dec = jnp.exp(jnp.where(incl, g_c - g_r, NEG))
            a = jnp.where(strict, beta_c * kk * dec, 0.0)
            tinv = _unit_tri_inverse(a, same16, same32, eye)
            eg_c = jnp.exp(g_c)
            rhs = jnp.concatenate([beta_c * vv, (beta_c * eg_c) * kn], axis=1)
            uw = _mm(tinv, rhs)
            u_ref[d, pl.ds(r0, TL), :] = uw[:, :DN_DIM]
            w = uw[:, DN_DIM:]
            qe = qn * eg_c
            attn = qk * dec
            kdt = (kn * jnp.exp(gt_c - g_c)).T
            egt = jnp.exp(gt_c)
            for c in range(CPT):
                rows = slice(c * CH, (c + 1) * CH)
                wq_ref[d, i * CPT + c] = jnp.concatenate([w[rows], qe[rows]], axis=0).astype(BF16)
                ak_ref[d, i * CPT + c] = jnp.concatenate([attn[rows, rows], kdt[:, rows]], axis=0).astype(BF16)
                eg_ref[d, i * CPT + c] = jnp.broadcast_to(egt[c * CH:c * CH + 8, :], (8, LANES))
        return carry

    lax.fori_loop(0, nt, tile_body, 0)

    def chunk_body(c, carry):
        states = list(carry)
        for d in range(2):
            cc = c if d == 0 else nc - 1 - c
            r = pl.multiple_of(cc * CH, CH)
            st = states[d]
            ws = jnp.dot(wq_ref[d, cc], st.astype(BF16), preferred_element_type=F32)
            vnew = u_ref[d, pl.ds(r, CH), :] - ws[:CH]
            m2 = jnp.dot(ak_ref[d, cc], vnew.astype(BF16), preferred_element_type=F32)
            oo_ref[d, pl.ds(r, CH), :] = ws[CH:] + m2[:CH]
            states[d] = eg_ref[d, cc][0:1, :] * st + m2[CH:]
        return tuple(states)

    z0 = jnp.zeros((DN_DIM, DN_DIM), F32)
    lax.fori_loop(0, nc, chunk_body, (z0, z0))

    o = oo_ref[0] + oo_ref[1]
    o = o * lax.rsqrt(jnp.mean(o * o, axis=-1, keepdims=True) + EPS) * nw_ref[...]
    z = z_ref[...].astype(F32)
    o_ref[...] = (o * (z * _sigmoid(z))).astype(o_ref.dtype)


def _deltanet(proj, conv_w, gc, gr, norm_w, batch, seq):
    t = proj.shape[0]
    nc = seq // CH
    nt = seq // TL
    qb, kb, vb, zb = (OFF_DN_Q // DN_DIM, OFF_DN_K // DN_DIM, OFF_DN_V // DN_DIM, OFF_DN_Z // DN_DIM)
    col = lambda off: pl.BlockSpec((seq, DN_DIM), lambda b, h, off=off: (b, off + h))
    cw = lambda off: pl.BlockSpec((DN_CONV, DN_DIM), lambda b, h, off=off: (0, off + h))
    return pl.pallas_call(
        _dn_kernel,
        out_shape=jax.ShapeDtypeStruct((t, DN_WIDTH), BF16),
        grid=(batch, DN_HEADS),
        in_specs=[col(qb), col(kb), col(vb), col(zb), cw(0), cw(DN_HEADS), cw(2 * DN_HEADS),
                  pl.BlockSpec((1, seq, LANES), lambda b, h: (h, b, 0)),
                  pl.BlockSpec((nt, 8, TL), lambda b, h: (b, h, 0)),
                  pl.BlockSpec((1, DN_DIM), lambda b, h: (0, 0))],
        out_specs=pl.BlockSpec((seq, DN_DIM), lambda b, h: (b, h)),
        scratch_shapes=[pltpu.VMEM((seq + 16, DN_DIM), F32),
                        pltpu.VMEM((seq, DN_DIM), F32), pltpu.VMEM((seq, DN_DIM), F32),
                        pltpu.VMEM((seq, DN_DIM), F32),
                        pltpu.VMEM((2, seq, DN_DIM), F32),
                        pltpu.VMEM((2, nc, 2 * CH, DN_DIM), BF16),
                        pltpu.VMEM((2, nc, CH + DN_DIM, CH), BF16),
                        pltpu.VMEM((2, nc, 8, LANES), F32),
                        pltpu.VMEM((2, seq, DN_DIM), F32)],
        compiler_params=_cp(("arbitrary", "arbitrary")),
        name="deltanet",
    )(proj, proj, proj, proj, conv_w, conv_w, conv_w, gc, gr, norm_w.reshape(1, DN_DIM))


def _ml_kernel(q_ref, k_ref, v_ref, og_ref, gc_ref, gr_ref, nw_ref, o_ref,
               ak_ref, rs_ref, cs_ref, dn_ref, oo_ref):
    s = q_ref.shape[0]
    nt = s // TL
    nc = s // CH
    scale = ML_QK ** -0.5

    def tile_body(it, carry):
        m_carry = list(carry)
        ri, ci = _tile_masks()
        same64 = (ri >> 6) == (ci >> 6)
        rowc = lax.broadcasted_iota(I32, (TL, 1), 0) >> 6
        for d in range(2):
            i = it if d == 0 else nt - 1 - it
            r0 = pl.multiple_of(i * TL, TL)
            qs = q_ref[pl.ds(r0, TL), :].astype(F32) * scale
            kf = k_ref[pl.ds(r0, TL), :].astype(F32)
            qk = _mm_nt(qs, kf)
            gc = gc_ref[0, pl.ds(r0, TL), :]
            gr = gr_ref[i]
            ig_c = gc[:, d:d + 1]
            b_c = gc[:, 2 + d:3 + d]
            bt_c = gc[:, 4 + d:5 + d]
            ig_r = gr[d:d + 1, :]
            b_r = gr[2 + d:3 + d, :]
            bt_r = gr[4 + d:5 + d, :]
            incl = same64 & ((ri >= ci) if d == 0 else (ri <= ci))
            dmat = jnp.where(incl, b_c - b_r + ig_r, NEG)
            m_intra = jnp.max(dmat, axis=1, keepdims=True)
            d_end_c = bt_c - b_c + ig_c
            d_end_r = bt_r - b_r + ig_r
            dmax_c = jnp.max(jnp.where(same64, d_end_r, NEG), axis=1, keepdims=True)
            m_in_c = jnp.zeros((TL, 1), F32)
            m_out_c = jnp.zeros((TL, 1), F32)
            m = m_carry[d]
            for cc in (range(CPT) if d == 0 else range(CPT - 1, -1, -1)):
                btot = bt_c[cc * CH:cc * CH + 1, :]
                dmx = dmax_c[cc * CH:cc * CH + 1, :]
                m_new = jnp.maximum(btot + m, dmx)
                m_in_c = jnp.where(rowc == cc, m, m_in_c)
                m_out_c = jnp.where(rowc == cc, m_new, m_out_c)
                m = m_new
            m_carry[d] = m
            m_inter = b_c + m_in_c
            m_pos = jnp.maximum(m_intra, m_inter)
            scores = qk * jnp.exp(dmat - m_pos)
            inter_scale = jnp.exp(m_inter - m_pos)
            rsum = jnp.sum(scores, axis=1, keepdims=True)
            einv = jnp.exp(-m_pos)
            cscale = jnp.exp(bt_c + m_in_c - m_out_c)
            wk = kf * jnp.exp(d_end_c - m_out_c)
            wkt = wk.T
            lane = lax.broadcasted_iota(I32, (TL, LANES), 1)
            rs_ref[d, pl.ds(r0, TL), :] = jnp.where(lane == 0, inter_scale,
                                                   jnp.where(lane == 1, rsum, einv))
            for c in range(CPT):
                rows = slice(c * CH, (c + 1) * CH)
                ak_ref[d, i * CPT + c] = jnp.concatenate([scores[rows, rows], wkt[:, rows]], axis=0).astype(BF16)
                cs_ref[d, i * CPT + c] = jnp.broadcast_to(cscale[c * CH:c * CH + 8, :], (8, LANES))
                dn_ref[d, i * CPT + c] = jnp.broadcast_to(jnp.sum(wk[rows], axis=0, keepdims=True), (8, ML_QK))
        return tuple(m_carry)

    zero11 = jnp.zeros((1, 1), F32)
    lax.fori_loop(0, nt, tile_body, (zero11, zero11))

    def chunk_body(c, carry):
        cst = [carry[0], carry[1]]
        nst = [carry[2], carry[3]]
        for d in range(2):
            cc = c if d == 0 else nc - 1 - c
            r = pl.multiple_of(cc * CH, CH)
            qb = q_ref[pl.ds(r, CH), :]
            inter = jnp.dot(qb, cst[d].astype(BF16), preferred_element_type=F32) * scale
            m2 = jnp.dot(ak_ref[d, cc], v_ref[pl.ds(r, CH), :], preferred_element_type=F32)
            qn = jnp.sum(qb.astype(F32) * nst[d], axis=1, keepdims=True) * scale
            rs = rs_ref[d, pl.ds(r, CH), :]
            isc = rs[:, 0:1]
            num = m2[:CH] + isc * inter
            den = rs[:, 1:2] + isc * qn
            oo_ref[d, pl.ds(r, CH), :] = num / jnp.maximum(jnp.abs(den), rs[:, 2:3])
            csc = cs_ref[d, cc][0:1, 0:1]
            cst[d] = csc * cst[d] + m2[CH:]
            nst[d] = csc * nst[d] + dn_ref[d, cc][0:1, :]
        return (cst[0], cst[1], nst[0], nst[1])

    c0 = jnp.zeros((ML_QK, ML_V), F32)
    n0 = jnp.zeros((1, ML_QK), F32)
    lax.fori_loop(0, nc, chunk_body, (c0, c0, n0, n0))

    o = oo_ref[0] + oo_ref[1]
    o = o * lax.rsqrt(jnp.mean(o * o, axis=-1, keepdims=True) + EPS) * nw_ref[...]
    o_ref[...] = (o * _sigmoid(og_ref[...].astype(F32))).astype(o_ref.dtype)


def _mlstm(proj, gc, gr, norm_w, batch, seq):
    t = proj.shape[0]
    nc = seq // CH
    nt = seq // TL
    qb, kb, vb, ob = OFF_ML_Q // ML_QK, OFF_ML_K // ML_QK, OFF_ML_V // ML_V, OFF_ML_O // ML_V
    return pl.pallas_call(
        _ml_kernel,
        out_shape=jax.ShapeDtypeStruct((t, ML_WIDTH), BF16),
        grid=(batch, ML_HEADS),
        in_specs=[pl.BlockSpec((seq, ML_QK), lambda b, h: (b, qb + h)),
                  pl.BlockSpec((seq, ML_QK), lambda b, h: (b, kb + h)),
                  pl.BlockSpec((seq, ML_V), lambda b, h: (b, vb + h)),
                  pl.BlockSpec((seq, ML_V), lambda b, h: (b, ob + h)),
                  pl.BlockSpec((1, seq, LANES), lambda b, h: (DN_HEADS + h, b, 0)),
                  pl.BlockSpec((nt, 8, TL), lambda b, h: (b, LANES // 8 + h, 0)),
                  pl.BlockSpec((1, ML_V), lambda b, h: (0, h))],
        out_specs=pl.BlockSpec((seq, ML_V), lambda b, h: (b, h)),
        scratch_shapes=[pltpu.VMEM((2, nc, CH + ML_QK, CH), BF16),
                        pltpu.VMEM((2, seq, LANES), F32),
                        pltpu.VMEM((2, nc, 8, LANES), F32),
                        pltpu.VMEM((2, nc, 8, ML_QK), F32),
                        pltpu.VMEM((2, seq, ML_V), F32)],
        compiler_params=_cp(("arbitrary", "arbitrary")),
        name="mlstm",
    )(proj, proj, proj, proj, gc, gr, norm_w.reshape(1, ML_WIDTH))


def _sc_kernel(b_ref, c_ref, x_ref, w_ref, o_ref, cx_ref, pad_ref):
    cx_ref[...] = c_ref[...].astype(F32) * x_ref[...].astype(F32)
    y = _conv_centred(cx_ref, w_ref, pad_ref, SC_CONV)
    o_ref[...] = (b_ref[...].astype(F32) * y).astype(o_ref.dtype)


def _shortconv(proj, conv_w, batch, seq, wc=256):
    t = proj.shape[0]
    bb, cb, xb = OFF_SC_B // wc, OFF_SC_C // wc, OFF_SC_X // wc
    col = lambda off: pl.BlockSpec((seq, wc), lambda b, j, off=off: (b, off + j))
    return pl.pallas_call(
        _sc_kernel,
        out_shape=jax.ShapeDtypeStruct((t, SC_WIDTH), BF16),
        grid=(batch, SC_WIDTH // wc),
        in_specs=[col(bb), col(cb), col(xb), pl.BlockSpec((SC_CONV, wc), lambda b, j: (0, j))],
        out_specs=pl.BlockSpec((seq, wc), lambda b, j: (b, j)),
        scratch_shapes=[pltpu.VMEM((seq, wc), F32), pltpu.VMEM((seq + 16, wc), F32)],
        compiler_params=_cp(("arbitrary", "arbitrary")),
        name="shortconv",
    )(proj, proj, proj, conv_w)


def _pack_gate_cols(dn_b, dn_a, ml_i, ml_f):
    r = dn_b.shape[0]

    def grp(lo, hi, nh):
        a = jnp.stack([lo[:, :nh], lo[:, nh:], hi[:, :nh], hi[:, nh:]], axis=-1)
        a = jnp.concatenate([a, jnp.zeros((r, nh, 4), a.dtype)], axis=-1).reshape(r, nh * 8)
        return jnp.concatenate([a, jnp.zeros((r, LANES - nh * 8), a.dtype)], axis=-1)

    return jnp.concatenate([grp(dn_b, dn_a, DN_HEADS), grp(ml_i, ml_f, ML_HEADS)], axis=-1)


def _gate_params(dn_a_log, dn_dt_bias, ml_igate_b, ml_fgate_b):
    z_dn = jnp.zeros((1, 2 * DN_HEADS), F32)
    z_ml = jnp.zeros((1, 2 * ML_HEADS), F32)
    flat = lambda a: a.astype(F32).reshape(1, -1)
    rows = [_pack_gate_cols(z_dn, flat(dn_a_log), z_ml, z_ml),
            _pack_gate_cols(z_dn, flat(dn_dt_bias), z_ml, z_ml),
            _pack_gate_cols(z_dn, z_dn, flat(ml_igate_b), z_ml),
            _pack_gate_cols(z_dn, z_dn, z_ml, flat(ml_fgate_b))]
    return jnp.concatenate(rows + [jnp.zeros((4, 2 * LANES), F32)], axis=0)


def _outproj_kernel(a1_ref, a2_ref, a3_ref, w1_ref, w2_ref, w3_ref, x_ref, o_ref):
    acc = jnp.dot(a1_ref[...], w1_ref[...], preferred_element_type=F32)
    acc += jnp.dot(a2_ref[...], w2_ref[...], preferred_element_type=F32)
    acc += jnp.dot(a3_ref[...], w3_ref[...], preferred_element_type=F32)
    o_ref[...] = x_ref[...] + acc


def _outproj(dn, ml, sc, w_out, x, tm=1024, tn=512):
    t, d = x.shape
    k1, k2, k3 = dn.shape[1], ml.shape[1], sc.shape[1]
    assert k1 == k2 and (k1 + k2) % k3 == 0
    return pl.pallas_call(
        _outproj_kernel,
        out_shape=jax.ShapeDtypeStruct((t, d), F32),
        grid=(t // tm, d // tn),
        in_specs=[pl.BlockSpec((tm, k1), lambda i, j: (i, 0)),
                  pl.BlockSpec((tm, k2), lambda i, j: (i, 0)),
                  pl.BlockSpec((tm, k3), lambda i, j: (i, 0)),
                  pl.BlockSpec((k1, tn), lambda i, j: (0, j)),
                  pl.BlockSpec((k2, tn), lambda i, j: (1, j)),
                  pl.BlockSpec((k3, tn), lambda i, j: ((k1 + k2) // k3, j)),
                  pl.BlockSpec((tm, tn), lambda i, j: (i, j))],
        out_specs=pl.BlockSpec((tm, tn), lambda i, j: (i, j)),
        compiler_params=_cp(("arbitrary", "arbitrary")),
        name="outproj",
    )(dn, ml, sc, w_out, w_out, w_out, x)


ROUTE_TM = 256
EXP_ROW0 = 32
HALF_D = D_MODEL // 2
PK = HALF_D // LANES


def _router_kernel(x_ref, nw_ref, wr_ref, br_ref, hp_ref, meta_ref, wc_ref, cnt_ref, base_ref):
    tm = x_ref.shape[0]

    @pl.when(pl.program_id(0) == 0)
    def _():
        base_ref[...] = jnp.zeros(base_ref.shape, F32)

    x = x_ref[...]
    h = x * lax.rsqrt(jnp.mean(x * x, axis=-1, keepdims=True) + EPS) * nw_ref[...]
    bits = pltpu.bitcast(h.astype(BF16).astype(F32), U32)
    packed = (bits[:, :HALF_D] >> 16) | bits[:, HALF_D:]
    for s in range(PK):
        hp_ref[:, s, :] = packed[:, s * LANES:(s + 1) * LANES]

    lt = (_mm_f32(h, wr_ref[...]) + br_ref[...]).T
    row8 = lax.broadcasted_iota(I32, (8, tm), 0)
    gl = jnp.where(row8 < N_GROUPS, lt[0:8, :], NEG)
    gex = jnp.exp(gl - jnp.max(gl, axis=0, keepdims=True))
    gp = gex / jnp.sum(gex, axis=0, keepdims=True)
    g_w = jnp.max(gp, axis=0, keepdims=True)
    g_idx = jnp.min(jnp.where(gp == g_w, row8, 8), axis=0, keepdims=True)
    el = lt[EXP_ROW0 + 24:EXP_ROW0 + 32, :]
    for g in (2, 1, 0):
        el = jnp.where(g_idx == g, lt[EXP_ROW0 + 8 * g:EXP_ROW0 + 8 * g + 8, :], el)
    ee = jnp.exp(el - jnp.max(el, axis=0, keepdims=True))
    p = ee / jnp.sum(ee, axis=0, keepdims=True)
    p1 = jnp.max(p, axis=0, keepdims=True)
    i1 = jnp.min(jnp.where(p == p1, row8, 8), axis=0, keepdims=True)
    pm = jnp.where(row8 == i1, -1.0, p)
    p2 = jnp.max(pm, axis=0, keepdims=True)
    i2 = jnp.min(jnp.where(pm == p2, row8, 8), axis=0, keepdims=True)
    den = p1 + p2
    w1 = g_w * p1 / den
    w2 = g_w * p2 / den
    e1 = g_idx * EXPERTS_PER_GROUP + i1
    e2 = g_idx * EXPERTS_PER_GROUP + i2

    rowe = lax.broadcasted_iota(I32, (N_EXPERTS, tm), 0)
    oh1 = jnp.where(rowe == e1, 1.0, 0.0)
    oh2 = jnp.where(rowe == e2, 1.0, 0.0)
    oh = (oh1 + oh2).astype(BF16)
    ri = lax.broadcasted_iota(I32, (tm, tm), 0)
    ci = lax.broadcasted_iota(I32, (tm, tm), 1)
    before = jnp.where(ri < ci, 1.0, 0.0).astype(BF16)
    ones = jnp.ones((tm, tm), BF16)
    tot = jnp.dot(oh, before, preferred_element_type=F32) + base_ref[...]
    r1 = jnp.sum(oh1 * tot, axis=0, keepdims=True)
    r2 = jnp.sum(oh2 * tot, axis=0, keepdims=True)
    base_ref[...] = base_ref[...] + jnp.dot(oh, ones, preferred_element_type=F32)
    cnt_ref[...] = base_ref[...]

    meta_ref[...] = jnp.concatenate(
        [e1, e2, r1.astype(I32), r2.astype(I32), jnp.zeros((4, tm), I32)], axis=0)
    wrow = lax.broadcasted_iota(I32, (LANES, tm), 0)
    wc_ref[...] = jnp.where(wrow == 0, w1, jnp.where(wrow == 1, w2, 0.0)).T


def _router(x, norm_w, wr, br):
    t, d = x.shape
    tm = ROUTE_TM
    return pl.pallas_call(
        _router_kernel,
        out_shape=(jax.ShapeDtypeStruct((t, PK, LANES), U32),
                   jax.ShapeDtypeStruct((8, t), I32),
                   jax.ShapeDtypeStruct((t, LANES), F32),
                   jax.ShapeDtypeStruct((N_EXPERTS, tm), F32)),
        grid=(t // tm,),
        in_specs=[pl.BlockSpec((tm, d), lambda i: (i, 0)),
                  pl.BlockSpec((1, d), lambda i: (0, 0)),
                  pl.BlockSpec((d, LANES), lambda i: (0, 0)),
                  pl.BlockSpec((1, LANES), lambda i: (0, 0))],
        out_specs=(pl.BlockSpec((tm, PK, LANES), lambda i: (i, 0, 0)),
                   pl.BlockSpec((8, tm), lambda i: (0, i)),
                   pl.BlockSpec((tm, LANES), lambda i: (i, 0)),
                   pl.BlockSpec((N_EXPERTS, tm), lambda i: (0, 0))),
        scratch_shapes=[pltpu.VMEM((N_EXPERTS, tm), F32)],
        compiler_params=_cp(("arbitrary",)),
        name="router",
    )(x, norm_w.reshape(1, d), wr, br)


SCAT_TM = 1024


def _scatter_kernel(zs_ref, nu_ref, s1_ref, s2_ref, hp_ref, zero_ref, xs_ref, sem):
    i = pl.program_id(0)
    tm = s1_ref.shape[0]
    n_all = xs_ref.shape[0] // MOE_BLK

    @pl.when(i == 0)
    def _():
        def zfill(e, c):
            pltpu.make_async_copy(zero_ref, xs_ref.at[pl.ds(zs_ref[e], MOE_BLK)], sem).start()
            return c

        def zwait(e, c):
            pltpu.make_async_copy(zero_ref, xs_ref.at[pl.ds(0, MOE_BLK)], sem).wait()
            return c

        def tfill(b, c):
            pltpu.make_async_copy(zero_ref, xs_ref.at[pl.ds(b * MOE_BLK, MOE_BLK)], sem).start()
            return c

        lax.fori_loop(0, N_EXPERTS, zfill, 0)
        lax.fori_loop(0, N_EXPERTS, zwait, 0)
        lax.fori_loop(nu_ref[0], n_all, tfill, 0)
        lax.fori_loop(nu_ref[0], n_all, zwait, 0)

    def issue(t, c):
        src = hp_ref.at[i * tm + t]
        pltpu.make_async_copy(src, xs_ref.at[s1_ref[t]], sem).start()
        pltpu.make_async_copy(src, xs_ref.at[s2_ref[t]], sem).start()
        return c

    def drain(t, c):
        pltpu.make_async_copy(hp_ref.at[0], xs_ref.at[0], sem).wait()
        pltpu.make_async_copy(hp_ref.at[0], xs_ref.at[0], sem).wait()
        return c

    lax.fori_loop(0, tm, issue, 0)
    lax.fori_loop(0, tm, drain, 0)


def _scatter_rows(zero_start, n_used, slot1, slot2, hp, n_rows):
    t = hp.shape[0]
    tm = SCAT_TM
    zero = jnp.zeros((MOE_BLK, PK, LANES), U32)
    grid_spec = pltpu.PrefetchScalarGridSpec(
        num_scalar_prefetch=2,
        grid=(t // tm,),
        in_specs=[pl.BlockSpec((tm,), lambda i, zs, nu: (i,), memory_space=pltpu.SMEM),
                  pl.BlockSpec((tm,), lambda i, zs, nu: (i,), memory_space=pltpu.SMEM),
                  pl.BlockSpec(memory_space=pl.ANY),
                  pl.BlockSpec(memory_space=pl.ANY)],
        out_specs=pl.BlockSpec(memory_space=pl.ANY),
        scratch_shapes=[pltpu.SemaphoreType.DMA(())],
    )
    return pl.pallas_call(
        _scatter_kernel,
        out_shape=jax.ShapeDtypeStruct((n_rows, PK, LANES), U32),
        grid_spec=grid_spec,
        compiler_params=_cp(("arbitrary",)),
        name="scatter_rows",
    )(zero_start, n_used, slot1, slot2, hp, zero)


FF_TN = 256


def _unpack_rows(x_ref, xs_ref):
    for s in range(PK):
        u = x_ref[:, s, :]
        xs_ref[:, s * LANES:(s + 1) * LANES] = pltpu.bitcast(u << 16, F32).astype(BF16)
        xs_ref[:, HALF_D + s * LANES:HALF_D + (s + 1) * LANES] = pltpu.bitcast(
            u & jnp.uint32(0xFFFF0000), F32).astype(BF16)


def _ffn_up_kernel(be_ref, nu_ref, x_ref, wg_ref, wu_ref, o_ref, wgb_ref, wub_ref, xs_ref):
    i = pl.program_id(1)

    @pl.when(i < nu_ref[0])
    def _():
        @pl.when((i == 0) | (be_ref[i] != be_ref[jnp.maximum(i - 1, 0)]))
        def _():
            wgb_ref[...] = wg_ref[0, 0].astype(BF16)
            wub_ref[...] = wu_ref[0, 0].astype(BF16)

        _unpack_rows(x_ref, xs_ref)
        x = xs_ref[...]
        g = jnp.dot(x, wgb_ref[...], preferred_element_type=F32)
        u = jnp.dot(x, wub_ref[...], preferred_element_type=F32)
        o_ref[...] = (g * _sigmoid(g) * u).astype(o_ref.dtype)

    @pl.when(i >= nu_ref[0])
    def _():
        o_ref[...] = jnp.zeros(o_ref.shape, o_ref.dtype)


def _ffn_up(block_exp, n_used, xs, w_gate, w_up, layer, n_blocks):
    d, ff = w_gate.shape[2], w_gate.shape[3]
    blk = lambda i, nu: jnp.minimum(i, nu[0] - 1)
    grid_spec = pltpu.PrefetchScalarGridSpec(
        num_scalar_prefetch=2,
        grid=(ff // FF_TN, n_blocks),
        in_specs=[pl.BlockSpec((MOE_BLK, PK, LANES), lambda j, i, be, nu: (blk(i, nu), 0, 0)),
                  pl.BlockSpec((1, 1, d, FF_TN), lambda j, i, be, nu: (layer, be[blk(i, nu)], 0, j)),
                  pl.BlockSpec((1, 1, d, FF_TN), lambda j, i, be, nu: (layer, be[blk(i, nu)], 0, j))],
        out_specs=pl.BlockSpec((MOE_BLK, FF_TN), lambda j, i, be, nu: (i, j)),
        scratch_shapes=[pltpu.VMEM((d, FF_TN), BF16), pltpu.VMEM((d, FF_TN), BF16),
                        pltpu.VMEM((MOE_BLK, d), BF16)],
    )
    return pl.pallas_call(
        _ffn_up_kernel,
        out_shape=jax.ShapeDtypeStruct((n_blocks * MOE_BLK, ff), BF16),
        grid_spec=grid_spec,
        compiler_params=_cp(("arbitrary", "arbitrary")),
        name="ffn_up",
    )(block_exp, n_used, xs, w_gate, w_up)


def _ffn_down_kernel(be_ref, nu_ref, h_ref, wd_ref, y_ref, wdb_ref):
    i = pl.program_id(0)

    @pl.when(i < nu_ref[0])
    def _():
        @pl.when((i == 0) | (be_ref[i] != be_ref[jnp.maximum(i - 1, 0)]))
        def _():
            wdb_ref[...] = wd_ref[0, 0].astype(BF16)

        y = jnp.dot(h_ref[...], wdb_ref[...], preferred_element_type=F32)
        for s in range(y.shape[1] // LANES):
            y_ref[:, s, :] = y[:, s * LANES:(s + 1) * LANES]

    @pl.when(i >= nu_ref[0])
    def _():
        y_ref[...] = jnp.zeros(y_ref.shape, y_ref.dtype)


def _ffn_down(block_exp, n_used, hmid, w_down, layer, n_blocks):
    ff, d = w_down.shape[2], w_down.shape[3]
    blk = lambda i, nu: jnp.minimum(i, nu[0] - 1)
    grid_spec = pltpu.PrefetchScalarGridSpec(
        num_scalar_prefetch=2,
        grid=(n_blocks,),
        in_specs=[pl.BlockSpec((MOE_BLK, ff), lambda i, be, nu: (blk(i, nu), 0)),
                  pl.BlockSpec((1, 1, ff, d), lambda i, be, nu: (layer, be[blk(i, nu)], 0, 0))],
        out_specs=pl.BlockSpec((MOE_BLK, d // LANES, LANES), lambda i, be, nu: (i, 0, 0)),
        scratch_shapes=[pltpu.VMEM((ff, d), BF16)],
    )
    return pl.pallas_call(
        _ffn_down_kernel,
        out_shape=jax.ShapeDtypeStruct((n_blocks * MOE_BLK, d // LANES, LANES), F32),
        grid_spec=grid_spec,
        compiler_params=_cp(("arbitrary",)),
        name="ffn_down",
    )(block_exp, n_used, hmid, w_down)


COMB_TM = 256


def _combine_kernel(s1_ref, s2_ref, x_ref, wc_ref, fw_ref, y_ref, o_ref, ya_ref, yb_ref, sem, *, final):
    tm = x_ref.shape[0]

    def issue(t, c):
        pltpu.make_async_copy(y_ref.at[s1_ref[0, 0, t]], ya_ref.at[t], sem).start()
        pltpu.make_async_copy(y_ref.at[s2_ref[0, 0, t]], yb_ref.at[t], sem).start()
        return c

    def drain(t, c):
        pltpu.make_async_copy(y_ref.at[0], ya_ref.at[0], sem).wait()
        pltpu.make_async_copy(y_ref.at[0], yb_ref.at[0], sem).wait()
        return c

    lax.fori_loop(0, tm, issue, 0)
    lax.fori_loop(0, tm, drain, 0)
    w1 = wc_ref[:, 0:1]
    w2 = wc_ref[:, 1:2]
    for s in range(x_ref.shape[1] // LANES):
        cols = slice(s * LANES, (s + 1) * LANES)
        o_ref[:, cols] = x_ref[:, cols] + w1 * ya_ref[:, s, :] + w2 * yb_ref[:, s, :]
    if final:
        o = o_ref[...]
        o_ref[...] = o * lax.rsqrt(jnp.mean(o * o, axis=-1, keepdims=True) + EPS) * fw_ref[...]


def _combine(slot1, slot2, x, wcol, y, final_w, final):
    t, d = x.shape
    tm = COMB_TM
    s3 = lambda a: a.reshape(t // tm, 1, tm)
    return pl.pallas_call(
        functools.partial(_combine_kernel, final=final),
        out_shape=jax.ShapeDtypeStruct((t, d), F32),
        grid=(t // tm,),
        in_specs=[pl.BlockSpec((1, 1, tm), lambda i: (i, 0, 0), memory_space=pltpu.SMEM),
                  pl.BlockSpec((1, 1, tm), lambda i: (i, 0, 0), memory_space=pltpu.SMEM),
                  pl.BlockSpec((tm, d), lambda i: (i, 0)),
                  pl.BlockSpec((tm, LANES), lambda i: (i, 0)),
                  pl.BlockSpec((1, d), lambda i: (0, 0)),
                  pl.BlockSpec(memory_space=pl.ANY)],
        out_specs=pl.BlockSpec((tm, d), lambda i: (i, 0)),
        scratch_shapes=[pltpu.VMEM((tm, d // LANES, LANES), F32),
                        pltpu.VMEM((tm, d // LANES, LANES), F32),
                        pltpu.SemaphoreType.DMA(())],
        compiler_params=_cp(("arbitrary",)),
        name="combine",
    )(s3(slot1), s3(slot2), x, wcol, final_w.reshape(1, d), y)


def _mixer_layer(xf, batch, seq, norm_w, w_in, dn_conv_w, dn_a_log, dn_dt_bias, dn_norm_w, ml_igate_b,
                 ml_fgate_b, ml_norm_w, sc_conv_w, w_out):
    c0 = 3 * DN_WIDTH + DN_WIDTH
    c1 = c0 + 4 * DN_HEADS
    c2 = c1 + 2 * ML_HEADS * ML_QK + 2 * ML_WIDTH
    c3 = c2 + 4 * ML_HEADS
    w_main = jnp.concatenate([w_in[:, :c0], w_in[:, c1:c2], w_in[:, c3:]], axis=1).astype(BF16)
    w_gate = _pack_gate_cols(w_in[:, c0:c0 + 2 * DN_HEADS], w_in[:, c0 + 2 * DN_HEADS:c1],
                             w_in[:, c2:c2 + 2 * ML_HEADS], w_in[:, c2 + 2 * ML_HEADS:c3]).astype(BF16)
    h = _rmsnorm(xf, norm_w, BF16)
    proj, graw = _inproj(h, w_main, w_gate)
    gc, gr = _gate_prep(graw, _gate_params(dn_a_log, dn_dt_bias, ml_igate_b, ml_fgate_b))
    dn = _deltanet(proj, dn_conv_w, gc, gr, dn_norm_w, batch, seq)
    ml = _mlstm(proj, gc, gr, ml_norm_w, batch, seq)
    sc = _shortconv(proj, sc_conv_w, batch, seq)
    return _outproj(dn, ml, sc, w_out.astype(BF16), xf)


def _moe_layer(xf, norm_w, rg_w, rg_b, re_w, re_b, w_gate, w_up, w_down, layer, final_w, final):
    t, d = xf.shape
    wr = jnp.zeros((d, LANES), F32).at[:, :N_GROUPS].set(rg_w).at[:, EXP_ROW0:EXP_ROW0 + N_EXPERTS].set(re_w)
    br = jnp.zeros((1, LANES), F32).at[0, :N_GROUPS].set(rg_b).at[0, EXP_ROW0:EXP_ROW0 + N_EXPERTS].set(re_b)
    hp, meta, wcol, cnt = _router(xf, norm_w, wr, br)
    counts = cnt[:, 0].astype(I32)
    padded = (counts + MOE_BLK - 1) // MOE_BLK * MOE_BLK
    pad_end = jnp.cumsum(padded)
    pad_start = pad_end - padded
    n_blocks = -(-(2 * t + N_EXPERTS * (MOE_BLK - 1)) // MOE_BLK)
    n_used = (pad_end[-1:] // MOE_BLK).astype(I32)
    block_exp = jnp.minimum(jnp.searchsorted(pad_end, jnp.arange(n_blocks, dtype=I32) * MOE_BLK, side="right"),
                            N_EXPERTS - 1).astype(I32)
    slot1 = pad_start[meta[0]] + meta[2]
    slot2 = pad_start[meta[1]] + meta[3]
    xs = _scatter_rows(pad_start + counts, n_used, slot1, slot2, hp, (n_blocks + 1) * MOE_BLK)
    hmid = _ffn_up(block_exp, n_used, xs, w_gate, w_up, layer, n_blocks)
    y = _ffn_down(block_exp, n_used, hmid, w_down, layer, n_blocks)
    return _combine(slot1, slot2, xf, wcol, y, final_w, final)


def kernel(x, norm_mix_w, w_in, dn_conv_w, dn_a_log, dn_dt_bias, dn_norm_w, ml_igate_b, ml_fgate_b, ml_norm_w,
           sc_conv_w, w_out, norm_ffn_w, router_group_w, router_group_b, router_expert_w, router_expert_b,
           expert_w_gate, expert_w_up, expert_w_down, final_norm_w):
    batch, seq, d = x.shape
    depth = w_in.shape[0]
    xf = x.reshape(batch * seq, d)
    for l in range(depth):
        xf = _mixer_layer(xf, batch, seq, norm_mix_w[l], w_in[l], dn_conv_w[l], dn_a_log[l], dn_dt_bias[l],
                          dn_norm_w[l], ml_igate_b[l], ml_fgate_b[l], ml_norm_w[l], sc_conv_w[l], w_out[l])
        xf = _moe_layer(xf, norm_ffn_w[l], router_group_w[l], router_group_b[l], router_expert_w[l],
                        router_expert_b[l], expert_w_gate, expert_w_up, expert_w_down, l,
                        final_norm_w, l == depth - 1)
    return xf.reshape(batch, seq, d)
```

```python
import functools

import numpy as np
import jax
import jax.numpy as jnp
from jax import lax
from jax.experimental import pallas as pl
from jax.experimental.pallas import tpu as pltpu

F32 = jnp.float32
BF16 = jnp.bfloat16
I32 = jnp.int32
U32 = jnp.uint32

D_MODEL = 4096
DN_HEADS = 12
DN_DIM = 128
DN_WIDTH = DN_HEADS * DN_DIM
DN_CONV = 5
ML_HEADS = 6
ML_QK = 128
ML_V = 256
ML_WIDTH = ML_HEADS * ML_V
SC_WIDTH = 1024
SC_CONV = 3
N_GROUPS = 4
EXPERTS_PER_GROUP = 8
N_EXPERTS = N_GROUPS * EXPERTS_PER_GROUP
D_FF = 768
EPS = 1e-6

LANES = 128
SUBLANES = 8
VMEM_LIMIT = 52 * 1024 * 1024

CH = 64
TL = 256
CPT = TL // CH
N_HEADS_ALL = DN_HEADS + ML_HEADS
MOE_BLK = 256
NEG = -1e30

OFF_DN_Q, OFF_DN_K, OFF_DN_V, OFF_DN_Z = 0, 1536, 3072, 4608
OFF_ML_Q, OFF_ML_K, OFF_ML_V, OFF_ML_O = 6144, 6912, 7680, 9216
OFF_SC_B, OFF_SC_C, OFF_SC_X = 10752, 11776, 12800
N_MAIN = 13824


def _cp(sem, vmem=VMEM_LIMIT):
    return pltpu.CompilerParams(dimension_semantics=sem, vmem_limit_bytes=vmem)


def _mm(a, b):
    return jnp.dot(a.astype(BF16), b.astype(BF16), preferred_element_type=F32)


def _mm_nt(a, b):
    return lax.dot_general(a.astype(BF16), b.astype(BF16), (((1,), (1,)), ((), ())),
                           preferred_element_type=F32)


def _mm_f32(a, b):
    return jnp.dot(a, b, preferred_element_type=F32, precision=lax.Precision.HIGHEST)


def _sigmoid(x):
    return 1.0 / (1.0 + jnp.exp(-x))


def _softplus(x):
    return jnp.maximum(x, 0.0) + jnp.log(1.0 + jnp.exp(-jnp.abs(x)))


def _rms_kernel(x_ref, w_ref, o_ref):
    x = x_ref[...]
    ms = jnp.mean(x * x, axis=-1, keepdims=True)
    o_ref[...] = (x * lax.rsqrt(ms + EPS) * w_ref[...]).astype(o_ref.dtype)


def _rmsnorm(x2d, w, out_dtype, tm=256):
    t, d = x2d.shape
    return pl.pallas_call(
        _rms_kernel,
        out_shape=jax.ShapeDtypeStruct((t, d), out_dtype),
        grid=(t // tm,),
        in_specs=[pl.BlockSpec((tm, d), lambda i: (i, 0)), pl.BlockSpec((1, d), lambda i: (0, 0))],
        out_specs=pl.BlockSpec((tm, d), lambda i: (i, 0)),
        compiler_params=_cp(("arbitrary",)),
        name="rmsnorm",
    )(x2d, w.reshape(1, d))


def _inproj_kernel(a_ref, w_ref, wg_ref, o_ref, g_ref):
    a = a_ref[...]
    o_ref[...] = jnp.dot(a, w_ref[...], preferred_element_type=F32).astype(o_ref.dtype)

    @pl.when(pl.program_id(1) == 0)
    def _():
        g_ref[...] = jnp.dot(a, wg_ref[...], preferred_element_type=F32)


def _inproj(h, w_main, w_gate, tm=1024, tn=512):
    t, d = h.shape
    n = w_main.shape[1]
    ng = w_gate.shape[1]
    return pl.pallas_call(
        _inproj_kernel,
        out_shape=(jax.ShapeDtypeStruct((t, n), BF16), jax.ShapeDtypeStruct((t, ng), F32)),
        grid=(t // tm, n // tn),
        in_specs=[pl.BlockSpec((tm, d), lambda i, j: (i, 0)),
                  pl.BlockSpec((d, tn), lambda i, j: (0, j)),
                  pl.BlockSpec((d, ng), lambda i, j: (0, 0))],
        out_specs=(pl.BlockSpec((tm, tn), lambda i, j: (i, j)),
                   pl.BlockSpec((tm, ng), lambda i, j: (i, 0))),
        compiler_params=_cp(("arbitrary", "arbitrary")),
        name="inproj",
    )(h, w_main, w_gate)


def _gate_kernel(raw_ref, p_ref, gc_ref, gr_ref):
    x = raw_ref[...]
    lane = lax.broadcasted_iota(I32, x.shape, 1)
    chn = lane & 7
    is_ml = lane >= LANES
    a_log, dt_b, ig_b, fg_b = p_ref[0:1, :], p_ref[1:2, :], p_ref[2:3, :], p_ref[3:4, :]
    beta = _sigmoid(x)
    g = -jnp.exp(a_log) * _softplus(x + dt_b)
    ig = x + ig_b
    lf = -_softplus(-(x + fg_b))
    lo = chn < 2
    val = jnp.where(is_ml, jnp.where(lo, ig, lf), jnp.where(lo, beta, g))
    val = jnp.where(chn < 4, val, 0.0)
    ri = lax.broadcasted_iota(I32, (TL, TL), 0)
    ci = lax.broadcasted_iota(I32, (TL, TL), 1)
    same = (ri >> 6) == (ci >> 6)
    m_f = jnp.where(same & (ri >= ci), 1.0, 0.0)
    m_b = jnp.where(same & (ri <= ci), 1.0, 0.0)
    m_t = jnp.where(same, 1.0, 0.0)
    cum_f = _mm_f32(m_f, val)
    cum_b = _mm_f32(m_b, val)
    tot = pltpu.roll(_mm_f32(m_t, val), 2, axis=1)
    out = jnp.where(chn == 2, cum_f, jnp.where(chn == 3, cum_b, jnp.where((chn == 4) | (chn == 5), tot, val)))
    gr_ref[0] = out.T
    for hh in range(N_HEADS_ALL):
        grp, h = (0, hh) if hh < DN_HEADS else (1, hh - DN_HEADS)
        v = out[:, grp * LANES:(grp + 1) * LANES]
        sh = (LANES - 8 * h) % LANES
        gc_ref[hh] = pltpu.roll(v, sh, axis=1) if sh else v


def _gate_prep(raw, params):
    t = raw.shape[0]
    nt = t // TL
    return pl.pallas_call(
        _gate_kernel,
        out_shape=(jax.ShapeDtypeStruct((N_HEADS_ALL, t, LANES), F32),
                   jax.ShapeDtypeStruct((nt, 2 * LANES, TL), F32)),
        grid=(nt,),
        in_specs=[pl.BlockSpec((TL, 2 * LANES), lambda i: (i, 0)),
                  pl.BlockSpec((8, 2 * LANES), lambda i: (0, 0))],
        out_specs=(pl.BlockSpec((N_HEADS_ALL, TL, LANES), lambda i: (0, i, 0)),
                   pl.BlockSpec((1, 2 * LANES, TL), lambda i: (i, 0, 0))),
        compiler_params=_cp(("arbitrary",)),
        name="gate_prep",
    )(raw, params)


def _conv_centred(x_ref, w_ref, pad_ref, width):
    s = x_ref.shape[0]
    half = width // 2
    pad_ref[0:8, :] = jnp.zeros((8, pad_ref.shape[1]), F32)
    pad_ref[s + 8:s + 16, :] = jnp.zeros((8, pad_ref.shape[1]), F32)
    pad_ref[8:s + 8, :] = x_ref[...].astype(F32)
    acc = None
    for i in range(width):
        term = w_ref[i:i + 1, :] * pad_ref[8 - half + i:8 - half + i + s, :]
        acc = term if acc is None else acc + term
    return acc


def _tile_masks():
    ri = lax.broadcasted_iota(I32, (TL, TL), 0)
    ci = lax.broadcasted_iota(I32, (TL, TL), 1)
    return ri, ci


def _unit_tri_inverse(a, same16, same32, eye):
    a16 = jnp.where(same16, a, 0.0)
    b1 = _mm(a16, a16)
    b2 = _mm(b1, b1)
    b3 = _mm(b2, b2)
    p = eye - a16
    p = p + _mm(p, b1)
    p = p + _mm(p, b2)
    p = p + _mm(p, b3)
    a32 = jnp.where(same32 & jnp.logical_not(same16), a, 0.0)
    p = p - _mm(_mm(p, a32), p)
    a64 = jnp.where(same32, 0.0, a)
    p = p - _mm(_mm(p, a64), p)
    return p


def _dn_kernel(q_ref, k_ref, v_ref, z_ref, cq_ref, ck_ref, cv_ref, gc_ref, gr_ref, nw_ref, o_ref,
               pad_ref, qn_ref, kn_ref, vv_ref, u_ref, wq_ref, ak_ref, eg_ref, oo_ref):
    s = q_ref.shape[0]
    nt = s // TL
    nc = s // CH
    scale = DN_DIM ** -0.5

    q = _conv_centred(q_ref, cq_ref, pad_ref, DN_CONV)
    q = q * _sigmoid(q)
    qn_ref[...] = q * (lax.rsqrt(jnp.sum(q * q, axis=-1, keepdims=True) + EPS) * scale)
    k = _conv_centred(k_ref, ck_ref, pad_ref, DN_CONV)
    k = k * _sigmoid(k)
    kn_ref[...] = k * lax.rsqrt(jnp.sum(k * k, axis=-1, keepdims=True) + EPS)
    v = _conv_centred(v_ref, cv_ref, pad_ref, DN_CONV)
    vv_ref[...] = v * _sigmoid(v)

    def tile_body(i, carry):
        r0 = pl.multiple_of(i * TL, TL)
        ri, ci = _tile_masks()
        same64 = (ri >> 6) == (ci >> 6)
        same32 = (ri >> 5) == (ci >> 5)
        same16 = (ri >> 4) == (ci >> 4)
        eye = jnp.where(ri == ci, 1.0, 0.0)
        qn = qn_ref[pl.ds(r0, TL), :]
        kn = kn_ref[pl.ds(r0, TL), :]
        vv = vv_ref[pl.ds(r0, TL), :]
        kk = _mm_nt(kn, kn)
        qk = _mm_nt(qn, kn)
        gc = gc_ref[0, pl.ds(r0, TL), :]
        gr = gr_ref[i]
        for d in range(2):
            beta_c = gc[:, d:d + 1]
            g_c = gc[:, 2 + d:3 + d]
            gt_c = gc[:, 4 + d:5 + d]
            g_r = gr[2 + d:3 + d, :]
            incl = same64 & ((ri >= ci) if d == 0 else (ri <= ci))
            strict = same64 & ((ri > ci) if d == 0 else (ri < ci))
            dec = jnp.exp(jnp.where(incl, g_c - g_r, NEG))
            a = jnp.where(strict, beta_c * kk * dec, 0.0)
            tinv = _unit_tri_inverse(a, same16, same32, eye)
            eg_c = jnp.exp(g_c)
            rhs = jnp.concatenate([beta_c * vv, (beta_c * eg_c) * kn], axis=1)
            uw = _mm(tinv, rhs)
            u_ref[d, pl.ds(r0, TL), :] = uw[:, :DN_DIM]
            w = uw[:, DN_DIM:]
            qe = qn * eg_c
            attn = qk * dec
            kdt = (kn * jnp.exp(gt_c - g_c)).T
            egt = jnp.exp(gt_c)
            for c in range(CPT):
                rows = slice(c * CH, (c + 1) * CH)
                wq_ref[d, i * CPT + c] = jnp.concatenate([w[rows], qe[rows]], axis=0).astype(BF16)
                ak_ref[d, i * CPT + c] = jnp.concatenate([attn[rows, rows], kdt[:, rows]], axis=0).astype(BF16)
                eg_ref[d, i * CPT + c] = jnp.broadcast_to(egt[c * CH:c * CH + 8, :], (8, LANES))
        return carry

    lax.fori_loop(0, nt, tile_body, 0)

    def chunk_body(c, carry):
        states = list(carry)
        for d in range(2):
            cc = c if d == 0 else nc - 1 - c
            r = pl.multiple_of(cc * CH, CH)
            st = states[d]
            ws = jnp.dot(wq_ref[d, cc], st.astype(BF16), preferred_element_type=F32)
            vnew = u_ref[d, pl.ds(r, CH), :] - ws[:CH]
            m2 = jnp.dot(ak_ref[d, cc], vnew.astype(BF16), preferred_element_type=F32)
            oo_ref[d, pl.ds(r, CH), :] = ws[CH:] + m2[:CH]
            states[d] = eg_ref[d, cc][0:1, :] * st + m2[CH:]
        return tuple(states)

    z0 = jnp.zeros((DN_DIM, DN_DIM), F32)
    lax.fori_loop(0, nc, chunk_body, (z0, z0))

    o = oo_ref[0] + oo_ref[1]
    o = o * lax.rsqrt(jnp.mean(o * o, axis=-1, keepdims=True) + EPS) * nw_ref[...]
    z = z_ref[...].astype(F32)
    o_ref[...] = (o * (z * _sigmoid(z))).astype(o_ref.dtype)


def _deltanet(proj, conv_w, gc, gr, norm_w, batch, seq):
    t = proj.shape[0]
    nc = seq // CH
    nt = seq // TL
    qb, kb, vb, zb = (OFF_DN_Q // DN_DIM, OFF_DN_K // DN_DIM, OFF_DN_V // DN_DIM, OFF_DN_Z // DN_DIM)
    col = lambda off: pl.BlockSpec((seq, DN_DIM), lambda b, h, off=off: (b, off + h))
    cw = lambda off: pl.BlockSpec((DN_CONV, DN_DIM), lambda b, h, off=off: (0, off + h))
    return pl.pallas_call(
        _dn_kernel,
        out_shape=jax.ShapeDtypeStruct((t, DN_WIDTH), BF16),
        grid=(batch, DN_HEADS),
        in_specs=[col(qb), col(kb), col(vb), col(zb), cw(0), cw(DN_HEADS), cw(2 * DN_HEADS),
                  pl.BlockSpec((1, seq, LANES), lambda b, h: (h, b, 0)),
                  pl.BlockSpec((nt, 8, TL), lambda b, h: (b, h, 0)),
                  pl.BlockSpec((1, DN_DIM), lambda b, h: (0, 0))],
        out_specs=pl.BlockSpec((seq, DN_DIM), lambda b, h: (b, h)),
        scratch_shapes=[pltpu.VMEM((seq + 16, DN_DIM), F32),
                        pltpu.VMEM((seq, DN_DIM), F32), pltpu.VMEM((seq, DN_DIM), F32),
                        pltpu.VMEM((seq, DN_DIM), F32),
                        pltpu.VMEM((2, seq, DN_DIM), F32),
                        pltpu.VMEM((2, nc, 2 * CH, DN_DIM), BF16),
                        pltpu.VMEM((2, nc, CH + DN_DIM, CH), BF16),
                        pltpu.VMEM((2, nc, 8, LANES), F32),
                        pltpu.VMEM((2, seq, DN_DIM), F32)],
        compiler_params=_cp(("arbitrary", "arbitrary")),
        name="deltanet",
    )(proj, proj, proj, proj, conv_w, conv_w, conv_w, gc, gr, norm_w.reshape(1, DN_DIM))


def _ml_kernel(q_ref, k_ref, v_ref, og_ref, gc_ref, gr_ref, nw_ref, o_ref,
               ak_ref, rs_ref, cs_ref, dn_ref, oo_ref):
    s = q_ref.shape[0]
    nt = s // TL
    nc = s // CH
    scale = ML_QK ** -0.5

    def tile_body(it, carry):
        m_carry = list(carry)
        ri, ci = _tile_masks()
        same64 = (ri >> 6) == (ci >> 6)
        rowc = lax.broadcasted_iota(I32, (TL, 1), 0) >> 6
        for d in range(2):
            i = it if d == 0 else nt - 1 - it
            r0 = pl.multiple_of(i * TL, TL)
            qs = q_ref[pl.ds(r0, TL), :].astype(F32) * scale
            kf = k_ref[pl.ds(r0, TL), :].astype(F32)
            qk = _mm_nt(qs, kf)
            gc = gc_ref[0, pl.ds(r0, TL), :]
            gr = gr_ref[i]
            ig_c = gc[:, d:d + 1]
            b_c = gc[:, 2 + d:3 + d]
            bt_c = gc[:, 4 + d:5 + d]
            ig_r = gr[d:d + 1, :]
            b_r = gr[2 + d:3 + d, :]
            bt_r = gr[4 + d:5 + d, :]
            incl = same64 & ((ri >= ci) if d == 0 else (ri <= ci))
            dmat = jnp.where(incl, b_c - b_r + ig_r, NEG)
            m_intra = jnp.max(dmat, axis=1, keepdims=True)
            d_end_c = bt_c - b_c + ig_c
            d_end_r = bt_r - b_r + ig_r
            dmax_c = jnp.max(jnp.where(same64, d_end_r, NEG), axis=1, keepdims=True)
            m_in_c = jnp.zeros((TL, 1), F32)
            m_out_c = jnp.zeros((TL, 1), F32)
            m = m_carry[d]
            for cc in (range(CPT) if d == 0 else range(CPT - 1, -1, -1)):
                btot = bt_c[cc * CH:cc * CH + 1, :]
                dmx = dmax_c[cc * CH:cc * CH + 1, :]
                m_new = jnp.maximum(btot + m, dmx)
                m_in_c = jnp.where(rowc == cc, m, m_in_c)
                m_out_c = jnp.where(rowc == cc, m_new, m_out_c)
                m = m_new
            m_carry[d] = m
            m_inter = b_c + m_in_c
            m_pos = jnp.maximum(m_intra, m_inter)
            scores = qk * jnp.exp(dmat - m_pos)
            inter_scale = jnp.exp(m_inter - m_pos)
            rsum = jnp.sum(scores, axis=1, keepdims=True)
            einv = jnp.exp(-m_pos)
            cscale = jnp.exp(bt_c + m_in_c - m_out_c)
            wk = kf * jnp.exp(d_end_c - m_out_c)
            wkt = wk.T
            lane = lax.broadcasted_iota(I32, (TL, LANES), 1)
            rs_ref[d, pl.ds(r0, TL), :] = jnp.where(lane == 0, inter_scale,
                                                   jnp.where(lane == 1, rsum, einv))
            for c in range(CPT):
                rows = slice(c * CH, (c + 1) * CH)
                ak_ref[d, i * CPT + c] = jnp.concatenate([scores[rows, rows], wkt[:, rows]], axis=0).astype(BF16)
                cs_ref[d, i * CPT + c] = jnp.broadcast_to(cscale[c * CH:c * CH + 8, :], (8, LANES))
                dn_ref[d, i * CPT + c] = jnp.broadcast_to(jnp.sum(wk[rows], axis=0, keepdims=True), (8, ML_QK))
        return tuple(m_carry)

    zero11 = jnp.zeros((1, 1), F32)
    lax.fori_loop(0, nt, tile_body, (zero11, zero11))

    def chunk_body(c, carry):
        cst = [carry[0], carry[1]]
        nst = [carry[2], carry[3]]
        for d in range(2):
            cc = c if d == 0 else nc - 1 - c
            r = pl.multiple_of(cc * CH, CH)
            qb = q_ref[pl.ds(r, CH), :]
            inter = jnp.dot(qb, cst[d].astype(BF16), preferred_element_type=F32) * scale
            m2 = jnp.dot(ak_ref[d, cc], v_ref[pl.ds(r, CH), :], preferred_element_type=F32)
            qn = jnp.sum(qb.astype(F32) * nst[d], axis=1, keepdims=True) * scale
            rs = rs_ref[d, pl.ds(r, CH), :]
            isc = rs[:, 0:1]
            num = m2[:CH] + isc * inter
            den = rs[:, 1:2] + isc * qn
            oo_ref[d, pl.ds(r, CH), :] = num / jnp.maximum(jnp.abs(den), rs[:, 2:3])
            csc = cs_ref[d, cc][0:1, 0:1]
            cst[d] = csc * cst[d] + m2[CH:]
            nst[d] = csc * nst[d] + dn_ref[d, cc][0:1, :]
        return (cst[0], cst[1], nst[0], nst[1])

    c0 = jnp.zeros((ML_QK, ML_V), F32)
    n0 = jnp.zeros((1, ML_QK), F32)
    lax.fori_loop(0, nc, chunk_body, (c0, c0, n0, n0))

    o = oo_ref[0] + oo_ref[1]
    o = o * lax.rsqrt(jnp.mean(o * o, axis=-1, keepdims=True) + EPS) * nw_ref[...]
    o_ref[...] = (o * _sigmoid(og_ref[...].astype(F32))).astype(o_ref.dtype)


def _mlstm(proj, gc, gr, norm_w, batch, seq):
    t = proj.shape[0]
    nc = seq // CH
    nt = seq // TL
    qb, kb, vb, ob = OFF_ML_Q // ML_QK, OFF_ML_K // ML_QK, OFF_ML_V // ML_V, OFF_ML_O // ML_V
    return pl.pallas_call(
        _ml_kernel,
        out_shape=jax.ShapeDtypeStruct((t, ML_WIDTH), BF16),
        grid=(batch, ML_HEADS),
        in_specs=[pl.BlockSpec((seq, ML_QK), lambda b, h: (b, qb + h)),
                  pl.BlockSpec((seq, ML_QK), lambda b, h: (b, kb + h)),
                  pl.BlockSpec((seq, ML_V), lambda b, h: (b, vb + h)),
                  pl.BlockSpec((seq, ML_V), lambda b, h: (b, ob + h)),
                  pl.BlockSpec((1, seq, LANES), lambda b, h: (DN_HEADS + h, b, 0)),
                  pl.BlockSpec((nt, 8, TL), lambda b, h: (b, LANES // 8 + h, 0)),
                  pl.BlockSpec((1, ML_V), lambda b, h: (0, h))],
        out_specs=pl.BlockSpec((seq, ML_V), lambda b, h: (b, h)),
        scratch_shapes=[pltpu.VMEM((2, nc, CH + ML_QK, CH), BF16),
                        pltpu.VMEM((2, seq, LANES), F32),
                        pltpu.VMEM((2, nc, 8, LANES), F32),
                        pltpu.VMEM((2, nc, 8, ML_QK), F32),
                        pltpu.VMEM((2, seq, ML_V), F32)],
        compiler_params=_cp(("arbitrary", "arbitrary")),
        name="mlstm",
    )(proj, proj, proj, proj, gc, gr, norm_w.reshape(1, ML_WIDTH))


def _sc_kernel(b_ref, c_ref, x_ref, w_ref, o_ref, cx_ref, pad_ref):
    cx_ref[...] = c_ref[...].astype(F32) * x_ref[...].astype(F32)
    y = _conv_centred(cx_ref, w_ref, pad_ref, SC_CONV)
    o_ref[...] = (b_ref[...].astype(F32) * y).astype(o_ref.dtype)


def _shortconv(proj, conv_w, batch, seq, wc=256):
    t = proj.shape[0]
    bb, cb, xb = OFF_SC_B // wc, OFF_SC_C // wc, OFF_SC_X // wc
    col = lambda off: pl.BlockSpec((seq, wc), lambda b, j, off=off: (b, off + j))
    return pl.pallas_call(
        _sc_kernel,
        out_shape=jax.ShapeDtypeStruct((t, SC_WIDTH), BF16),
        grid=(batch, SC_WIDTH // wc),
        in_specs=[col(bb), col(cb), col(xb), pl.BlockSpec((SC_CONV, wc), lambda b, j: (0, j))],
        out_specs=pl.BlockSpec((seq, wc), lambda b, j: (b, j)),
        scratch_shapes=[pltpu.VMEM((seq, wc), F32), pltpu.VMEM((seq + 16, wc), F32)],
        compiler_params=_cp(("arbitrary", "arbitrary")),
        name="shortconv",
    )(proj, proj, proj, conv_w)


def _pack_gate_cols(dn_b, dn_a, ml_i, ml_f):
    r = dn_b.shape[0]

    def grp(lo, hi, nh):
        a = jnp.stack([lo[:, :nh], lo[:, nh:], hi[:, :nh], hi[:, nh:]], axis=-1)
        a = jnp.concatenate([a, jnp.zeros((r, nh, 4), a.dtype)], axis=-1).reshape(r, nh * 8)
        return jnp.concatenate([a, jnp.zeros((r, LANES - nh * 8), a.dtype)], axis=-1)

    return jnp.concatenate([grp(dn_b, dn_a, DN_HEADS), grp(ml_i, ml_f, ML_HEADS)], axis=-1)


def _gate_params(dn_a_log, dn_dt_bias, ml_igate_b, ml_fgate_b):
    z_dn = jnp.zeros((1, 2 * DN_HEADS), F32)
    z_ml = jnp.zeros((1, 2 * ML_HEADS), F32)
    flat = lambda a: a.astype(F32).reshape(1, -1)
    rows = [_pack_gate_cols(z_dn, flat(dn_a_log), z_ml, z_ml),
            _pack_gate_cols(z_dn, flat(dn_dt_bias), z_ml, z_ml),
            _pack_gate_cols(z_dn, z_dn, flat(ml_igate_b), z_ml),
            _pack_gate_cols(z_dn, z_dn, z_ml, flat(ml_fgate_b))]
    return jnp.concatenate(rows + [jnp.zeros((4, 2 * LANES), F32)], axis=0)


def _outproj_kernel(a1_ref, a2_ref, a3_ref, w1_ref, w2_ref, w3_ref, x_ref, o_ref):
    acc = jnp.dot(a1_ref[...], w1_ref[...], preferred_element_type=F32)
    acc += jnp.dot(a2_ref[...], w2_ref[...], preferred_element_type=F32)
    acc += jnp.dot(a3_ref[...], w3_ref[...], preferred_element_type=F32)
    o_ref[...] = x_ref[...] + acc


def _outproj(dn, ml, sc, w_out, x, tm=1024, tn=512):
    t, d = x.shape
    k1, k2, k3 = dn.shape[1], ml.shape[1], sc.shape[1]
    assert k1 == k2 and (k1 + k2) % k3 == 0
    return pl.pallas_call(
        _outproj_kernel,
        out_shape=jax.ShapeDtypeStruct((t, d), F32),
        grid=(t // tm, d // tn),
        in_specs=[pl.BlockSpec((tm, k1), lambda i, j: (i, 0)),
                  pl.BlockSpec((tm, k2), lambda i, j: (i, 0)),
                  pl.BlockSpec((tm, k3), lambda i, j: (i, 0)),
                  pl.BlockSpec((k1, tn), lambda i, j: (0, j)),
                  pl.BlockSpec((k2, tn), lambda i, j: (1, j)),
                  pl.BlockSpec((k3, tn), lambda i, j: ((k1 + k2) // k3, j)),
                  pl.BlockSpec((tm, tn), lambda i, j: (i, j))],
        out_specs=pl.BlockSpec((tm, tn), lambda i, j: (i, j)),
        compiler_params=_cp(("arbitrary", "arbitrary")),
        name="outproj",
    )(dn, ml, sc, w_out, w_out, w_out, x)


ROUTE_TM = 256
EXP_ROW0 = 32
HALF_D = D_MODEL // 2


def _router_kernel(x_ref, nw_ref, wr_ref, br_ref, hp_ref, meta_ref, wc_ref, cnt_ref, base_ref):
    tm = x_ref.shape[0]

    @pl.when(pl.program_id(0) == 0)
    def _():
        base_ref[...] = jnp.zeros(base_ref.shape, F32)

    x = x_ref[...]
    h = x * lax.rsqrt(jnp.mean(x * x, axis=-1, keepdims=True) + EPS) * nw_ref[...]
    bits = pltpu.bitcast(h.astype(BF16).astype(F32), U32)
    hp_ref[...] = (bits[:, :HALF_D] >> 16) | bits[:, HALF_D:]

    lt = (_mm_f32(h, wr_ref[...]) + br_ref[...]).T
    row8 = lax.broadcasted_iota(I32, (8, tm), 0)
    gl = jnp.where(row8 < N_GROUPS, lt[0:8, :], NEG)
    gex = jnp.exp(gl - jnp.max(gl, axis=0, keepdims=True))
    gp = gex / jnp.sum(gex, axis=0, keepdims=True)
    g_w = jnp.max(gp, axis=0, keepdims=True)
    g_idx = jnp.min(jnp.where(gp == g_w, row8, 8), axis=0, keepdims=True)
    el = lt[EXP_ROW0 + 24:EXP_ROW0 + 32, :]
    for g in (2, 1, 0):
        el = jnp.where(g_idx == g, lt[EXP_ROW0 + 8 * g:EXP_ROW0 + 8 * g + 8, :], el)
    ee = jnp.exp(el - jnp.max(el, axis=0, keepdims=True))
    p = ee / jnp.sum(ee, axis=0, keepdims=True)
    p1 = jnp.max(p, axis=0, keepdims=True)
    i1 = jnp.min(jnp.where(p == p1, row8, 8), axis=0, keepdims=True)
    pm = jnp.where(row8 == i1, -1.0, p)
    p2 = jnp.max(pm, axis=0, keepdims=True)
    i2 = jnp.min(jnp.where(pm == p2, row8, 8), axis=0, keepdims=True)
    den = p1 + p2
    w1 = g_w * p1 / den
    w2 = g_w * p2 / den
    e1 = g_idx * EXPERTS_PER_GROUP + i1
    e2 = g_idx * EXPERTS_PER_GROUP + i2

    rowe = lax.broadcasted_iota(I32, (N_EXPERTS, tm), 0)
    oh1 = jnp.where(rowe == e1, 1.0, 0.0)
    oh2 = jnp.where(rowe == e2, 1.0, 0.0)
    oh = (oh1 + oh2).astype(BF16)
    ri = lax.broadcasted_iota(I32, (tm, tm), 0)
    ci = lax.broadcasted_iota(I32, (tm, tm), 1)
    before = jnp.where(ri < ci, 1.0, 0.0).astype(BF16)
    ones = jnp.ones((tm, tm), BF16)
    tot = jnp.dot(oh, before, preferred_element_type=F32) + base_ref[...]
    r1 = jnp.sum(oh1 * tot, axis=0, keepdims=True)
    r2 = jnp.sum(oh2 * tot, axis=0, keepdims=True)
    base_ref[...] = base_ref[...] + jnp.dot(oh, ones, preferred_element_type=F32)
    cnt_ref[...] = base_ref[...]

    meta_ref[...] = jnp.concatenate(
        [e1, e2, r1.astype(I32), r2.astype(I32), jnp.zeros((4, tm), I32)], axis=0)
    wrow = lax.broadcasted_iota(I32, (LANES, tm), 0)
    wc_ref[...] = jnp.where(wrow == 0, w1, jnp.where(wrow == 1, w2, 0.0)).T


def _router(x, norm_w, wr, br):
    t, d = x.shape
    tm = ROUTE_TM
    return pl.pallas_call(
        _router_kernel,
        out_shape=(jax.ShapeDtypeStruct((t, HALF_D), U32),
                   jax.ShapeDtypeStruct((8, t), I32),
                   jax.ShapeDtypeStruct((t, LANES), F32),
                   jax.ShapeDtypeStruct((N_EXPERTS, tm), F32)),
        grid=(t // tm,),
        in_specs=[pl.BlockSpec((tm, d), lambda i: (i, 0)),
                  pl.BlockSpec((1, d), lambda i: (0, 0)),
                  pl.BlockSpec((d, LANES), lambda i: (0, 0)),
                  pl.BlockSpec((1, LANES), lambda i: (0, 0))],
        out_specs=(pl.BlockSpec((tm, HALF_D), lambda i: (i, 0)),
                   pl.BlockSpec((8, tm), lambda i: (0, i)),
                   pl.BlockSpec((tm, LANES), lambda i: (i, 0)),
                   pl.BlockSpec((N_EXPERTS, tm), lambda i: (0, 0))),
        scratch_shapes=[pltpu.VMEM((N_EXPERTS, tm), F32)],
        compiler_params=_cp(("arbitrary",)),
        name="router",
    )(x, norm_w.reshape(1, d), wr, br)


SCAT_TM = 256


def _scatter_kernel(zs_ref, ze_ref, nu_ref, s1_ref, s2_ref, hp_ref, xs_ref, zero_ref, sem):
    i = pl.program_id(0)
    tm = hp_ref.shape[0]
    n_all = xs_ref.shape[0] // MOE_BLK

    @pl.when(i == 0)
    def _():
        zero_ref[...] = jnp.zeros(zero_ref.shape, zero_ref.dtype)

        def zfill(e, c):
            def one(r, c2):
                pltpu.make_async_copy(zero_ref.at[pl.ds(0, 1), :], xs_ref.at[pl.ds(r, 1), :], sem).start()
                return c2

            def one_wait(r, c2):
                pltpu.make_async_copy(zero_ref.at[pl.ds(0, 1), :], xs_ref.at[pl.ds(0, 1), :], sem).wait()
                return c2

            lax.fori_loop(zs_ref[e], ze_ref[e], one, 0)
            lax.fori_loop(zs_ref[e], ze_ref[e], one_wait, 0)
            return c

        def tfill(b, c):
            r0 = pl.multiple_of(b * MOE_BLK, MOE_BLK)
            pltpu.make_async_copy(zero_ref, xs_ref.at[pl.ds(r0, MOE_BLK), :], sem).start()
            return c

        def twait(b, c):
            pltpu.make_async_copy(zero_ref, xs_ref.at[pl.ds(0, MOE_BLK), :], sem).wait()
            return c

        lax.fori_loop(0, N_EXPERTS, zfill, 0)
        lax.fori_loop(nu_ref[0], n_all, tfill, 0)
        lax.fori_loop(nu_ref[0], n_all, twait, 0)

    def issue(t, c):
        src = hp_ref.at[pl.ds(t, 1), :]
        pltpu.make_async_copy(src, xs_ref.at[pl.ds(s1_ref[0, 0, t], 1), :], sem).start()
        pltpu.make_async_copy(src, xs_ref.at[pl.ds(s2_ref[0, 0, t], 1), :], sem).start()
        return c

    def drain(t, c):
        row = pltpu.make_async_copy(hp_ref.at[pl.ds(0, 1), :], xs_ref.at[pl.ds(0, 1), :], sem)
        row.wait()
        row.wait()
        return c

    lax.fori_loop(0, tm, issue, 0, unroll=8)
    lax.fori_loop(0, tm, drain, 0, unroll=8)


def _scatter_rows(zero_start, zero_end, n_used, slot1, slot2, hp, n_rows):
    t, w = hp.shape
    tm = SCAT_TM
    s3 = lambda a: a.reshape(t // tm, 1, tm)
    grid_spec = pltpu.PrefetchScalarGridSpec(
        num_scalar_prefetch=3,
        grid=(t // tm,),
        in_specs=[pl.BlockSpec((1, 1, tm), lambda i, zs, ze, nu: (i, 0, 0), memory_space=pltpu.SMEM),
                  pl.BlockSpec((1, 1, tm), lambda i, zs, ze, nu: (i, 0, 0), memory_space=pltpu.SMEM),
                  pl.BlockSpec((tm, w), lambda i, zs, ze, nu: (i, 0))],
        out_specs=pl.BlockSpec(memory_space=pl.ANY),
        scratch_shapes=[pltpu.VMEM((MOE_BLK, w), U32), pltpu.SemaphoreType.DMA(())],
    )
    return pl.pallas_call(
        _scatter_kernel,
        out_shape=jax.ShapeDtypeStruct((n_rows, w), U32),
        grid_spec=grid_spec,
        compiler_params=_cp(("arbitrary",)),
        name="scatter_rows",
    )(zero_start, zero_end, n_used, s3(slot1), s3(slot2), hp)


FF_TN = 256


def _unpack_rows(x_ref, xs_ref):
    u = x_ref[...]
    xs_ref[:, :HALF_D] = pltpu.bitcast(u << 16, F32).astype(BF16)
    xs_ref[:, HALF_D:] = pltpu.bitcast(u & jnp.uint32(0xFFFF0000), F32).astype(BF16)


def _ffn_up_kernel(be_ref, nu_ref, x_ref, wg_ref, wu_ref, o_ref, wgb_ref, wub_ref, xs_ref):
    i = pl.program_id(1)

    @pl.when(i < nu_ref[0])
    def _():
        @pl.when((i == 0) | (be_ref[i] != be_ref[jnp.maximum(i - 1, 0)]))
        def _():
            wgb_ref[...] = wg_ref[0, 0].astype(BF16)
            wub_ref[...] = wu_ref[0, 0].astype(BF16)

        _unpack_rows(x_ref, xs_ref)
        x = xs_ref[...]
        g = jnp.dot(x, wgb_ref[...], preferred_element_type=F32)
        u = jnp.dot(x, wub_ref[...], preferred_element_type=F32)
        o_ref[...] = (g * _sigmoid(g) * u).astype(o_ref.dtype)

    @pl.when(i >= nu_ref[0])
    def _():
        o_ref[...] = jnp.zeros(o_ref.shape, o_ref.dtype)


def _ffn_up(block_exp, n_used, xs, w_gate, w_up, layer, n_blocks):
    d, ff = w_gate.shape[2], w_gate.shape[3]
    blk = lambda i, nu: jnp.minimum(i, nu[0] - 1)
    grid_spec = pltpu.PrefetchScalarGridSpec(
        num_scalar_prefetch=2,
        grid=(ff // FF_TN, n_blocks),
        in_specs=[pl.BlockSpec((MOE_BLK, HALF_D), lambda j, i, be, nu: (blk(i, nu), 0)),
                  pl.BlockSpec((1, 1, d, FF_TN), lambda j, i, be, nu: (layer, be[blk(i, nu)], 0, j)),
                  pl.BlockSpec((1, 1, d, FF_TN), lambda j, i, be, nu: (layer, be[blk(i, nu)], 0, j))],
        out_specs=pl.BlockSpec((MOE_BLK, FF_TN), lambda j, i, be, nu: (i, j)),
        scratch_shapes=[pltpu.VMEM((d, FF_TN), BF16), pltpu.VMEM((d, FF_TN), BF16),
                        pltpu.VMEM((MOE_BLK, d), BF16)],
    )
    return pl.pallas_call(
        _ffn_up_kernel,
        out_shape=jax.ShapeDtypeStruct((n_blocks * MOE_BLK, ff), BF16),
        grid_spec=grid_spec,
        compiler_params=_cp(("arbitrary", "arbitrary")),
        name="ffn_up",
    )(block_exp, n_used, xs, w_gate, w_up)


def _ffn_down_kernel(be_ref, nu_ref, h_ref, wd_ref, y_ref, wdb_ref):
    i = pl.program_id(0)

    @pl.when(i < nu_ref[0])
    def _():
        @pl.when((i == 0) | (be_ref[i] != be_ref[jnp.maximum(i - 1, 0)]))
        def _():
            wdb_ref[...] = wd_ref[0, 0].astype(BF16)

        y_ref[...] = jnp.dot(h_ref[...], wdb_ref[...], preferred_element_type=F32)

    @pl.when(i >= nu_ref[0])
    def _():
        y_ref[...] = jnp.zeros(y_ref.shape, y_ref.dtype)


def _ffn_down(block_exp, n_used, hmid, w_down, layer, n_blocks):
    ff, d = w_down.shape[2], w_down.shape[3]
    blk = lambda i, nu: jnp.minimum(i, nu[0] - 1)
    grid_spec = pltpu.PrefetchScalarGridSpec(
        num_scalar_prefetch=2,
        grid=(n_blocks,),
        in_specs=[pl.BlockSpec((MOE_BLK, ff), lambda i, be, nu: (blk(i, nu), 0)),
                  pl.BlockSpec((1, 1, ff, d), lambda i, be, nu: (layer, be[blk(i, nu)], 0, 0))],
        out_specs=pl.BlockSpec((MOE_BLK, d), lambda i, be, nu: (i, 0)),
        scratch_shapes=[pltpu.VMEM((ff, d), BF16)],
    )
    return pl.pallas_call(
        _ffn_down_kernel,
        out_shape=jax.ShapeDtypeStruct((n_blocks * MOE_BLK, d), F32),
        grid_spec=grid_spec,
        compiler_params=_cp(("arbitrary",)),
        name="ffn_down",
    )(block_exp, n_used, hmid, w_down)


COMB_TM = 256


def _combine_kernel(s1_ref, s2_ref, x_ref, wc_ref, fw_ref, y_ref, o_ref, ya_ref, yb_ref, sem, *, final):
    tm = x_ref.shape[0]

    def issue(t, c):
        pltpu.make_async_copy(y_ref.at[pl.ds(s1_ref[0, 0, t], 1), :], ya_ref.at[pl.ds(t, 1), :], sem).start()
        pltpu.make_async_copy(y_ref.at[pl.ds(s2_ref[0, 0, t], 1), :], yb_ref.at[pl.ds(t, 1), :], sem).start()
        return c

    def drain(t, c):
        row = pltpu.make_async_copy(y_ref.at[pl.ds(0, 1), :], ya_ref.at[pl.ds(0, 1), :], sem)
        row.wait()
        row.wait()
        return c

    lax.fori_loop(0, tm, issue, 0, unroll=8)
    lax.fori_loop(0, tm, drain, 0, unroll=8)
    o = x_ref[...] + wc_ref[:, 0:1] * ya_ref[...] + wc_ref[:, 1:2] * yb_ref[...]
    if final:
        o = o * lax.rsqrt(jnp.mean(o * o, axis=-1, keepdims=True) + EPS) * fw_ref[...]
    o_ref[...] = o


def _combine(slot1, slot2, x, wcol, y, final_w, final):
    t, d = x.shape
    tm = COMB_TM
    s3 = lambda a: a.reshape(t // tm, 1, tm)
    return pl.pallas_call(
        functools.partial(_combine_kernel, final=final),
        out_shape=jax.ShapeDtypeStruct((t, d), F32),
        grid=(t // tm,),
        in_specs=[pl.BlockSpec((1, 1, tm), lambda i: (i, 0, 0), memory_space=pltpu.SMEM),
                  pl.BlockSpec((1, 1, tm), lambda i: (i, 0, 0), memory_space=pltpu.SMEM),
                  pl.BlockSpec((tm, d), lambda i: (i, 0)),
                  pl.BlockSpec((tm, LANES), lambda i: (i, 0)),
                  pl.BlockSpec((1, d), lambda i: (0, 0)),
                  pl.BlockSpec(memory_space=pl.ANY)],
        out_specs=pl.BlockSpec((tm, d), lambda i: (i, 0)),
        scratch_shapes=[pltpu.VMEM((tm, d), F32), pltpu.VMEM((tm, d), F32),
                        pltpu.SemaphoreType.DMA(())],
        compiler_params=_cp(("arbitrary",)),
        name="combine",
    )(s3(slot1), s3(slot2), x, wcol, final_w.reshape(1, d), y)


def _mixer_layer(xf, batch, seq, norm_w, w_in, dn_conv_w, dn_a_log, dn_dt_bias, dn_norm_w, ml_igate_b,
                 ml_fgate_b, ml_norm_w, sc_conv_w, w_out):
    c0 = 3 * DN_WIDTH + DN_WIDTH
    c1 = c0 + 4 * DN_HEADS
    c2 = c1 + 2 * ML_HEADS * ML_QK + 2 * ML_WIDTH
    c3 = c2 + 4 * ML_HEADS
    w_main = jnp.concatenate([w_in[:, :c0], w_in[:, c1:c2], w_in[:, c3:]], axis=1).astype(BF16)
    w_gate = _pack_gate_cols(w_in[:, c0:c0 + 2 * DN_HEADS], w_in[:, c0 + 2 * DN_HEADS:c1],
                             w_in[:, c2:c2 + 2 * ML_HEADS], w_in[:, c2 + 2 * ML_HEADS:c3]).astype(BF16)
    h = _rmsnorm(xf, norm_w, BF16)
    proj, graw = _inproj(h, w_main, w_gate)
    gc, gr = _gate_prep(graw, _gate_params(dn_a_log, dn_dt_bias, ml_igate_b, ml_fgate_b))
    dn = _deltanet(proj, dn_conv_w, gc, gr, dn_norm_w, batch, seq)
    ml = _mlstm(proj, gc, gr, ml_norm_w, batch, seq)
    sc = _shortconv(proj, sc_conv_w, batch, seq)
    return _outproj(dn, ml, sc, w_out.astype(BF16), xf)


def _moe_layer(xf, norm_w, rg_w, rg_b, re_w, re_b, w_gate, w_up, w_down, layer, final_w, final):
    t, d = xf.shape
    wr = jnp.zeros((d, LANES), F32).at[:, :N_GROUPS].set(rg_w).at[:, EXP_ROW0:EXP_ROW0 + N_EXPERTS].set(re_w)
    br = jnp.zeros((1, LANES), F32).at[0, :N_GROUPS].set(rg_b).at[0, EXP_ROW0:EXP_ROW0 + N_EXPERTS].set(re_b)
    hp, meta, wcol, cnt = _router(xf, norm_w, wr, br)
    counts = cnt[:, 0].astype(I32)
    padded = (counts + MOE_BLK - 1) // MOE_BLK * MOE_BLK
    pad_end = jnp.cumsum(padded)
    pad_start = pad_end - padded
    n_blocks = -(-(2 * t + N_EXPERTS * (MOE_BLK - 1)) // MOE_BLK)
    n_used = (pad_end[-1:] // MOE_BLK).astype(I32)
    block_exp = jnp.minimum(jnp.searchsorted(pad_end, jnp.arange(n_blocks, dtype=I32) * MOE_BLK, side="right"),
                            N_EXPERTS - 1).astype(I32)
    slot1 = pad_start[meta[0]] + meta[2]
    slot2 = pad_start[meta[1]] + meta[3]
    xs = _scatter_rows(pad_start + counts, pad_end, n_used, slot1, slot2, hp, n_blocks * MOE_BLK)
    hmid = _ffn_up(block_exp, n_used, xs, w_gate, w_up, layer, n_blocks)
    y = _ffn_down(block_exp, n_used, hmid, w_down, layer, n_blocks)
    return _combine(slot1, slot2, xf, wcol, y, final_w, final)


def kernel(x, norm_mix_w, w_in, dn_conv_w, dn_a_log, dn_dt_bias, dn_norm_w, ml_igate_b, ml_fgate_b, ml_norm_w,
           sc_conv_w, w_out, norm_ffn_w, router_group_w, router_group_b, router_expert_w, router_expert_b,
           expert_w_gate, expert_w_up, expert_w_down, final_norm_w):
    batch, seq, d = x.shape
    depth = w_in.shape[0]
    xf = x.reshape(batch * seq, d)
    for l in range(depth):
        xf = _mixer_layer(xf, batch, seq, norm_mix_w[l], w_in[l], dn_conv_w[l], dn_a_log[l], dn_dt_bias[l],
                          dn_norm_w[l], ml_igate_b[l], ml_fgate_b[l], ml_norm_w[l], sc_conv_w[l], w_out[l])
        xf = _moe_layer(xf, norm_ffn_w[l], router_group_w[l], router_group_b[l], router_expert_w[l],
                        router_expert_b[l], expert_w_gate, expert_w_up, expert_w_down, l,
                        final_norm_w, l == depth - 1)
    return xf.reshape(batch, seq, d)
```

```python
import functools

import numpy as np
import jax
import jax.numpy as jnp
from jax import lax
from jax.experimental import pallas as pl
from jax.experimental.pallas import tpu as pltpu

F32 = jnp.float32
BF16 = jnp.bfloat16
I32 = jnp.int32
U32 = jnp.uint32

D_MODEL = 4096
DN_HEADS = 12
DN_DIM = 128
DN_WIDTH = DN_HEADS * DN_DIM
DN_CONV = 5
ML_HEADS = 6
ML_QK = 128
ML_V = 256
ML_WIDTH = ML_HEADS * ML_V
SC_WIDTH = 1024
SC_CONV = 3
N_GROUPS = 4
EXPERTS_PER_GROUP = 8
N_EXPERTS = N_GROUPS * EXPERTS_PER_GROUP
D_FF = 768
EPS = 1e-6

LANES = 128
SUBLANES = 8
VMEM_LIMIT = 52 * 1024 * 1024

CH = 64
TL = 256
CPT = TL // CH
DN_TPI = 2
N_HEADS_ALL = DN_HEADS + ML_HEADS
MOE_BLK = 256
NEG = -1e30

OFF_DN_Q, OFF_DN_K, OFF_DN_V, OFF_DN_Z = 0, 1536, 3072, 4608
OFF_ML_Q, OFF_ML_K, OFF_ML_V, OFF_ML_O = 6144, 6912, 7680, 9216
OFF_SC_B, OFF_SC_C, OFF_SC_X = 10752, 11776, 12800
N_MAIN = 13824


def _cp(sem, vmem=VMEM_LIMIT):
    return pltpu.CompilerParams(dimension_semantics=sem, vmem_limit_bytes=vmem)


def _mm(a, b):
    return jnp.dot(a.astype(BF16), b.astype(BF16), preferred_element_type=F32)


def _mm_nt(a, b):
    return lax.dot_general(a.astype(BF16), b.astype(BF16), (((1,), (1,)), ((), ())),
                           preferred_element_type=F32)


def _mm_f32(a, b):
    return jnp.dot(a, b, preferred_element_type=F32, precision=lax.Precision.HIGHEST)


def _sigmoid(x):
    return 1.0 / (1.0 + jnp.exp(-x))


def _softplus(x):
    return jnp.maximum(x, 0.0) + jnp.log(1.0 + jnp.exp(-jnp.abs(x)))


def _rms_kernel(x_ref, w_ref, o_ref):
    x = x_ref[...]
    ms = jnp.mean(x * x, axis=-1, keepdims=True)
    o_ref[...] = (x * lax.rsqrt(ms + EPS) * w_ref[...]).astype(o_ref.dtype)


def _rmsnorm(x2d, w, out_dtype, tm=256):
    t, d = x2d.shape
    return pl.pallas_call(
        _rms_kernel,
        out_shape=jax.ShapeDtypeStruct((t, d), out_dtype),
        grid=(t // tm,),
        in_specs=[pl.BlockSpec((tm, d), lambda i: (i, 0)), pl.BlockSpec((1, d), lambda i: (0, 0))],
        out_specs=pl.BlockSpec((tm, d), lambda i: (i, 0)),
        compiler_params=_cp(("arbitrary",)),
        name="rmsnorm",
    )(x2d, w.reshape(1, d))


def _inproj_kernel(a_ref, w_ref, wg_ref, o_ref, g_ref):
    a = a_ref[...]
    o_ref[...] = jnp.dot(a, w_ref[...], preferred_element_type=F32).astype(o_ref.dtype)

    @pl.when(pl.program_id(1) == 0)
    def _():
        g_ref[...] = jnp.dot(a, wg_ref[...], preferred_element_type=F32)


def _inproj(h, w_main, w_gate, tm=1024, tn=512):
    t, d = h.shape
    n = w_main.shape[1]
    ng = w_gate.shape[1]
    return pl.pallas_call(
        _inproj_kernel,
        out_shape=(jax.ShapeDtypeStruct((t, n), BF16), jax.ShapeDtypeStruct((t, ng), F32)),
        grid=(t // tm, n // tn),
        in_specs=[pl.BlockSpec((tm, d), lambda i, j: (i, 0)),
                  pl.BlockSpec((d, tn), lambda i, j: (0, j)),
                  pl.BlockSpec((d, ng), lambda i, j: (0, 0))],
        out_specs=(pl.BlockSpec((tm, tn), lambda i, j: (i, j)),
                   pl.BlockSpec((tm, ng), lambda i, j: (i, 0))),
        compiler_params=_cp(("arbitrary", "arbitrary")),
        name="inproj",
    )(h, w_main, w_gate)


def _gate_kernel(raw_ref, p_ref, gc_ref, gr_ref):
    x = raw_ref[...]
    lane = lax.broadcasted_iota(I32, x.shape, 1)
    chn = lane & 7
    is_ml = lane >= LANES
    a_log, dt_b, ig_b, fg_b = p_ref[0:1, :], p_ref[1:2, :], p_ref[2:3, :], p_ref[3:4, :]
    beta = _sigmoid(x)
    g = -jnp.exp(a_log) * _softplus(x + dt_b)
    ig = x + ig_b
    lf = -_softplus(-(x + fg_b))
    lo = chn < 2
    val = jnp.where(is_ml, jnp.where(lo, ig, lf), jnp.where(lo, beta, g))
    val = jnp.where(chn < 4, val, 0.0)
    ri = lax.broadcasted_iota(I32, (TL, TL), 0)
    ci = lax.broadcasted_iota(I32, (TL, TL), 1)
    same = (ri >> 6) == (ci >> 6)
    m_f = jnp.where(same & (ri >= ci), 1.0, 0.0)
    m_b = jnp.where(same & (ri <= ci), 1.0, 0.0)
    m_t = jnp.where(same, 1.0, 0.0)
    cum_f = _mm_f32(m_f, val)
    cum_b = _mm_f32(m_b, val)
    tot = pltpu.roll(_mm_f32(m_t, val), 2, axis=1)
    out = jnp.where(chn == 2, cum_f, jnp.where(chn == 3, cum_b, jnp.where((chn == 4) | (chn == 5), tot, val)))
    gr_ref[0] = out.T
    for hh in range(N_HEADS_ALL):
        grp, h = (0, hh) if hh < DN_HEADS else (1, hh - DN_HEADS)
        v = out[:, grp * LANES:(grp + 1) * LANES]
        sh = (LANES - 8 * h) % LANES
        gc_ref[hh] = pltpu.roll(v, sh, axis=1) if sh else v


def _gate_prep(raw, params):
    t = raw.shape[0]
    nt = t // TL
    return pl.pallas_call(
        _gate_kernel,
        out_shape=(jax.ShapeDtypeStruct((N_HEADS_ALL, t, LANES), F32),
                   jax.ShapeDtypeStruct((nt, 2 * LANES, TL), F32)),
        grid=(nt,),
        in_specs=[pl.BlockSpec((TL, 2 * LANES), lambda i: (i, 0)),
                  pl.BlockSpec((8, 2 * LANES), lambda i: (0, 0))],
        out_specs=(pl.BlockSpec((N_HEADS_ALL, TL, LANES), lambda i: (0, i, 0)),
                   pl.BlockSpec((1, 2 * LANES, TL), lambda i: (i, 0, 0))),
        compiler_params=_cp(("arbitrary",)),
        name="gate_prep",
    )(raw, params)


def _conv_centred(x_ref, w_ref, pad_ref, width):
    s = x_ref.shape[0]
    half = width // 2
    pad_ref[0:8, :] = jnp.zeros((8, pad_ref.shape[1]), F32)
    pad_ref[s + 8:s + 16, :] = jnp.zeros((8, pad_ref.shape[1]), F32)
    pad_ref[8:s + 8, :] = x_ref[...].astype(F32)
    acc = None
    for i in range(width):
        term = w_ref[i:i + 1, :] * pad_ref[8 - half + i:8 - half + i + s, :]
        acc = term if acc is None else acc + term
    return acc


def _tile_masks():
    ri = lax.broadcasted_iota(I32, (TL, TL), 0)
    ci = lax.broadcasted_iota(I32, (TL, TL), 1)
    return ri, ci


def _unit_tri_inverse(mats, same16, same32, eye):
    each = lambda f, *ls: [f(*xs) for xs in zip(*ls)]
    a16 = each(lambda a: jnp.where(same16, a, 0.0).astype(BF16), mats)
    b1 = each(lambda x: _mm(x, x).astype(BF16), a16)
    p = each(lambda x: eye - x.astype(F32), a16)
    b2 = each(lambda x: _mm(x, x).astype(BF16), b1)
    p = each(lambda x, b: x + _mm(x, b), p, b1)
    b3 = each(lambda x: _mm(x, x).astype(BF16), b2)
    p = each(lambda x, b: x + _mm(x, b), p, b2)
    p = each(lambda x, b: x + _mm(x, b), p, b3)
    off32 = same32 & jnp.logical_not(same16)
    a32 = each(lambda a: jnp.where(off32, a, 0.0).astype(BF16), mats)
    pb = each(lambda x: x.astype(BF16), p)
    t = each(_mm, pb, a32)
    p = each(lambda x, y, z: x - _mm(y, z), p, t, pb)
    a64 = each(lambda a: jnp.where(same32, 0.0, a).astype(BF16), mats)
    pb = each(lambda x: x.astype(BF16), p)
    t = each(_mm, pb, a64)
    p = each(lambda x, y, z: x - _mm(y, z), p, t, pb)
    return p


def _dn_kernel(q_ref, k_ref, v_ref, z_ref, cq_ref, ck_ref, cv_ref, gc_ref, gr_ref, nw_ref, o_ref,
               pad_ref, qn_ref, kn_ref, vv_ref, u_ref, wq_ref, ak_ref, eg_ref, oo_ref):
    s = q_ref.shape[0]
    nt = s // TL
    nc = s // CH
    scale = DN_DIM ** -0.5

    q = _conv_centred(q_ref, cq_ref, pad_ref, DN_CONV)
    q = q * _sigmoid(q)
    qn_ref[...] = q * (lax.rsqrt(jnp.sum(q * q, axis=-1, keepdims=True) + EPS) * scale)
    k = _conv_centred(k_ref, ck_ref, pad_ref, DN_CONV)
    k = k * _sigmoid(k)
    kn_ref[...] = k * lax.rsqrt(jnp.sum(k * k, axis=-1, keepdims=True) + EPS)
    v = _conv_centred(v_ref, cv_ref, pad_ref, DN_CONV)
    vv_ref[...] = v * _sigmoid(v)

    def tile_body(j, carry):
        ri, ci = _tile_masks()
        same64 = (ri >> 6) == (ci >> 6)
        same32 = (ri >> 5) == (ci >> 5)
        same16 = (ri >> 4) == (ci >> 4)
        eye = jnp.where(ri == ci, 1.0, 0.0)
        incl = [same64 & (ri >= ci), same64 & (ri <= ci)]
        strict = [same64 & (ri > ci), same64 & (ri < ci)]
        tiles = [j * DN_TPI + t for t in range(DN_TPI)]
        r0 = [pl.multiple_of(i * TL, TL) for i in tiles]
        qn = [qn_ref[pl.ds(r, TL), :] for r in r0]
        kn = [kn_ref[pl.ds(r, TL), :] for r in r0]
        vv = [vv_ref[pl.ds(r, TL), :] for r in r0]
        kk = [_mm_nt(k, k) for k in kn]
        qk = [_mm_nt(q, k) for q, k in zip(qn, kn)]
        gc = [gc_ref[0, pl.ds(r, TL), :] for r in r0]
        gr = [gr_ref[i] for i in tiles]
        ch = [(t, d) for t in range(DN_TPI) for d in (0, 1)]
        beta_c = [gc[t][:, d:d + 1] for t, d in ch]
        g_c = [gc[t][:, 2 + d:3 + d] for t, d in ch]
        gt_c = [gc[t][:, 4 + d:5 + d] for t, d in ch]
        g_r = [gr[t][2 + d:3 + d, :] for t, d in ch]
        dec = [jnp.exp(jnp.where(incl[d], g_c[n] - g_r[n], NEG)) for n, (t, d) in enumerate(ch)]
        a = [jnp.where(strict[d], beta_c[n] * kk[t] * dec[n], 0.0) for n, (t, d) in enumerate(ch)]
        tinv = _unit_tri_inverse(a, same16, same32, eye)
        eg_c = [jnp.exp(g) for g in g_c]
        rhs = [jnp.concatenate([beta_c[n] * vv[t], (beta_c[n] * eg_c[n]) * kn[t]], axis=1)
               for n, (t, d) in enumerate(ch)]
        uw = [_mm(x, y) for x, y in zip(tinv, rhs)]
        for n, (t, d) in enumerate(ch):
            i = tiles[t]
            u_ref[d, pl.ds(r0[t], TL), :] = uw[n][:, :DN_DIM]
            w = uw[n][:, DN_DIM:]
            qe = qn[t] * eg_c[n]
            attn = qk[t] * dec[n]
            kdt = (kn[t] * jnp.exp(gt_c[n] - g_c[n])).T
            egt = jnp.exp(gt_c[n])
            for c in range(CPT):
                rows = slice(c * CH, (c + 1) * CH)
                wq_ref[d, i * CPT + c] = jnp.concatenate([w[rows], qe[rows]], axis=0).astype(BF16)
                ak_ref[d, i * CPT + c] = jnp.concatenate([attn[rows, rows], kdt[:, rows]], axis=0).astype(BF16)
                eg_ref[d, i * CPT + c] = jnp.broadcast_to(egt[c * CH:c * CH + 8, :], (8, LANES))
        return carry

    lax.fori_loop(0, nt // DN_TPI, tile_body, 0)

    def chunk_body(c, carry):
        states = list(carry)
        for d in range(2):
            cc = c if d == 0 else nc - 1 - c
            r = pl.multiple_of(cc * CH, CH)
            st = states[d]
            ws = jnp.dot(wq_ref[d, cc], st.astype(BF16), preferred_element_type=F32)
            vnew = u_ref[d, pl.ds(r, CH), :] - ws[:CH]
            m2 = jnp.dot(ak_ref[d, cc], vnew.astype(BF16), preferred_element_type=F32)
            oo_ref[d, pl.ds(r, CH), :] = ws[CH:] + m2[:CH]
            states[d] = eg_ref[d, cc][0:1, :] * st + m2[CH:]
        return tuple(states)

    z0 = jnp.zeros((DN_DIM, DN_DIM), F32)
    lax.fori_loop(0, nc, chunk_body, (z0, z0))

    o = oo_ref[0] + oo_ref[1]
    o = o * lax.rsqrt(jnp.mean(o * o, axis=-1, keepdims=True) + EPS) * nw_ref[...]
    z = z_ref[...].astype(F32)
    o_ref[...] = (o * (z * _sigmoid(z))).astype(o_ref.dtype)


def _deltanet(proj, conv_w, gc, gr, norm_w, batch, seq):
    t = proj.shape[0]
    nc = seq // CH
    nt = seq // TL
    qb, kb, vb, zb = (OFF_DN_Q // DN_DIM, OFF_DN_K // DN_DIM, OFF_DN_V // DN_DIM, OFF_DN_Z // DN_DIM)
    col = lambda off: pl.BlockSpec((seq, DN_DIM), lambda b, h, off=off: (b, off + h))
    cw = lambda off: pl.BlockSpec((DN_CONV, DN_DIM), lambda b, h, off=off: (0, off + h))
    return pl.pallas_call(
        _dn_kernel,
        out_shape=jax.ShapeDtypeStruct((t, DN_WIDTH), BF16),
        grid=(batch, DN_HEADS),
        in_specs=[col(qb), col(kb), col(vb), col(zb), cw(0), cw(DN_HEADS), cw(2 * DN_HEADS),
                  pl.BlockSpec((1, seq, LANES), lambda b, h: (h, b, 0)),
                  pl.BlockSpec((nt, 8, TL), lambda b, h: (b, h, 0)),
                  pl.BlockSpec((1, DN_DIM), lambda b, h: (0, 0))],
        out_specs=pl.BlockSpec((seq, DN_DIM), lambda b, h: (b, h)),
        scratch_shapes=[pltpu.VMEM((seq + 16, DN_DIM), F32),
                        pltpu.VMEM((seq, DN_DIM), F32), pltpu.VMEM((seq, DN_DIM), F32),
                        pltpu.VMEM((seq, DN_DIM), F32),
                        pltpu.VMEM((2, seq, DN_DIM), F32),
                        pltpu.VMEM((2, nc, 2 * CH, DN_DIM), BF16),
                        pltpu.VMEM((2, nc, CH + DN_DIM, CH), BF16),
                        pltpu.VMEM((2, nc, 8, LANES), F32),
                        pltpu.VMEM((2, seq, DN_DIM), F32)],
        compiler_params=_cp(("arbitrary", "arbitrary")),
        name="deltanet",
    )(proj, proj, proj, proj, conv_w, conv_w, conv_w, gc, gr, norm_w.reshape(1, DN_DIM))


def _ml_kernel(q_ref, k_ref, v_ref, og_ref, gc_ref, gr_ref, nw_ref, o_ref,
               ak_ref, rs_ref, cs_ref, dn_ref, oo_ref):
    s = q_ref.shape[0]
    nt = s // TL
    nc = s // CH
    scale = ML_QK ** -0.5

    def tile_body(it, carry):
        m_carry = list(carry)
        ri, ci = _tile_masks()
        same64 = (ri >> 6) == (ci >> 6)
        rowc = lax.broadcasted_iota(I32, (TL, 1), 0) >> 6
        lane = lax.broadcasted_iota(I32, (TL, LANES), 1)
        dirs = (0, 1)
        tile = [it, nt - 1 - it]
        r0 = [pl.multiple_of(i * TL, TL) for i in tile]
        qs = [q_ref[pl.ds(r, TL), :].astype(F32) * scale for r in r0]
        kf = [k_ref[pl.ds(r, TL), :].astype(F32) for r in r0]
        qk = [_mm_nt(a, b) for a, b in zip(qs, kf)]
        gc = [gc_ref[0, pl.ds(r, TL), :] for r in r0]
        gr = [gr_ref[i] for i in tile]
        ig_c = [gc[d][:, d:d + 1] for d in dirs]
        b_c = [gc[d][:, 2 + d:3 + d] for d in dirs]
        bt_c = [gc[d][:, 4 + d:5 + d] for d in dirs]
        ig_r = [gr[d][d:d + 1, :] for d in dirs]
        b_r = [gr[d][2 + d:3 + d, :] for d in dirs]
        bt_r = [gr[d][4 + d:5 + d, :] for d in dirs]
        incl = [same64 & (ri >= ci), same64 & (ri <= ci)]
        dmat = [jnp.where(incl[d], b_c[d] - b_r[d] + ig_r[d], NEG) for d in dirs]
        m_intra = [jnp.max(x, axis=1, keepdims=True) for x in dmat]
        d_end_c = [bt_c[d] - b_c[d] + ig_c[d] for d in dirs]
        d_end_r = [bt_r[d] - b_r[d] + ig_r[d] for d in dirs]
        dmax_c = [jnp.max(jnp.where(same64, x, NEG), axis=1, keepdims=True) for x in d_end_r]
        m_in_c, m_out_c = [], []
        for d in dirs:
            m_in = jnp.zeros((TL, 1), F32)
            m_out = jnp.zeros((TL, 1), F32)
            m = m_carry[d]
            for cc in (range(CPT) if d == 0 else range(CPT - 1, -1, -1)):
                btot = bt_c[d][cc * CH:cc * CH + 1, :]
                dmx = dmax_c[d][cc * CH:cc * CH + 1, :]
                m_new = jnp.maximum(btot + m, dmx)
                m_in = jnp.where(rowc == cc, m, m_in)
                m_out = jnp.where(rowc == cc, m_new, m_out)
                m = m_new
            m_carry[d] = m
            m_in_c.append(m_in)
            m_out_c.append(m_out)
        m_inter = [b_c[d] + m_in_c[d] for d in dirs]
        m_pos = [jnp.maximum(m_intra[d], m_inter[d]) for d in dirs]
        scores = [qk[d] * jnp.exp(dmat[d] - m_pos[d]) for d in dirs]
        inter_scale = [jnp.exp(m_inter[d] - m_pos[d]) for d in dirs]
        rsum = [jnp.sum(x, axis=1, keepdims=True) for x in scores]
        einv = [jnp.exp(-x) for x in m_pos]
        cscale = [jnp.exp(bt_c[d] + m_in_c[d] - m_out_c[d]) for d in dirs]
        wk = [kf[d] * jnp.exp(d_end_c[d] - m_out_c[d]) for d in dirs]
        wkt = [x.T for x in wk]
        for d in dirs:
            i = tile[d]
            rs_ref[d, pl.ds(r0[d], TL), :] = jnp.where(lane == 0, inter_scale[d],
                                                      jnp.where(lane == 1, rsum[d], einv[d]))
            for c in range(CPT):
                rows = slice(c * CH, (c + 1) * CH)
                ak_ref[d, i * CPT + c] = jnp.concatenate([scores[d][rows, rows], wkt[d][:, rows]],
                                                         axis=0).astype(BF16)
                cs_ref[d, i * CPT + c] = jnp.broadcast_to(cscale[d][c * CH:c * CH + 8, :], (8, LANES))
                dn_ref[d, i * CPT + c] = jnp.broadcast_to(jnp.sum(wk[d][rows], axis=0, keepdims=True),
                                                          (8, ML_QK))
        return tuple(m_carry)

    zero11 = jnp.zeros((1, 1), F32)
    lax.fori_loop(0, nt, tile_body, (zero11, zero11))

    def chunk_body(c, carry):
        cst = [carry[0], carry[1]]
        nst = [carry[2], carry[3]]
        for d in range(2):
            cc = c if d == 0 else nc - 1 - c
            r = pl.multiple_of(cc * CH, CH)
            qb = q_ref[pl.ds(r, CH), :]
            inter = jnp.dot(qb, cst[d].astype(BF16), preferred_element_type=F32) * scale
            m2 = jnp.dot(ak_ref[d, cc], v_ref[pl.ds(r, CH), :], preferred_element_type=F32)
            qn = jnp.sum(qb.astype(F32) * nst[d], axis=1, keepdims=True) * scale
            rs = rs_ref[d, pl.ds(r, CH), :]
            isc = rs[:, 0:1]
            num = m2[:CH] + isc * inter
            den = rs[:, 1:2] + isc * qn
            oo_ref[d, pl.ds(r, CH), :] = num / jnp.maximum(jnp.abs(den), rs[:, 2:3])
            csc = cs_ref[d, cc][0:1, 0:1]
            cst[d] = csc * cst[d] + m2[CH:]
            nst[d] = csc * nst[d] + dn_ref[d, cc][0:1, :]
        return (cst[0], cst[1], nst[0], nst[1])

    c0 = jnp.zeros((ML_QK, ML_V), F32)
    n0 = jnp.zeros((1, ML_QK), F32)
    lax.fori_loop(0, nc, chunk_body, (c0, c0, n0, n0))

    o = oo_ref[0] + oo_ref[1]
    o = o * lax.rsqrt(jnp.mean(o * o, axis=-1, keepdims=True) + EPS) * nw_ref[...]
    o_ref[...] = (o * _sigmoid(og_ref[...].astype(F32))).astype(o_ref.dtype)


def _mlstm(proj, gc, gr, norm_w, batch, seq):
    t = proj.shape[0]
    nc = seq // CH
    nt = seq // TL
    qb, kb, vb, ob = OFF_ML_Q // ML_QK, OFF_ML_K // ML_QK, OFF_ML_V // ML_V, OFF_ML_O // ML_V
    return pl.pallas_call(
        _ml_kernel,
        out_shape=jax.ShapeDtypeStruct((t, ML_WIDTH), BF16),
        grid=(batch, ML_HEADS),
        in_specs=[pl.BlockSpec((seq, ML_QK), lambda b, h: (b, qb + h)),
                  pl.BlockSpec((seq, ML_QK), lambda b, h: (b, kb + h)),
                  pl.BlockSpec((seq, ML_V), lambda b, h: (b, vb + h)),
                  pl.BlockSpec((seq, ML_V), lambda b, h: (b, ob + h)),
                  pl.BlockSpec((1, seq, LANES), lambda b, h: (DN_HEADS + h, b, 0)),
                  pl.BlockSpec((nt, 8, TL), lambda b, h: (b, LANES // 8 + h, 0)),
                  pl.BlockSpec((1, ML_V), lambda b, h: (0, h))],
        out_specs=pl.BlockSpec((seq, ML_V), lambda b, h: (b, h)),
        scratch_shapes=[pltpu.VMEM((2, nc, CH + ML_QK, CH), BF16),
                        pltpu.VMEM((2, seq, LANES), F32),
                        pltpu.VMEM((2, nc, 8, LANES), F32),
                        pltpu.VMEM((2, nc, 8, ML_QK), F32),
                        pltpu.VMEM((2, seq, ML_V), F32)],
        compiler_params=_cp(("arbitrary", "arbitrary")),
        name="mlstm",
    )(proj, proj, proj, proj, gc, gr, norm_w.reshape(1, ML_WIDTH))


def _sc_kernel(b_ref, c_ref, x_ref, w_ref, o_ref, cx_ref, pad_ref):
    cx_ref[...] = c_ref[...].astype(F32) * x_ref[...].astype(F32)
    y = _conv_centred(cx_ref, w_ref, pad_ref, SC_CONV)
    o_ref[...] = (b_ref[...].astype(F32) * y).astype(o_ref.dtype)


def _shortconv(proj, conv_w, batch, seq, wc=256):
    t = proj.shape[0]
    bb, cb, xb = OFF_SC_B // wc, OFF_SC_C // wc, OFF_SC_X // wc
    col = lambda off: pl.BlockSpec((seq, wc), lambda b, j, off=off: (b, off + j))
    return pl.pallas_call(
        _sc_kernel,
        out_shape=jax.ShapeDtypeStruct((t, SC_WIDTH), BF16),
        grid=(batch, SC_WIDTH // wc),
        in_specs=[col(bb), col(cb), col(xb), pl.BlockSpec((SC_CONV, wc), lambda b, j: (0, j))],
        out_specs=pl.BlockSpec((seq, wc), lambda b, j: (b, j)),
        scratch_shapes=[pltpu.VMEM((seq, wc), F32), pltpu.VMEM((seq + 16, wc), F32)],
        compiler_params=_cp(("arbitrary", "arbitrary")),
        name="shortconv",
    )(proj, proj, proj, conv_w)


def _pack_gate_cols(dn_b, dn_a, ml_i, ml_f):
    r = dn_b.shape[0]

    def grp(lo, hi, nh):
        a = jnp.stack([lo[:, :nh], lo[:, nh:], hi[:, :nh], hi[:, nh:]], axis=-1)
        a = jnp.concatenate([a, jnp.zeros((r, nh, 4), a.dtype)], axis=-1).reshape(r, nh * 8)
        return jnp.concatenate([a, jnp.zeros((r, LANES - nh * 8), a.dtype)], axis=-1)

    return jnp.concatenate([grp(dn_b, dn_a, DN_HEADS), grp(ml_i, ml_f, ML_HEADS)], axis=-1)


def _gate_params(dn_a_log, dn_dt_bias, ml_igate_b, ml_fgate_b):
    z_dn = jnp.zeros((1, 2 * DN_HEADS), F32)
    z_ml = jnp.zeros((1, 2 * ML_HEADS), F32)
    flat = lambda a: a.astype(F32).reshape(1, -1)
    rows = [_pack_gate_cols(z_dn, flat(dn_a_log), z_ml, z_ml),
            _pack_gate_cols(z_dn, flat(dn_dt_bias), z_ml, z_ml),
            _pack_gate_cols(z_dn, z_dn, flat(ml_igate_b), z_ml),
            _pack_gate_cols(z_dn, z_dn, z_ml, flat(ml_fgate_b))]
    return jnp.concatenate(rows + [jnp.zeros((4, 2 * LANES), F32)], axis=0)


def _outproj_kernel(a1_ref, a2_ref, a3_ref, w1_ref, w2_ref, w3_ref, x_ref, o_ref):
    acc = jnp.dot(a1_ref[...], w1_ref[...], preferred_element_type=F32)
    acc += jnp.dot(a2_ref[...], w2_ref[...], preferred_element_type=F32)
    acc += jnp.dot(a3_ref[...], w3_ref[...], preferred_element_type=F32)
    o_ref[...] = x_ref[...] + acc


def _outproj(dn, ml, sc, w_out, x, tm=1024, tn=512):
    t, d = x.shape
    k1, k2, k3 = dn.shape[1], ml.shape[1], sc.shape[1]
    assert k1 == k2 and (k1 + k2) % k3 == 0
    return pl.pallas_call(
        _outproj_kernel,
        out_shape=jax.ShapeDtypeStruct((t, d), F32),
        grid=(t // tm, d // tn),
        in_specs=[pl.BlockSpec((tm, k1), lambda i, j: (i, 0)),
                  pl.BlockSpec((tm, k2), lambda i, j: (i, 0)),
                  pl.BlockSpec((tm, k3), lambda i, j: (i, 0)),
                  pl.BlockSpec((k1, tn), lambda i, j: (0, j)),
                  pl.BlockSpec((k2, tn), lambda i, j: (1, j)),
                  pl.BlockSpec((k3, tn), lambda i, j: ((k1 + k2) // k3, j)),
                  pl.BlockSpec((tm, tn), lambda i, j: (i, j))],
        out_specs=pl.BlockSpec((tm, tn), lambda i, j: (i, j)),
        compiler_params=_cp(("arbitrary", "arbitrary")),
        name="outproj",
    )(dn, ml, sc, w_out, w_out, w_out, x)


ROUTE_TM = 256
EXP_ROW0 = 32
HALF_D = D_MODEL // 2


def _router_kernel(x_ref, nw_ref, wr_ref, br_ref, hp_ref, meta_ref, wc_ref, cnt_ref, base_ref):
    tm = x_ref.shape[0]

    @pl.when(pl.program_id(0) == 0)
    def _():
        base_ref[...] = jnp.zeros(base_ref.shape, F32)

    x = x_ref[...]
    h = x * lax.rsqrt(jnp.mean(x * x, axis=-1, keepdims=True) + EPS) * nw_ref[...]
    bits = pltpu.bitcast(h.astype(BF16).astype(F32), U32)
    hp_ref[...] = (bits[:, :HALF_D] >> 16) | bits[:, HALF_D:]

    lt = (_mm_f32(h, wr_ref[...]) + br_ref[...]).T
    row8 = lax.broadcasted_iota(I32, (8, tm), 0)
    gl = jnp.where(row8 < N_GROUPS, lt[0:8, :], NEG)
    gex = jnp.exp(gl - jnp.max(gl, axis=0, keepdims=True))
    gp = gex / jnp.sum(gex, axis=0, keepdims=True)
    g_w = jnp.max(gp, axis=0, keepdims=True)
    g_idx = jnp.min(jnp.where(gp == g_w, row8, 8), axis=0, keepdims=True)
    el = lt[EXP_ROW0 + 24:EXP_ROW0 + 32, :]
    for g in (2, 1, 0):
        el = jnp.where(g_idx == g, lt[EXP_ROW0 + 8 * g:EXP_ROW0 + 8 * g + 8, :], el)
    ee = jnp.exp(el - jnp.max(el, axis=0, keepdims=True))
    p = ee / jnp.sum(ee, axis=0, keepdims=True)
    p1 = jnp.max(p, axis=0, keepdims=True)
    i1 = jnp.min(jnp.where(p == p1, row8, 8), axis=0, keepdims=True)
    pm = jnp.where(row8 == i1, -1.0, p)
    p2 = jnp.max(pm, axis=0, keepdims=True)
    i2 = jnp.min(jnp.where(pm == p2, row8, 8), axis=0, keepdims=True)
    den = p1 + p2
    w1 = g_w * p1 / den
    w2 = g_w * p2 / den
    e1 = g_idx * EXPERTS_PER_GROUP + i1
    e2 = g_idx * EXPERTS_PER_GROUP + i2

    rowe = lax.broadcasted_iota(I32, (N_EXPERTS, tm), 0)
    oh1 = jnp.where(rowe == e1, 1.0, 0.0)
    oh2 = jnp.where(rowe == e2, 1.0, 0.0)
    oh = (oh1 + oh2).astype(BF16)
    ri = lax.broadcasted_iota(I32, (tm, tm), 0)
    ci = lax.broadcasted_iota(I32, (tm, tm), 1)
    before = jnp.where(ri < ci, 1.0, 0.0).astype(BF16)
    ones = jnp.ones((tm, tm), BF16)
    tot = jnp.dot(oh, before, preferred_element_type=F32) + base_ref[...]
    r1 = jnp.sum(oh1 * tot, axis=0, keepdims=True)
    r2 = jnp.sum(oh2 * tot, axis=0, keepdims=True)
    base_ref[...] = base_ref[...] + jnp.dot(oh, ones, preferred_element_type=F32)
    cnt_ref[...] = base_ref[...]

    meta_ref[...] = jnp.concatenate(
        [e1, e2, r1.astype(I32), r2.astype(I32), jnp.zeros((4, tm), I32)], axis=0)
    wrow = lax.broadcasted_iota(I32, (LANES, tm), 0)
    wc_ref[...] = jnp.where(wrow == 0, w1, jnp.where(wrow == 1, w2, 0.0)).T


def _router(x, norm_w, wr, br):
    t, d = x.shape
    tm = ROUTE_TM
    return pl.pallas_call(
        _router_kernel,
        out_shape=(jax.ShapeDtypeStruct((t, HALF_D), U32),
                   jax.ShapeDtypeStruct((8, t), I32),
                   jax.ShapeDtypeStruct((t, LANES), F32),
                   jax.ShapeDtypeStruct((N_EXPERTS, tm), F32)),
        grid=(t // tm,),
        in_specs=[pl.BlockSpec((tm, d), lambda i: (i, 0)),
                  pl.BlockSpec((1, d), lambda i: (0, 0)),
                  pl.BlockSpec((d, LANES), lambda i: (0, 0)),
                  pl.BlockSpec((1, LANES), lambda i: (0, 0))],
        out_specs=(pl.BlockSpec((tm, HALF_D), lambda i: (i, 0)),
                   pl.BlockSpec((8, tm), lambda i: (0, i)),
                   pl.BlockSpec((tm, LANES), lambda i: (i, 0)),
                   pl.BlockSpec((N_EXPERTS, tm), lambda i: (0, 0))),
        scratch_shapes=[pltpu.VMEM((N_EXPERTS, tm), F32)],
        compiler_params=_cp(("arbitrary",)),
        name="router",
    )(x, norm_w.reshape(1, d), wr, br)


SCAT_TM = 256


def _scatter_kernel(zs_ref, ze_ref, nu_ref, s1_ref, s2_ref, hp_ref, xs_ref, zero_ref, sem):
    i = pl.program_id(0)
    tm = hp_ref.shape[0]
    n_all = xs_ref.shape[0] // MOE_BLK

    @pl.when(i == 0)
    def _():
        zero_ref[...] = jnp.zeros(zero_ref.shape, zero_ref.dtype)

        def zfill(e, c):
            zero_row = lambda r: pltpu.make_async_copy(zero_ref.at[pl.ds(0, 1), :], xs_ref.at[pl.ds(r, 1), :], sem)

            def one(r, c2):
                zero_row(r).start()
                return c2

            def one_wait(r, c2):
                zero_row(r).wait()
                return c2

            lax.fori_loop(zs_ref[e], ze_ref[e], one, 0)
            lax.fori_loop(zs_ref[e], ze_ref[e], one_wait, 0)
            return c

        def zero_block(b):
            r0 = pl.multiple_of(b * MOE_BLK, MOE_BLK)
            return pltpu.make_async_copy(zero_ref, xs_ref.at[pl.ds(r0, MOE_BLK), :], sem)

        def tfill(b, c):
            zero_block(b).start()
            return c

        def twait(b, c):
            zero_block(b).wait()
            return c

        lax.fori_loop(0, N_EXPERTS, zfill, 0)
        lax.fori_loop(nu_ref[0], n_all, tfill, 0)
        lax.fori_loop(nu_ref[0], n_all, twait, 0)

    def row_copies(t):
        src = hp_ref.at[pl.ds(t, 1), :]
        return (pltpu.make_async_copy(src, xs_ref.at[pl.ds(s1_ref[0, 0, t], 1), :], sem),
                pltpu.make_async_copy(src, xs_ref.at[pl.ds(s2_ref[0, 0, t], 1), :], sem))

    def issue(t, c):
        for cp in row_copies(t):
            cp.start()
        return c

    def drain(t, c):
        for cp in row_copies(t):
            cp.wait()
        return c

    lax.fori_loop(0, tm, issue, 0, unroll=8)
    lax.fori_loop(0, tm, drain, 0, unroll=8)


def _scatter_rows(zero_start, zero_end, n_used, slot1, slot2, hp, n_rows):
    t, w = hp.shape
    tm = SCAT_TM
    s3 = lambda a: a.reshape(t // tm, 1, tm)
    grid_spec = pltpu.PrefetchScalarGridSpec(
        num_scalar_prefetch=3,
        grid=(t // tm,),
        in_specs=[pl.BlockSpec((1, 1, tm), lambda i, zs, ze, nu: (i, 0, 0), memory_space=pltpu.SMEM),
                  pl.BlockSpec((1, 1, tm), lambda i, zs, ze, nu: (i, 0, 0), memory_space=pltpu.SMEM),
                  pl.BlockSpec((tm, w), lambda i, zs, ze, nu: (i, 0))],
        out_specs=pl.BlockSpec(memory_space=pl.ANY),
        scratch_shapes=[pltpu.VMEM((MOE_BLK, w), U32), pltpu.SemaphoreType.DMA(())],
    )
    return pl.pallas_call(
        _scatter_kernel,
        out_shape=jax.ShapeDtypeStruct((n_rows, w), U32),
        grid_spec=grid_spec,
        compiler_params=_cp(("arbitrary",)),
        name="scatter_rows",
    )(zero_start, zero_end, n_used, s3(slot1), s3(slot2), hp)


FF_TN = 256


def _unpack_rows(x_ref, xs_ref):
    u = x_ref[...]
    xs_ref[:, :HALF_D] = pltpu.bitcast(u << 16, F32).astype(BF16)
    xs_ref[:, HALF_D:] = pltpu.bitcast(u & jnp.uint32(0xFFFF0000), F32).astype(BF16)


def _ffn_up_kernel(be_ref, nu_ref, x_ref, wg_ref, wu_ref, o_ref, wgb_ref, wub_ref, xs_ref):
    i = pl.program_id(1)

    @pl.when(i < nu_ref[0])
    def _():
        @pl.when((i == 0) | (be_ref[i] != be_ref[jnp.maximum(i - 1, 0)]))
        def _():
            wgb_ref[...] = wg_ref[0, 0].astype(BF16)
            wub_ref[...] = wu_ref[0, 0].astype(BF16)

        _unpack_rows(x_ref, xs_ref)
        x = xs_ref[...]
        g = jnp.dot(x, wgb_ref[...], preferred_element_type=F32)
        u = jnp.dot(x, wub_ref[...], preferred_element_type=F32)
        o_ref[...] = (g * _sigmoid(g) * u).astype(o_ref.dtype)

    @pl.when(i >= nu_ref[0])
    def _():
        o_ref[...] = jnp.zeros(o_ref.shape, o_ref.dtype)


def _ffn_up(block_exp, n_used, xs, w_gate, w_up, layer, n_blocks):
    d, ff = w_gate.shape[2], w_gate.shape[3]
    blk = lambda i, nu: jnp.minimum(i, nu[0] - 1)
    grid_spec = pltpu.PrefetchScalarGridSpec(
        num_scalar_prefetch=2,
        grid=(ff // FF_TN, n_blocks),
        in_specs=[pl.BlockSpec((MOE_BLK, HALF_D), lambda j, i, be, nu: (blk(i, nu), 0)),
                  pl.BlockSpec((1, 1, d, FF_TN), lambda j, i, be, nu: (layer, be[blk(i, nu)], 0, j)),
                  pl.BlockSpec((1, 1, d, FF_TN), lambda j, i, be, nu: (layer, be[blk(i, nu)], 0, j))],
        out_specs=pl.BlockSpec((MOE_BLK, FF_TN), lambda j, i, be, nu: (i, j)),
        scratch_shapes=[pltpu.VMEM((d, FF_TN), BF16), pltpu.VMEM((d, FF_TN), BF16),
                        pltpu.VMEM((MOE_BLK, d), BF16)],
    )
    return pl.pallas_call(
        _ffn_up_kernel,
        out_shape=jax.ShapeDtypeStruct((n_blocks * MOE_BLK, ff), BF16),
        grid_spec=grid_spec,
        compiler_params=_cp(("arbitrary", "arbitrary")),
        name="ffn_up",
    )(block_exp, n_used, xs, w_gate, w_up)


def _ffn_down_kernel(be_ref, nu_ref, h_ref, wd_ref, y_ref, wdb_ref):
    i = pl.program_id(0)

    @pl.when(i < nu_ref[0])
    def _():
        @pl.when((i == 0) | (be_ref[i] != be_ref[jnp.maximum(i - 1, 0)]))
        def _():
            wdb_ref[...] = wd_ref[0, 0].astype(BF16)

        y_ref[...] = jnp.dot(h_ref[...], wdb_ref[...], preferred_element_type=F32)

    @pl.when(i >= nu_ref[0])
    def _():
        y_ref[...] = jnp.zeros(y_ref.shape, y_ref.dtype)


def _ffn_down(block_exp, n_used, hmid, w_down, layer, n_blocks):
    ff, d = w_down.shape[2], w_down.shape[3]
    blk = lambda i, nu: jnp.minimum(i, nu[0] - 1)
    grid_spec = pltpu.PrefetchScalarGridSpec(
        num_scalar_prefetch=2,
        grid=(n_blocks,),
        in_specs=[pl.BlockSpec((MOE_BLK, ff), lambda i, be, nu: (blk(i, nu), 0)),
                  pl.BlockSpec((1, 1, ff, d), lambda i, be, nu: (layer, be[blk(i, nu)], 0, 0))],
        out_specs=pl.BlockSpec((MOE_BLK, d), lambda i, be, nu: (i, 0)),
        scratch_shapes=[pltpu.VMEM((ff, d), BF16)],
    )
    return pl.pallas_call(
        _ffn_down_kernel,
        out_shape=jax.ShapeDtypeStruct((n_blocks * MOE_BLK, d), F32),
        grid_spec=grid_spec,
        compiler_params=_cp(("arbitrary",)),
        name="ffn_down",
    )(block_exp, n_used, hmid, w_down)


COMB_TM = 256


def _combine_kernel(s1_ref, s2_ref, x_ref, wc_ref, fw_ref, y_ref, o_ref, ya_ref, yb_ref, sem, *, final):
    tm = x_ref.shape[0]

    def row_copies(t):
        return (pltpu.make_async_copy(y_ref.at[pl.ds(s1_ref[0, 0, t], 1), :], ya_ref.at[pl.ds(t, 1), :], sem),
                pltpu.make_async_copy(y_ref.at[pl.ds(s2_ref[0, 0, t], 1), :], yb_ref.at[pl.ds(t, 1), :], sem))

    def issue(t, c):
        for cp in row_copies(t):
            cp.start()
        return c

    def drain(t, c):
        for cp in row_copies(t):
            cp.wait()
        return c

    lax.fori_loop(0, tm, issue, 0, unroll=8)
    lax.fori_loop(0, tm, drain, 0, unroll=8)
    o = x_ref[...] + wc_ref[:, 0:1] * ya_ref[...] + wc_ref[:, 1:2] * yb_ref[...]
    if final:
        o = o * lax.rsqrt(jnp.mean(o * o, axis=-1, keepdims=True) + EPS) * fw_ref[...]
    o_ref[...] = o


def _combine(slot1, slot2, x, wcol, y, final_w, final):
    t, d = x.shape
    tm = COMB_TM
    s3 = lambda a: a.reshape(t // tm, 1, tm)
    return pl.pallas_call(
        functools.partial(_combine_kernel, final=final),
        out_shape=jax.ShapeDtypeStruct((t, d), F32),
        grid=(t // tm,),
        in_specs=[pl.BlockSpec((1, 1, tm), lambda i: (i, 0, 0), memory_space=pltpu.SMEM),
                  pl.BlockSpec((1, 1, tm), lambda i: (i, 0, 0), memory_space=pltpu.SMEM),
                  pl.BlockSpec((tm, d), lambda i: (i, 0)),
                  pl.BlockSpec((tm, LANES), lambda i: (i, 0)),
                  pl.BlockSpec((1, d), lambda i: (0, 0)),
                  pl.BlockSpec(memory_space=pl.ANY)],
        out_specs=pl.BlockSpec((tm, d), lambda i: (i, 0)),
        scratch_shapes=[pltpu.VMEM((tm, d), F32), pltpu.VMEM((tm, d), F32),
                        pltpu.SemaphoreType.DMA(())],
        compiler_params=_cp(("arbitrary",)),
        name="combine",
    )(s3(slot1), s3(slot2), x, wcol, final_w.reshape(1, d), y)


def _mixer_layer(xf, batch, seq, norm_w, w_in, dn_conv_w, dn_a_log, dn_dt_bias, dn_norm_w, ml_igate_b,
                 ml_fgate_b, ml_norm_w, sc_conv_w, w_out):
    c0 = 3 * DN_WIDTH + DN_WIDTH
    c1 = c0 + 4 * DN_HEADS
    c2 = c1 + 2 * ML_HEADS * ML_QK + 2 * ML_WIDTH
    c3 = c2 + 4 * ML_HEADS
    w_main = jnp.concatenate([w_in[:, :c0], w_in[:, c1:c2], w_in[:, c3:]], axis=1).astype(BF16)
    w_gate = _pack_gate_cols(w_in[:, c0:c0 + 2 * DN_HEADS], w_in[:, c0 + 2 * DN_HEADS:c1],
                             w_in[:, c2:c2 + 2 * ML_HEADS], w_in[:, c2 + 2 * ML_HEADS:c3]).astype(BF16)
    h = _rmsnorm(xf, norm_w, BF16)
    proj, graw = _inproj(h, w_main, w_gate)
    gc, gr = _gate_prep(graw, _gate_params(dn_a_log, dn_dt_bias, ml_igate_b, ml_fgate_b))
    dn = _deltanet(proj, dn_conv_w, gc, gr, dn_norm_w, batch, seq)
    ml = _mlstm(proj, gc, gr, ml_norm_w, batch, seq)
    sc = _shortconv(proj, sc_conv_w, batch, seq)
    return _outproj(dn, ml, sc, w_out.astype(BF16), xf)


def _moe_layer(xf, norm_w, rg_w, rg_b, re_w, re_b, w_gate, w_up, w_down, layer, final_w, final):
    t, d = xf.shape
    def lanes(g, e):
        r = g.shape[0]
        return jnp.concatenate([g, jnp.zeros((r, EXP_ROW0 - N_GROUPS), F32), e,
                                jnp.zeros((r, LANES - EXP_ROW0 - N_EXPERTS), F32)], axis=1)

    wr = lanes(rg_w, re_w)
    br = lanes(rg_b.reshape(1, -1), re_b.reshape(1, -1))
    hp, meta, wcol, cnt = _router(xf, norm_w, wr, br)
    counts = cnt[:, 0].astype(I32)
    padded = (counts + MOE_BLK - 1) // MOE_BLK * MOE_BLK
    pad_end = jnp.cumsum(padded)
    pad_start = pad_end - padded
    n_blocks = -(-(2 * t + N_EXPERTS * (MOE_BLK - 1)) // MOE_BLK)
    n_used = (pad_end[-1:] // MOE_BLK).astype(I32)
    blk_row = jnp.arange(n_blocks, dtype=I32)[:, None] * MOE_BLK
    block_exp = jnp.minimum(jnp.sum((pad_end[None, :] <= blk_row).astype(I32), axis=1), N_EXPERTS - 1)
    slot1 = pad_start[meta[0]] + meta[2]
    slot2 = pad_start[meta[1]] + meta[3]
    xs = _scatter_rows(pad_start + counts, pad_end, n_used, slot1, slot2, hp, n_blocks * MOE_BLK)
    hmid = _ffn_up(block_exp, n_used, xs, w_gate, w_up, layer, n_blocks)
    y = _ffn_down(block_exp, n_used, hmid, w_down, layer, n_blocks)
    return _combine(slot1, slot2, xf, wcol, y, final_w, final)


def kernel(x, norm_mix_w, w_in, dn_conv_w, dn_a_log, dn_dt_bias, dn_norm_w, ml_igate_b, ml_fgate_b, ml_norm_w,
           sc_conv_w, w_out, norm_ffn_w, router_group_w, router_group_b, router_expert_w, router_expert_b,
           expert_w_gate, expert_w_up, expert_w_down, final_norm_w):
    batch, seq, d = x.shape
    depth = w_in.shape[0]
    xf = x.reshape(batch * seq, d)
    for l in range(depth):
        xf = _mixer_layer(xf, batch, seq, norm_mix_w[l], w_in[l], dn_conv_w[l], dn_a_log[l], dn_dt_bias[l],
                          dn_norm_w[l], ml_igate_b[l], ml_fgate_b[l], ml_norm_w[l], sc_conv_w[l], w_out[l])
        xf = _moe_layer(xf, norm_ffn_w[l], router_group_w[l], router_group_b[l], router_expert_w[l],
                        router_expert_b[l], expert_w_gate, expert_w_up, expert_w_down, l,
                        final_norm_w, l == depth - 1)
    return xf.reshape(batch, seq, d)
```

```python
import functools

import numpy as np
import jax
import jax.numpy as jnp
from jax import lax
from jax.experimental import pallas as pl
from jax.experimental.pallas import tpu as pltpu

F32 = jnp.float32
BF16 = jnp.bfloat16
I32 = jnp.int32
U32 = jnp.uint32

D_MODEL = 4096
DN_HEADS = 12
DN_DIM = 128
DN_WIDTH = DN_HEADS * DN_DIM
DN_CONV = 5
ML_HEADS = 6
ML_QK = 128
ML_V = 256
ML_WIDTH = ML_HEADS * ML_V
SC_WIDTH = 1024
SC_CONV = 3
N_GROUPS = 4
EXPERTS_PER_GROUP = 8
N_EXPERTS = N_GROUPS * EXPERTS_PER_GROUP
D_FF = 768
EPS = 1e-6

LANES = 128
SUBLANES = 8
VMEM_LIMIT = 52 * 1024 * 1024

CH = 64
TL = 256
CPT = TL // CH
DN_TPI = 1
DN_HPS = 2
N_HEADS_ALL = DN_HEADS + ML_HEADS
MOE_BLK = 256
NEG = -1e30

OFF_DN_Q, OFF_DN_K, OFF_DN_V, OFF_DN_Z = 0, 1536, 3072, 4608
OFF_ML_Q, OFF_ML_K, OFF_ML_V, OFF_ML_O = 6144, 6912, 7680, 9216
OFF_SC_B, OFF_SC_C, OFF_SC_X = 10752, 11776, 12800
N_MAIN = 13824


def _cp(sem, vmem=VMEM_LIMIT):
    return pltpu.CompilerParams(dimension_semantics=sem, vmem_limit_bytes=vmem)


def _mm(a, b):
    return jnp.dot(a.astype(BF16), b.astype(BF16), preferred_element_type=F32)


def _mm_nt(a, b):
    return lax.dot_general(a.astype(BF16), b.astype(BF16), (((1,), (1,)), ((), ())),
                           preferred_element_type=F32)


def _mm_f32(a, b):
    return jnp.dot(a, b, preferred_element_type=F32, precision=lax.Precision.HIGHEST)


def _sigmoid(x):
    return 1.0 / (1.0 + jnp.exp(-x))


def _softplus(x):
    return jnp.maximum(x, 0.0) + jnp.log(1.0 + jnp.exp(-jnp.abs(x)))


def _rms_kernel(x_ref, w_ref, o_ref):
    x = x_ref[...]
    ms = jnp.mean(x * x, axis=-1, keepdims=True)
    o_ref[...] = (x * lax.rsqrt(ms + EPS) * w_ref[...]).astype(o_ref.dtype)


def _rmsnorm(x2d, w, out_dtype, tm=256):
    t, d = x2d.shape
    return pl.pallas_call(
        _rms_kernel,
        out_shape=jax.ShapeDtypeStruct((t, d), out_dtype),
        grid=(t // tm,),
        in_specs=[pl.BlockSpec((tm, d), lambda i: (i, 0)), pl.BlockSpec((1, d), lambda i: (0, 0))],
        out_specs=pl.BlockSpec((tm, d), lambda i: (i, 0)),
        compiler_params=_cp(("arbitrary",)),
        name="rmsnorm",
    )(x2d, w.reshape(1, d))


def _inproj_kernel(a_ref, w_ref, wg_ref, o_ref, g_ref):
    a = a_ref[...]
    o_ref[...] = jnp.dot(a, w_ref[...], preferred_element_type=F32).astype(o_ref.dtype)

    @pl.when(pl.program_id(1) == 0)
    def _():
        g_ref[...] = jnp.dot(a, wg_ref[...], preferred_element_type=F32)


def _inproj(h, w_main, w_gate, tm=1024, tn=512):
    t, d = h.shape
    n = w_main.shape[1]
    ng = w_gate.shape[1]
    return pl.pallas_call(
        _inproj_kernel,
        out_shape=(jax.ShapeDtypeStruct((t, n), BF16), jax.ShapeDtypeStruct((t, ng), F32)),
        grid=(t // tm, n // tn),
        in_specs=[pl.BlockSpec((tm, d), lambda i, j: (i, 0)),
                  pl.BlockSpec((d, tn), lambda i, j: (0, j)),
                  pl.BlockSpec((d, ng), lambda i, j: (0, 0))],
        out_specs=(pl.BlockSpec((tm, tn), lambda i, j: (i, j)),
                   pl.BlockSpec((tm, ng), lambda i, j: (i, 0))),
        compiler_params=_cp(("arbitrary", "arbitrary")),
        name="inproj",
    )(h, w_main, w_gate)


def _gate_kernel(raw_ref, p_ref, gc_ref, gr_ref):
    x = raw_ref[...]
    lane = lax.broadcasted_iota(I32, x.shape, 1)
    chn = lane & 7
    is_ml = lane >= LANES
    a_log, dt_b, ig_b, fg_b = p_ref[0:1, :], p_ref[1:2, :], p_ref[2:3, :], p_ref[3:4, :]
    beta = _sigmoid(x)
    g = -jnp.exp(a_log) * _softplus(x + dt_b)
    ig = x + ig_b
    lf = -_softplus(-(x + fg_b))
    lo = chn < 2
    val = jnp.where(is_ml, jnp.where(lo, ig, lf), jnp.where(lo, beta, g))
    val = jnp.where(chn < 4, val, 0.0)
    ri = lax.broadcasted_iota(I32, (TL, TL), 0)
    ci = lax.broadcasted_iota(I32, (TL, TL), 1)
    same = (ri >> 6) == (ci >> 6)
    m_f = jnp.where(same & (ri >= ci), 1.0, 0.0)
    m_b = jnp.where(same & (ri <= ci), 1.0, 0.0)
    m_t = jnp.where(same, 1.0, 0.0)
    cum_f = _mm_f32(m_f, val)
    cum_b = _mm_f32(m_b, val)
    tot = pltpu.roll(_mm_f32(m_t, val), 2, axis=1)
    out = jnp.where(chn == 2, cum_f, jnp.where(chn == 3, cum_b, jnp.where((chn == 4) | (chn == 5), tot, val)))
    gr_ref[0] = out.T
    for hh in range(N_HEADS_ALL):
        grp, h = (0, hh) if hh < DN_HEADS else (1, hh - DN_HEADS)
        v = out[:, grp * LANES:(grp + 1) * LANES]
        sh = (LANES - 8 * h) % LANES
        gc_ref[hh] = pltpu.roll(v, sh, axis=1) if sh else v


def _gate_prep(raw, params):
    t = raw.shape[0]
    nt = t // TL
    return pl.pallas_call(
        _gate_kernel,
        out_shape=(jax.ShapeDtypeStruct((N_HEADS_ALL, t, LANES), F32),
                   jax.ShapeDtypeStruct((nt, 2 * LANES, TL), F32)),
        grid=(nt,),
        in_specs=[pl.BlockSpec((TL, 2 * LANES), lambda i: (i, 0)),
                  pl.BlockSpec((8, 2 * LANES), lambda i: (0, 0))],
        out_specs=(pl.BlockSpec((N_HEADS_ALL, TL, LANES), lambda i: (0, i, 0)),
                   pl.BlockSpec((1, 2 * LANES, TL), lambda i: (i, 0, 0))),
        compiler_params=_cp(("arbitrary",)),
        name="gate_prep",
    )(raw, params)


def _conv_centred(x_ref, w_ref, pad_ref, width):
    s = x_ref.shape[0]
    half = width // 2
    pad_ref[0:8, :] = jnp.zeros((8, pad_ref.shape[1]), F32)
    pad_ref[s + 8:s + 16, :] = jnp.zeros((8, pad_ref.shape[1]), F32)
    pad_ref[8:s + 8, :] = x_ref[...].astype(F32)
    acc = None
    for i in range(width):
        term = w_ref[i:i + 1, :] * pad_ref[8 - half + i:8 - half + i + s, :]
        acc = term if acc is None else acc + term
    return acc


def _conv_centred_tiles(x_ref, w_ref, pad_ref, width, cols):
    s = x_ref.shape[0]
    half = width // 2
    pad_ref[0:8, :] = jnp.zeros((8, pad_ref.shape[1]), F32)
    pad_ref[s + 8:s + 16, :] = jnp.zeros((8, pad_ref.shape[1]), F32)
    pad_ref[8:s + 8, :] = x_ref[:, cols].astype(F32)
    for rt in range(s // TL):
        base = 8 - half + rt * TL
        acc = None
        for i in range(width):
            term = w_ref[i:i + 1, cols] * pad_ref[base + i:base + i + TL, :]
            acc = term if acc is None else acc + term
        yield slice(rt * TL, (rt + 1) * TL), acc


def _tile_masks():
    ri = lax.broadcasted_iota(I32, (TL, TL), 0)
    ci = lax.broadcasted_iota(I32, (TL, TL), 1)
    return ri, ci


def _unit_tri_inverse(mats, same16, same32, eye):
    each = lambda f, *ls: [f(*xs) for xs in zip(*ls)]
    a16 = each(lambda a: jnp.where(same16, a, 0.0).astype(BF16), mats)
    b1 = each(lambda x: _mm(x, x).astype(BF16), a16)
    p = each(lambda x: eye - x.astype(F32), a16)
    b2 = each(lambda x: _mm(x, x).astype(BF16), b1)
    p = each(lambda x, b: x + _mm(x, b), p, b1)
    b3 = each(lambda x: _mm(x, x).astype(BF16), b2)
    p = each(lambda x, b: x + _mm(x, b), p, b2)
    p = each(lambda x, b: x + _mm(x, b), p, b3)
    off32 = same32 & jnp.logical_not(same16)
    a32 = each(lambda a: jnp.where(off32, a, 0.0).astype(BF16), mats)
    pb = each(lambda x: x.astype(BF16), p)
    t = each(_mm, pb, a32)
    p = each(lambda x, y, z: x - _mm(y, z), p, t, pb)
    a64 = each(lambda a: jnp.where(same32, 0.0, a).astype(BF16), mats)
    pb = each(lambda x: x.astype(BF16), p)
    t = each(_mm, pb, a64)
    p = each(lambda x, y, z: x - _mm(y, z), p, t, pb)
    return p


def _dn_prep_kernel(x_ref, w_ref, o_ref, pad_ref):
    j = pl.program_id(1)
    n_qk = 2 * DN_WIDTH // x_ref.shape[1]
    qscale = jnp.where(j < n_qk // 2, DN_DIM ** -0.5, 1.0)
    for h in range(x_ref.shape[1] // DN_DIM):
        sl = slice(h * DN_DIM, (h + 1) * DN_DIM)
        for r, y in _conv_centred_tiles(x_ref, w_ref, pad_ref, DN_CONV, sl):
            y = y * _sigmoid(y)
            yn = y * (lax.rsqrt(jnp.sum(y * y, axis=-1, keepdims=True) + EPS) * qscale)
            o_ref[r, sl] = jnp.where(j < n_qk, yn, y).astype(o_ref.dtype)


def _dn_prep(proj, conv_w, batch, seq, wc=256):
    t = proj.shape[0]
    n = 3 * DN_WIDTH
    return pl.pallas_call(
        _dn_prep_kernel,
        out_shape=jax.ShapeDtypeStruct((t, n), BF16),
        grid=(batch, n // wc),
        in_specs=[pl.BlockSpec((seq, wc), lambda b, j: (b, j)), pl.BlockSpec((DN_CONV, wc), lambda b, j: (0, j))],
        out_specs=pl.BlockSpec((seq, wc), lambda b, j: (b, j)),
        scratch_shapes=[pltpu.VMEM((seq + 16, DN_DIM), F32)],
        compiler_params=_cp(("arbitrary", "arbitrary")),
        name="dn_prep",
    )(proj, conv_w)


def _dn_kernel(q_ref, k_ref, v_ref, z_ref, gc_ref, gr_ref, nw_ref, o_ref,
               u_ref, wq_ref, ak_ref, eg_ref, oo_ref):
    s = q_ref.shape[0]
    nt = s // TL
    nc = s // CH
    heads = range(DN_HPS)
    hl = [slice(h * DN_DIM, (h + 1) * DN_DIM) for h in heads]

    def tile_body(j, carry):
        ri, ci = _tile_masks()
        same64 = (ri >> 6) == (ci >> 6)
        same32 = (ri >> 5) == (ci >> 5)
        same16 = (ri >> 4) == (ci >> 4)
        eye = jnp.where(ri == ci, 1.0, 0.0)
        incl = [same64 & (ri >= ci), same64 & (ri <= ci)]
        strict = [same64 & (ri > ci), same64 & (ri < ci)]
        tiles = [j * DN_TPI + t for t in range(DN_TPI)]
        r0 = [pl.multiple_of(i * TL, TL) for i in tiles]
        ht = [(h, t) for h in heads for t in range(DN_TPI)]
        qn = {(h, t): q_ref[pl.ds(r0[t], TL), hl[h]].astype(F32) for h, t in ht}
        kn = {(h, t): k_ref[pl.ds(r0[t], TL), hl[h]].astype(F32) for h, t in ht}
        vv = {(h, t): v_ref[pl.ds(r0[t], TL), hl[h]].astype(F32) for h, t in ht}
        kk = {x: _mm_nt(kn[x], kn[x]) for x in ht}
        qk = {x: _mm_nt(qn[x], kn[x]) for x in ht}
        gc = {(h, t): gc_ref[h, pl.ds(r0[t], TL), :] for h, t in ht}
        gr = {(h, t): gr_ref[tiles[t]][8 * h:8 * h + 8, :] for h, t in ht}
        ch = [(h, t, d) for h, t in ht for d in (0, 1)]
        beta_c = [gc[h, t][:, d:d + 1] for h, t, d in ch]
        g_c = [gc[h, t][:, 2 + d:3 + d] for h, t, d in ch]
        gt_c = [gc[h, t][:, 4 + d:5 + d] for h, t, d in ch]
        g_r = [gr[h, t][2 + d:3 + d, :] for h, t, d in ch]
        dec = [jnp.exp(jnp.where(incl[d], g_c[n] - g_r[n], NEG)) for n, (h, t, d) in enumerate(ch)]
        a = [jnp.where(strict[d], beta_c[n] * kk[h, t] * dec[n], 0.0) for n, (h, t, d) in enumerate(ch)]
        tinv = _unit_tri_inverse(a, same16, same32, eye)
        eg_c = [jnp.exp(g) for g in g_c]
        rhs = [jnp.concatenate([beta_c[n] * vv[h, t], (beta_c[n] * eg_c[n]) * kn[h, t]], axis=1)
               for n, (h, t, d) in enumerate(ch)]
        uw = [_mm(x, y) for x, y in zip(tinv, rhs)]
        for n, (h, t, d) in enumerate(ch):
            i = tiles[t]
            u_ref[h, d, pl.ds(r0[t], TL), :] = uw[n][:, :DN_DIM]
            w = uw[n][:, DN_DIM:]
            qe = qn[h, t] * eg_c[n]
            attn = qk[h, t] * dec[n]
            kdt = (kn[h, t] * jnp.exp(gt_c[n] - g_c[n])).T
            egt = jnp.exp(gt_c[n])
            for c in range(CPT):
                rows = slice(c * CH, (c + 1) * CH)
                wq_ref[h, d, i * CPT + c] = jnp.concatenate([w[rows], qe[rows]], axis=0).astype(BF16)
                ak_ref[h, d, i * CPT + c] = jnp.concatenate([attn[rows, rows], kdt[:, rows]], axis=0).astype(BF16)
                eg_ref[h, d, i * CPT + c] = jnp.broadcast_to(egt[c * CH:c * CH + 8, :], (8, LANES))
        return carry

    lax.fori_loop(0, nt // DN_TPI, tile_body, 0)

    hd = [(h, d) for h in heads for d in (0, 1)]

    def chunk_body(c, carry):
        states = list(carry)
        ws, vnew, m2 = {}, {}, {}
        cc = [c, nc - 1 - c]
        r = [pl.multiple_of(x * CH, CH) for x in cc]
        for n, (h, d) in enumerate(hd):
            ws[n] = jnp.dot(wq_ref[h, d, cc[d]], states[n].astype(BF16), preferred_element_type=F32)
        for n, (h, d) in enumerate(hd):
            vnew[n] = u_ref[h, d, pl.ds(r[d], CH), :] - ws[n][:CH]
        for n, (h, d) in enumerate(hd):
            m2[n] = jnp.dot(ak_ref[h, d, cc[d]], vnew[n].astype(BF16), preferred_element_type=F32)
        for n, (h, d) in enumerate(hd):
            oo_ref[h, d, pl.ds(r[d], CH), :] = ws[n][CH:] + m2[n][:CH]
            states[n] = eg_ref[h, d, cc[d]][0:1, :] * states[n] + m2[n][CH:]
        return tuple(states)

    z0 = jnp.zeros((DN_DIM, DN_DIM), F32)
    lax.fori_loop(0, nc, chunk_body, tuple(z0 for _ in hd))

    for h in heads:
        for rt in range(s // TL):
            r = slice(rt * TL, (rt + 1) * TL)
            o = oo_ref[h, 0, r, :] + oo_ref[h, 1, r, :]
            o = o * lax.rsqrt(jnp.mean(o * o, axis=-1, keepdims=True) + EPS) * nw_ref[...]
            z = z_ref[r, hl[h]].astype(F32)
            o_ref[r, hl[h]] = (o * (z * _sigmoid(z))).astype(o_ref.dtype)


def _deltanet(proj, conv_w, gc, gr, norm_w, batch, seq):
    t = proj.shape[0]
    nc = seq // CH
    nt = seq // TL
    wb = DN_HPS * DN_DIM
    qkv = _dn_prep(proj, conv_w, batch, seq)
    qb, kb, vb, zb = OFF_DN_Q // wb, OFF_DN_K // wb, OFF_DN_V // wb, OFF_DN_Z // wb
    col = lambda off: pl.BlockSpec((seq, wb), lambda b, h, off=off: (b, off + h))
    return pl.pallas_call(
        _dn_kernel,
        out_shape=jax.ShapeDtypeStruct((t, DN_WIDTH), BF16),
        grid=(batch, DN_HEADS // DN_HPS),
        in_specs=[col(qb), col(kb), col(vb), col(zb),
                  pl.BlockSpec((DN_HPS, seq, LANES), lambda b, h: (h, b, 0)),
                  pl.BlockSpec((nt, 8 * DN_HPS, TL), lambda b, h: (b, h, 0)),
                  pl.BlockSpec((1, DN_DIM), lambda b, h: (0, 0))],
        out_specs=pl.BlockSpec((seq, wb), lambda b, h: (b, h)),
        scratch_shapes=[pltpu.VMEM((DN_HPS, 2, seq, DN_DIM), F32),
                        pltpu.VMEM((DN_HPS, 2, nc, 2 * CH, DN_DIM), BF16),
                        pltpu.VMEM((DN_HPS, 2, nc, CH + DN_DIM, CH), BF16),
                        pltpu.VMEM((DN_HPS, 2, nc, 8, LANES), F32),
                        pltpu.VMEM((DN_HPS, 2, seq, DN_DIM), F32)],
        compiler_params=_cp(("arbitrary", "arbitrary")),
        name="deltanet",
    )(qkv, qkv, qkv, proj, gc, gr, norm_w.reshape(1, DN_DIM))


def _ml_kernel(q_ref, k_ref, v_ref, og_ref, gc_ref, gr_ref, nw_ref, o_ref,
               ak_ref, rs_ref, cs_ref, dn_ref, oo_ref):
    s = q_ref.shape[0]
    nt = s // TL
    nc = s // CH
    scale = ML_QK ** -0.5

    def tile_body(it, carry):
        m_carry = list(carry)
        ri, ci = _tile_masks()
        same64 = (ri >> 6) == (ci >> 6)
        rowc = lax.broadcasted_iota(I32, (TL, 1), 0) >> 6
        lane = lax.broadcasted_iota(I32, (TL, LANES), 1)
        dirs = (0, 1)
        tile = [it, nt - 1 - it]
        r0 = [pl.multiple_of(i * TL, TL) for i in tile]
        qs = [q_ref[pl.ds(r, TL), :].astype(F32) * scale for r in r0]
        kf = [k_ref[pl.ds(r, TL), :].astype(F32) for r in r0]
        qk = [_mm_nt(a, b) for a, b in zip(qs, kf)]
        gc = [gc_ref[0, pl.ds(r, TL), :] for r in r0]
        gr = [gr_ref[i] for i in tile]
        ig_c = [gc[d][:, d:d + 1] for d in dirs]
        b_c = [gc[d][:, 2 + d:3 + d] for d in dirs]
        bt_c = [gc[d][:, 4 + d:5 + d] for d in dirs]
        ig_r = [gr[d][d:d + 1, :] for d in dirs]
        b_r = [gr[d][2 + d:3 + d, :] for d in dirs]
        bt_r = [gr[d][4 + d:5 + d, :] for d in dirs]
        incl = [same64 & (ri >= ci), same64 & (ri <= ci)]
        dmat = [jnp.where(incl[d], b_c[d] - b_r[d] + ig_r[d], NEG) for d in dirs]
        m_intra = [jnp.max(x, axis=1, keepdims=True) for x in dmat]
        d_end_c = [bt_c[d] - b_c[d] + ig_c[d] for d in dirs]
        d_end_r = [bt_r[d] - b_r[d] + ig_r[d] for d in dirs]
        dmax_c = [jnp.max(jnp.where(same64, x, NEG), axis=1, keepdims=True) for x in d_end_r]
        m_in_c, m_out_c = [], []
        for d in dirs:
            m_in = jnp.zeros((TL, 1), F32)
            m_out = jnp.zeros((TL, 1), F32)
            m = m_carry[d]
            for cc in (range(CPT) if d == 0 else range(CPT - 1, -1, -1)):
                btot = bt_c[d][cc * CH:cc * CH + 1, :]
                dmx = dmax_c[d][cc * CH:cc * CH + 1, :]
                m_new = jnp.maximum(btot + m, dmx)
                m_in = jnp.where(rowc == cc, m, m_in)
                m_out = jnp.where(rowc == cc, m_new, m_out)
                m = m_new
            m_carry[d] = m
            m_in_c.append(m_in)
            m_out_c.append(m_out)
        m_inter = [b_c[d] + m_in_c[d] for d in dirs]
        m_pos = [jnp.maximum(m_intra[d], m_inter[d]) for d in dirs]
        scores = [qk[d] * jnp.exp(dmat[d] - m_pos[d]) for d in dirs]
        inter_scale = [jnp.exp(m_inter[d] - m_pos[d]) for d in dirs]
        rsum = [jnp.sum(x, axis=1, keepdims=True) for x in scores]
        einv = [jnp.exp(-x) for x in m_pos]
        cscale = [jnp.exp(bt_c[d] + m_in_c[d] - m_out_c[d]) for d in dirs]
        wk = [kf[d] * jnp.exp(d_end_c[d] - m_out_c[d]) for d in dirs]
        wkt = [x.T for x in wk]
        for d in dirs:
            i = tile[d]
            rs_ref[d, pl.ds(r0[d], TL), :] = jnp.where(lane == 0, inter_scale[d],
                                                      jnp.where(lane == 1, rsum[d], einv[d]))
            for c in range(CPT):
                rows = slice(c * CH, (c + 1) * CH)
                ak_ref[d, i * CPT + c] = jnp.concatenate([scores[d][rows, rows], wkt[d][:, rows]],
                                                         axis=0).astype(BF16)
                cs_ref[d, i * CPT + c] = jnp.broadcast_to(cscale[d][c * CH:c * CH + 8, :], (8, LANES))
                dn_ref[d, i * CPT + c] = jnp.broadcast_to(jnp.sum(wk[d][rows], axis=0, keepdims=True),
                                                          (8, ML_QK))
        return tuple(m_carry)

    zero11 = jnp.zeros((1, 1), F32)
    lax.fori_loop(0, nt, tile_body, (zero11, zero11))

    def chunk_body(c, carry):
        cst = [carry[0], carry[1]]
        nst = [carry[2], carry[3]]
        for d in range(2):
            cc = c if d == 0 else nc - 1 - c
            r = pl.multiple_of(cc * CH, CH)
            qb = q_ref[pl.ds(r, CH), :]
            inter = jnp.dot(qb, cst[d].astype(BF16), preferred_element_type=F32) * scale
            m2 = jnp.dot(ak_ref[d, cc], v_ref[pl.ds(r, CH), :], preferred_element_type=F32)
            qn = jnp.sum(qb.astype(F32) * nst[d], axis=1, keepdims=True) * scale
            rs = rs_ref[d, pl.ds(r, CH), :]
            isc = rs[:, 0:1]
            num = m2[:CH] + isc * inter
            den = rs[:, 1:2] + isc * qn
            oo_ref[d, pl.ds(r, CH), :] = num / jnp.maximum(jnp.abs(den), rs[:, 2:3])
            csc = cs_ref[d, cc][0:1, 0:1]
            cst[d] = csc * cst[d] + m2[CH:]
            nst[d] = csc * nst[d] + dn_ref[d, cc][0:1, :]
        return (cst[0], cst[1], nst[0], nst[1])

    c0 = jnp.zeros((ML_QK, ML_V), F32)
    n0 = jnp.zeros((1, ML_QK), F32)
    lax.fori_loop(0, nc, chunk_body, (c0, c0, n0, n0))

    o = oo_ref[0] + oo_ref[1]
    o = o * lax.rsqrt(jnp.mean(o * o, axis=-1, keepdims=True) + EPS) * nw_ref[...]
    o_ref[...] = (o * _sigmoid(og_ref[...].astype(F32))).astype(o_ref.dtype)


def _mlstm(proj, gc, gr, norm_w, batch, seq):
    t = proj.shape[0]
    nc = seq // CH
    nt = seq // TL
    qb, kb, vb, ob = OFF_ML_Q // ML_QK, OFF_ML_K // ML_QK, OFF_ML_V // ML_V, OFF_ML_O // ML_V
    return pl.pallas_call(
        _ml_kernel,
        out_shape=jax.ShapeDtypeStruct((t, ML_WIDTH), BF16),
        grid=(batch, ML_HEADS),
        in_specs=[pl.BlockSpec((seq, ML_QK), lambda b, h: (b, qb + h)),
                  pl.BlockSpec((seq, ML_QK), lambda b, h: (b, kb + h)),
                  pl.BlockSpec((seq, ML_V), lambda b, h: (b, vb + h)),
                  pl.BlockSpec((seq, ML_V), lambda b, h: (b, ob + h)),
                  pl.BlockSpec((1, seq, LANES), lambda b, h: (DN_HEADS + h, b, 0)),
                  pl.BlockSpec((nt, 8, TL), lambda b, h: (b, LANES // 8 + h, 0)),
                  pl.BlockSpec((1, ML_V), lambda b, h: (0, h))],
        out_specs=pl.BlockSpec((seq, ML_V), lambda b, h: (b, h)),
        scratch_shapes=[pltpu.VMEM((2, nc, CH + ML_QK, CH), BF16),
                        pltpu.VMEM((2, seq, LANES), F32),
                        pltpu.VMEM((2, nc, 8, LANES), F32),
                        pltpu.VMEM((2, nc, 8, ML_QK), F32),
                        pltpu.VMEM((2, seq, ML_V), F32)],
        compiler_params=_cp(("arbitrary", "arbitrary")),
        name="mlstm",
    )(proj, proj, proj, proj, gc, gr, norm_w.reshape(1, ML_WIDTH))


def _sc_kernel(b_ref, c_ref, x_ref, w_ref, o_ref, cx_ref, pad_ref):
    cx_ref[...] = c_ref[...].astype(F32) * x_ref[...].astype(F32)
    y = _conv_centred(cx_ref, w_ref, pad_ref, SC_CONV)
    o_ref[...] = (b_ref[...].astype(F32) * y).astype(o_ref.dtype)


def _shortconv(proj, conv_w, batch, seq, wc=256):
    t = proj.shape[0]
    bb, cb, xb = OFF_SC_B // wc, OFF_SC_C // wc, OFF_SC_X // wc
    col = lambda off: pl.BlockSpec((seq, wc), lambda b, j, off=off: (b, off + j))
    return pl.pallas_call(
        _sc_kernel,
        out_shape=jax.ShapeDtypeStruct((t, SC_WIDTH), BF16),
        grid=(batch, SC_WIDTH // wc),
        in_specs=[col(bb), col(cb), col(xb), pl.BlockSpec((SC_CONV, wc), lambda b, j: (0, j))],
        out_specs=pl.BlockSpec((seq, wc), lambda b, j: (b, j)),
        scratch_shapes=[pltpu.VMEM((seq, wc), F32), pltpu.VMEM((seq + 16, wc), F32)],
        compiler_params=_cp(("arbitrary", "arbitrary")),
        name="shortconv",
    )(proj, proj, proj, conv_w)


def _pack_gate_cols(dn_b, dn_a, ml_i, ml_f):
    r = dn_b.shape[0]

    def grp(lo, hi, nh):
        a = jnp.stack([lo[:, :nh], lo[:, nh:], hi[:, :nh], hi[:, nh:]], axis=-1)
        a = jnp.concatenate([a, jnp.zeros((r, nh, 4), a.dtype)], axis=-1).reshape(r, nh * 8)
        return jnp.concatenate([a, jnp.zeros((r, LANES - nh * 8), a.dtype)], axis=-1)

    return jnp.concatenate([grp(dn_b, dn_a, DN_HEADS), grp(ml_i, ml_f, ML_HEADS)], axis=-1)


def _gate_params(dn_a_log, dn_dt_bias, ml_igate_b, ml_fgate_b):
    z_dn = jnp.zeros((1, 2 * DN_HEADS), F32)
    z_ml = jnp.zeros((1, 2 * ML_HEADS), F32)
    flat = lambda a: a.astype(F32).reshape(1, -1)
    rows = [_pack_gate_cols(z_dn, flat(dn_a_log), z_ml, z_ml),
            _pack_gate_cols(z_dn, flat(dn_dt_bias), z_ml, z_ml),
            _pack_gate_cols(z_dn, z_dn, flat(ml_igate_b), z_ml),
            _pack_gate_cols(z_dn, z_dn, z_ml, flat(ml_fgate_b))]
    return jnp.concatenate(rows + [jnp.zeros((4, 2 * LANES), F32)], axis=0)


def _outproj_kernel(a1_ref, a2_ref, a3_ref, w1_ref, w2_ref, w3_ref, x_ref, o_ref):
    acc = jnp.dot(a1_ref[...], w1_ref[...], preferred_element_type=F32)
    acc += jnp.dot(a2_ref[...], w2_ref[...], preferred_element_type=F32)
    acc += jnp.dot(a3_ref[...], w3_ref[...], preferred_element_type=F32)
    o_ref[...] = x_ref[...] + acc


def _outproj(dn, ml, sc, w_out, x, tm=1024, tn=512):
    t, d = x.shape
    k1, k2, k3 = dn.shape[1], ml.shape[1], sc.shape[1]
    assert k1 == k2 and (k1 + k2) % k3 == 0
    return pl.pallas_call(
        _outproj_kernel,
        out_shape=jax.ShapeDtypeStruct((t, d), F32),
        grid=(t // tm, d // tn),
        in_specs=[pl.BlockSpec((tm, k1), lambda i, j: (i, 0)),
                  pl.BlockSpec((tm, k2), lambda i, j: (i, 0)),
                  pl.BlockSpec((tm, k3), lambda i, j: (i, 0)),
                  pl.BlockSpec((k1, tn), lambda i, j: (0, j)),
                  pl.BlockSpec((k2, tn), lambda i, j: (1, j)),
                  pl.BlockSpec((k3, tn), lambda i, j: ((k1 + k2) // k3, j)),
                  pl.BlockSpec((tm, tn), lambda i, j: (i, j))],
        out_specs=pl.BlockSpec((tm, tn), lambda i, j: (i, j)),
        compiler_params=_cp(("arbitrary", "arbitrary")),
        name="outproj",
    )(dn, ml, sc, w_out, w_out, w_out, x)


ROUTE_TM = 256
EXP_ROW0 = 32
HALF_D = D_MODEL // 2


def _router_kernel(x_ref, nw_ref, wr_ref, br_ref, hp_ref, meta_ref, wc_ref, cnt_ref, base_ref):
    tm = x_ref.shape[0]

    @pl.when(pl.program_id(0) == 0)
    def _():
        base_ref[...] = jnp.zeros(base_ref.shape, F32)

    x = x_ref[...]
    h = x * lax.rsqrt(jnp.mean(x * x, axis=-1, keepdims=True) + EPS) * nw_ref[...]
    bits = pltpu.bitcast(h.astype(BF16).astype(F32), U32)
    hp_ref[...] = (bits[:, :HALF_D] >> 16) | bits[:, HALF_D:]

    lt = (_mm_f32(h, wr_ref[...]) + br_ref[...]).T
    row8 = lax.broadcasted_iota(I32, (8, tm), 0)
    gl = jnp.where(row8 < N_GROUPS, lt[0:8, :], NEG)
    gex = jnp.exp(gl - jnp.max(gl, axis=0, keepdims=True))
    gp = gex / jnp.sum(gex, axis=0, keepdims=True)
    g_w = jnp.max(gp, axis=0, keepdims=True)
    g_idx = jnp.min(jnp.where(gp == g_w, row8, 8), axis=0, keepdims=True)
    el = lt[EXP_ROW0 + 24:EXP_ROW0 + 32, :]
    for g in (2, 1, 0):
        el = jnp.where(g_idx == g, lt[EXP_ROW0 + 8 * g:EXP_ROW0 + 8 * g + 8, :], el)
    ee = jnp.exp(el - jnp.max(el, axis=0, keepdims=True))
    p = ee / jnp.sum(ee, axis=0, keepdims=True)
    p1 = jnp.max(p, axis=0, keepdims=True)
    i1 = jnp.min(jnp.where(p == p1, row8, 8), axis=0, keepdims=True)
    pm = jnp.where(row8 == i1, -1.0, p)
    p2 = jnp.max(pm, axis=0, keepdims=True)
    i2 = jnp.min(jnp.where(pm == p2, row8, 8), axis=0, keepdims=True)
    den = p1 + p2
    w1 = g_w * p1 / den
    w2 = g_w * p2 / den
    e1 = g_idx * EXPERTS_PER_GROUP + i1
    e2 = g_idx * EXPERTS_PER_GROUP + i2

    rowe = lax.broadcasted_iota(I32, (N_EXPERTS, tm), 0)
    oh1 = jnp.where(rowe == e1, 1.0, 0.0)
    oh2 = jnp.where(rowe == e2, 1.0, 0.0)
    oh = (oh1 + oh2).astype(BF16)
    ri = lax.broadcasted_iota(I32, (tm, tm), 0)
    ci = lax.broadcasted_iota(I32, (tm, tm), 1)
    before = jnp.where(ri < ci, 1.0, 0.0).astype(BF16)
    ones = jnp.ones((tm, tm), BF16)
    tot = jnp.dot(oh, before, preferred_element_type=F32) + base_ref[...]
    r1 = jnp.sum(oh1 * tot, axis=0, keepdims=True)
    r2 = jnp.sum(oh2 * tot, axis=0, keepdims=True)
    base_ref[...] = base_ref[...] + jnp.dot(oh, ones, preferred_element_type=F32)
    cnt_ref[...] = base_ref[...]

    meta_ref[...] = jnp.concatenate(
        [e1, e2, r1.astype(I32), r2.astype(I32), jnp.zeros((4, tm), I32)], axis=0)
    wrow = lax.broadcasted_iota(I32, (LANES, tm), 0)
    wc_ref[...] = jnp.where(wrow == 0, w1, jnp.where(wrow == 1, w2, 0.0)).T


def _router(x, norm_w, wr, br):
    t, d = x.shape
    tm = ROUTE_TM
    return pl.pallas_call(
        _router_kernel,
        out_shape=(jax.ShapeDtypeStruct((t, HALF_D), U32),
                   jax.ShapeDtypeStruct((8, t), I32),
                   jax.ShapeDtypeStruct((t, LANES), F32),
                   jax.ShapeDtypeStruct((N_EXPERTS, tm), F32)),
        grid=(t // tm,),
        in_specs=[pl.BlockSpec((tm, d), lambda i: (i, 0)),
                  pl.BlockSpec((1, d), lambda i: (0, 0)),
                  pl.BlockSpec((d, LANES), lambda i: (0, 0)),
                  pl.BlockSpec((1, LANES), lambda i: (0, 0))],
        out_specs=(pl.BlockSpec((tm, HALF_D), lambda i: (i, 0)),
                   pl.BlockSpec((8, tm), lambda i: (0, i)),
                   pl.BlockSpec((tm, LANES), lambda i: (i, 0)),
                   pl.BlockSpec((N_EXPERTS, tm), lambda i: (0, 0))),
        scratch_shapes=[pltpu.VMEM((N_EXPERTS, tm), F32)],
        compiler_params=_cp(("arbitrary",)),
        name="router",
    )(x, norm_w.reshape(1, d), wr, br)


SCAT_TM = 256


def _scatter_kernel(zs_ref, ze_ref, nu_ref, s1_ref, s2_ref, hp_ref, xs_ref, zero_ref, sem):
    i = pl.program_id(0)
    tm = hp_ref.shape[0]
    n_all = xs_ref.shape[0] // MOE_BLK

    @pl.when(i == 0)
    def _():
        zero_ref[...] = jnp.zeros(zero_ref.shape, zero_ref.dtype)

        def zfill(e, c):
            zero_row = lambda r: pltpu.make_async_copy(zero_ref.at[pl.ds(0, 1), :], xs_ref.at[pl.ds(r, 1), :], sem)

            def one(r, c2):
                zero_row(r).start()
                return c2

            def one_wait(r, c2):
                zero_row(r).wait()
                return c2

            lax.fori_loop(zs_ref[e], ze_ref[e], one, 0)
            lax.fori_loop(zs_ref[e], ze_ref[e], one_wait, 0)
            return c

        def zero_block(b):
            r0 = pl.multiple_of(b * MOE_BLK, MOE_BLK)
            return pltpu.make_async_copy(zero_ref, xs_ref.at[pl.ds(r0, MOE_BLK), :], sem)

        def tfill(b, c):
            zero_block(b).start()
            return c

        def twait(b, c):
            zero_block(b).wait()
            return c

        lax.fori_loop(0, N_EXPERTS, zfill, 0)
        lax.fori_loop(nu_ref[0], n_all, tfill, 0)
        lax.fori_loop(nu_ref[0], n_all, twait, 0)

    def row_copies(t):
        src = hp_ref.at[pl.ds(t, 1), :]
        return (pltpu.make_async_copy(src, xs_ref.at[pl.ds(s1_ref[0, 0, t], 1), :], sem),
                pltpu.make_async_copy(src, xs_ref.at[pl.ds(s2_ref[0, 0, t], 1), :], sem))

    def issue(t, c):
        for cp in row_copies(t):
            cp.start()
        return c

    def drain(t, c):
        for cp in row_copies(t):
            cp.wait()
        return c

    lax.fori_loop(0, tm, issue, 0, unroll=8)
    lax.fori_loop(0, tm, drain, 0, unroll=8)


def _scatter_rows(zero_start, zero_end, n_used, slot1, slot2, hp, n_rows):
    t, w = hp.shape
    tm = SCAT_TM
    s3 = lambda a: a.reshape(t // tm, 1, tm)
    grid_spec = pltpu.PrefetchScalarGridSpec(
        num_scalar_prefetch=3,
        grid=(t // tm,),
        in_specs=[pl.BlockSpec((1, 1, tm), lambda i, zs, ze, nu: (i, 0, 0), memory_space=pltpu.SMEM),
                  pl.BlockSpec((1, 1, tm), lambda i, zs, ze, nu: (i, 0, 0), memory_space=pltpu.SMEM),
                  pl.BlockSpec((tm, w), lambda i, zs, ze, nu: (i, 0))],
        out_specs=pl.BlockSpec(memory_space=pl.ANY),
        scratch_shapes=[pltpu.VMEM((MOE_BLK, w), U32), pltpu.SemaphoreType.DMA(())],
    )
    return pl.pallas_call(
        _scatter_kernel,
        out_shape=jax.ShapeDtypeStruct((n_rows, w), U32),
        grid_spec=grid_spec,
        compiler_params=_cp(("arbitrary",)),
        name="scatter_rows",
    )(zero_start, zero_end, n_used, s3(slot1), s3(slot2), hp)


FF_TN = 256


def _unpack_rows(x_ref, xs_ref):
    u = x_ref[...]
    xs_ref[:, :HALF_D] = pltpu.bitcast(u << 16, F32).astype(BF16)
    xs_ref[:, HALF_D:] = pltpu.bitcast(u & jnp.uint32(0xFFFF0000), F32).astype(BF16)


def _ffn_up_kernel(be_ref, nu_ref, x_ref, wg_ref, wu_ref, o_ref, wgb_ref, wub_ref, xs_ref):
    i = pl.program_id(1)

    @pl.when(i < nu_ref[0])
    def _():
        @pl.when((i == 0) | (be_ref[i] != be_ref[jnp.maximum(i - 1, 0)]))
        def _():
            wgb_ref[...] = wg_ref[0, 0].astype(BF16)
            wub_ref[...] = wu_ref[0, 0].astype(BF16)

        _unpack_rows(x_ref, xs_ref)
        x = xs_ref[...]
        g = jnp.dot(x, wgb_ref[...], preferred_element_type=F32)
        u = jnp.dot(x, wub_ref[...], preferred_element_type=F32)
        o_ref[...] = (g * _sigmoid(g) * u).astype(o_ref.dtype)

    @pl.when(i >= nu_ref[0])
    def _():
        o_ref[...] = jnp.zeros(o_ref.shape, o_ref.dtype)


def _ffn_up(block_exp, n_used, xs, w_gate, w_up, layer, n_blocks):
    d, ff = w_gate.shape[2], w_gate.shape[3]
    blk = lambda i, nu: jnp.minimum(i, nu[0] - 1)
    grid_spec = pltpu.PrefetchScalarGridSpec(
        num_scalar_prefetch=2,
        grid=(ff // FF_TN, n_blocks),
        in_specs=[pl.BlockSpec((MOE_BLK, HALF_D), lambda j, i, be, nu: (blk(i, nu), 0)),
                  pl.BlockSpec((1, 1, d, FF_TN), lambda j, i, be, nu: (layer, be[blk(i, nu)], 0, j)),
                  pl.BlockSpec((1, 1, d, FF_TN), lambda j, i, be, nu: (layer, be[blk(i, nu)], 0, j))],
        out_specs=pl.BlockSpec((MOE_BLK, FF_TN), lambda j, i, be, nu: (i, j)),
        scratch_shapes=[pltpu.VMEM((d, FF_TN), BF16), pltpu.VMEM((d, FF_TN), BF16),
                        pltpu.VMEM((MOE_BLK, d), BF16)],
    )
    return pl.pallas_call(
        _ffn_up_kernel,
        out_shape=jax.ShapeDtypeStruct((n_blocks * MOE_BLK, ff), BF16),
        grid_spec=grid_spec,
        compiler_params=_cp(("arbitrary", "arbitrary")),
        name="ffn_up",
    )(block_exp, n_used, xs, w_gate, w_up)


def _ffn_down_kernel(be_ref, nu_ref, h_ref, wd_ref, y_ref, wdb_ref):
    i = pl.program_id(0)

    @pl.when(i < nu_ref[0])
    def _():
        @pl.when((i == 0) | (be_ref[i] != be_ref[jnp.maximum(i - 1, 0)]))
        def _():
            wdb_ref[...] = wd_ref[0, 0].astype(BF16)

        y_ref[...] = jnp.dot(h_ref[...], wdb_ref[...], preferred_element_type=F32)

    @pl.when(i >= nu_ref[0])
    def _():
        y_ref[...] = jnp.zeros(y_ref.shape, y_ref.dtype)


def _ffn_down(block_exp, n_used, hmid, w_down, layer, n_blocks):
    ff, d = w_down.shape[2], w_down.shape[3]
    blk = lambda i, nu: jnp.minimum(i, nu[0] - 1)
    grid_spec = pltpu.PrefetchScalarGridSpec(
        num_scalar_prefetch=2,
        grid=(n_blocks,),
        in_specs=[pl.BlockSpec((MOE_BLK, ff), lambda i, be, nu: (blk(i, nu), 0)),
                  pl.BlockSpec((1, 1, ff, d), lambda i, be, nu: (layer, be[blk(i, nu)], 0, 0))],
        out_specs=pl.BlockSpec((MOE_BLK, d), lambda i, be, nu: (i, 0)),
        scratch_shapes=[pltpu.VMEM((ff, d), BF16)],
    )
    return pl.pallas_call(
        _ffn_down_kernel,
        out_shape=jax.ShapeDtypeStruct((n_blocks * MOE_BLK, d), F32),
        grid_spec=grid_spec,
        compiler_params=_cp(("arbitrary",)),
        name="ffn_down",
    )(block_exp, n_used, hmid, w_down)


COMB_TM = 256


def _combine_kernel(s1_ref, s2_ref, x_ref, wc_ref, fw_ref, y_ref, o_ref, ya_ref, yb_ref, sem, *, final):
    tm = x_ref.shape[0]

    def row_copies(t):
        return (pltpu.make_async_copy(y_ref.at[pl.ds(s1_ref[0, 0, t], 1), :], ya_ref.at[pl.ds(t, 1), :], sem),
                pltpu.make_async_copy(y_ref.at[pl.ds(s2_ref[0, 0, t], 1), :], yb_ref.at[pl.ds(t, 1), :], sem))

    def issue(t, c):
        for cp in row_copies(t):
            cp.start()
        return c

    def drain(t, c):
        for cp in row_copies(t):
            cp.wait()
        return c

    lax.fori_loop(0, tm, issue, 0, unroll=8)
    lax.fori_loop(0, tm, drain, 0, unroll=8)
    o = x_ref[...] + wc_ref[:, 0:1] * ya_ref[...] + wc_ref[:, 1:2] * yb_ref[...]
    if final:
        o = o * lax.rsqrt(jnp.mean(o * o, axis=-1, keepdims=True) + EPS) * fw_ref[...]
    o_ref[...] = o


def _combine(slot1, slot2, x, wcol, y, final_w, final):
    t, d = x.shape
    tm = COMB_TM
    s3 = lambda a: a.reshape(t // tm, 1, tm)
    return pl.pallas_call(
        functools.partial(_combine_kernel, final=final),
        out_shape=jax.ShapeDtypeStruct((t, d), F32),
        grid=(t // tm,),
        in_specs=[pl.BlockSpec((1, 1, tm), lambda i: (i, 0, 0), memory_space=pltpu.SMEM),
                  pl.BlockSpec((1, 1, tm), lambda i: (i, 0, 0), memory_space=pltpu.SMEM),
                  pl.BlockSpec((tm, d), lambda i: (i, 0)),
                  pl.BlockSpec((tm, LANES), lambda i: (i, 0)),
                  pl.BlockSpec((1, d), lambda i: (0, 0)),
                  pl.BlockSpec(memory_space=pl.ANY)],
        out_specs=pl.BlockSpec((tm, d), lambda i: (i, 0)),
        scratch_shapes=[pltpu.VMEM((tm, d), F32), pltpu.VMEM((tm, d), F32),
                        pltpu.SemaphoreType.DMA(())],
        compiler_params=_cp(("arbitrary",)),
        name="combine",
    )(s3(slot1), s3(slot2), x, wcol, final_w.reshape(1, d), y)


def _mixer_layer(xf, batch, seq, norm_w, w_in, dn_conv_w, dn_a_log, dn_dt_bias, dn_norm_w, ml_igate_b,
                 ml_fgate_b, ml_norm_w, sc_conv_w, w_out):
    c0 = 3 * DN_WIDTH + DN_WIDTH
    c1 = c0 + 4 * DN_HEADS
    c2 = c1 + 2 * ML_HEADS * ML_QK + 2 * ML_WIDTH
    c3 = c2 + 4 * ML_HEADS
    w_main = jnp.concatenate([w_in[:, :c0], w_in[:, c1:c2], w_in[:, c3:]], axis=1).astype(BF16)
    w_gate = _pack_gate_cols(w_in[:, c0:c0 + 2 * DN_HEADS], w_in[:, c0 + 2 * DN_HEADS:c1],
                             w_in[:, c2:c2 + 2 * ML_HEADS], w_in[:, c2 + 2 * ML_HEADS:c3]).astype(BF16)
    h = _rmsnorm(xf, norm_w, BF16)
    proj, graw = _inproj(h, w_main, w_gate)
    gc, gr = _gate_prep(graw, _gate_params(dn_a_log, dn_dt_bias, ml_igate_b, ml_fgate_b))
    dn = _deltanet(proj, dn_conv_w, gc, gr, dn_norm_w, batch, seq)
    ml = _mlstm(proj, gc, gr, ml_norm_w, batch, seq)
    sc = _shortconv(proj, sc_conv_w, batch, seq)
    return _outproj(dn, ml, sc, w_out.astype(BF16), xf)


def _moe_layer(xf, norm_w, rg_w, rg_b, re_w, re_b, w_gate, w_up, w_down, layer, final_w, final):
    t, d = xf.shape
    def lanes(g, e):
        r = g.shape[0]
        return jnp.concatenate([g, jnp.zeros((r, EXP_ROW0 - N_GROUPS), F32), e,
                                jnp.zeros((r, LANES - EXP_ROW0 - N_EXPERTS), F32)], axis=1)

    wr = lanes(rg_w, re_w)
    br = lanes(rg_b.reshape(1, -1), re_b.reshape(1, -1))
    hp, meta, wcol, cnt = _router(xf, norm_w, wr, br)
    counts = cnt[:, 0].astype(I32)
    padded = (counts + MOE_BLK - 1) // MOE_BLK * MOE_BLK
    pad_end = jnp.cumsum(padded)
    pad_start = pad_end - padded
    n_blocks = -(-(2 * t + N_EXPERTS * (MOE_BLK - 1)) // MOE_BLK)
    n_used = (pad_end[-1:] // MOE_BLK).astype(I32)
    blk_row = jnp.arange(n_blocks, dtype=I32)[:, None] * MOE_BLK
    block_exp = jnp.minimum(jnp.sum((pad_end[None, :] <= blk_row).astype(I32), axis=1), N_EXPERTS - 1)
    slot1 = pad_start[meta[0]] + meta[2]
    slot2 = pad_start[meta[1]] + meta[3]
    xs = _scatter_rows(pad_start + counts, pad_end, n_used, slot1, slot2, hp, n_blocks * MOE_BLK)
    hmid = _ffn_up(block_exp, n_used, xs, w_gate, w_up, layer, n_blocks)
    y = _ffn_down(block_exp, n_used, hmid, w_down, layer, n_blocks)
    return _combine(slot1, slot2, xf, wcol, y, final_w, final)


def kernel(x, norm_mix_w, w_in, dn_conv_w, dn_a_log, dn_dt_bias, dn_norm_w, ml_igate_b, ml_fgate_b, ml_norm_w,
           sc_conv_w, w_out, norm_ffn_w, router_group_w, router_group_b, router_expert_w, router_expert_b,
           expert_w_gate, expert_w_up, expert_w_down, final_norm_w):
    batch, seq, d = x.shape
    depth = w_in.shape[0]
    xf = x.reshape(batch * seq, d)
    for l in range(depth):
        xf = _mixer_layer(xf, batch, seq, norm_mix_w[l], w_in[l], dn_conv_w[l], dn_a_log[l], dn_dt_bias[l],
                          dn_norm_w[l], ml_igate_b[l], ml_fgate_b[l], ml_norm_w[l], sc_conv_w[l], w_out[l])
        xf = _moe_layer(xf, norm_ffn_w[l], router_group_w[l], router_group_b[l], router_expert_w[l],
                        router_expert_b[l], expert_w_gate, expert_w_up, expert_w_down, l,
                        final_norm_w, l == depth - 1)
    return xf.reshape(batch, seq, d)
```

```python
import functools

import numpy as np
import jax
import jax.numpy as jnp
from jax import lax
from jax.experimental import pallas as pl
from jax.experimental.pallas import tpu as pltpu

F32 = jnp.float32
BF16 = jnp.bfloat16
I32 = jnp.int32
U32 = jnp.uint32

D_MODEL = 4096
DN_HEADS = 12
DN_DIM = 128
DN_WIDTH = DN_HEADS * DN_DIM
DN_CONV = 5
ML_HEADS = 6
ML_QK = 128
ML_V = 256
ML_WIDTH = ML_HEADS * ML_V
SC_WIDTH = 1024
SC_CONV = 3
N_GROUPS = 4
EXPERTS_PER_GROUP = 8
N_EXPERTS = N_GROUPS * EXPERTS_PER_GROUP
D_FF = 768
EPS = 1e-6

LANES = 128
SUBLANES = 8
VMEM_LIMIT = 52 * 1024 * 1024

CH = 64
TL = 256
CPT = TL // CH
DN_TPI = 1
DN_HPS = 2
N_HEADS_ALL = DN_HEADS + ML_HEADS
MOE_BLK = 256
NEG = -1e30

OFF_DN_Q, OFF_DN_K, OFF_DN_V, OFF_DN_Z = 0, 1536, 3072, 4608
OFF_ML_Q, OFF_ML_K, OFF_ML_V, OFF_ML_O = 6144, 6912, 7680, 9216
OFF_SC_B, OFF_SC_C, OFF_SC_X = 10752, 11776, 12800
N_MAIN = 13824


def _cp(sem, vmem=VMEM_LIMIT):
    return pltpu.CompilerParams(dimension_semantics=sem, vmem_limit_bytes=vmem)


def _mm(a, b):
    return jnp.dot(a.astype(BF16), b.astype(BF16), preferred_element_type=F32)


def _mm_nt(a, b):
    return lax.dot_general(a.astype(BF16), b.astype(BF16), (((1,), (1,)), ((), ())),
                           preferred_element_type=F32)


def _mm_f32(a, b):
    return jnp.dot(a, b, preferred_element_type=F32, precision=lax.Precision.HIGHEST)


def _sigmoid(x):
    return 1.0 / (1.0 + jnp.exp(-x))


def _softplus(x):
    return jnp.maximum(x, 0.0) + jnp.log(1.0 + jnp.exp(-jnp.abs(x)))


def _rms_kernel(x_ref, w_ref, o_ref):
    x = x_ref[...]
    ms = jnp.mean(x * x, axis=-1, keepdims=True)
    o_ref[...] = (x * lax.rsqrt(ms + EPS) * w_ref[...]).astype(o_ref.dtype)


def _rmsnorm(x2d, w, out_dtype, tm=256):
    t, d = x2d.shape
    return pl.pallas_call(
        _rms_kernel,
        out_shape=jax.ShapeDtypeStruct((t, d), out_dtype),
        grid=(t // tm,),
        in_specs=[pl.BlockSpec((tm, d), lambda i: (i, 0)), pl.BlockSpec((1, d), lambda i: (0, 0))],
        out_specs=pl.BlockSpec((tm, d), lambda i: (i, 0)),
        compiler_params=_cp(("arbitrary",)),
        name="rmsnorm",
    )(x2d, w.reshape(1, d))


def _inproj_kernel(a_ref, w_ref, wg_ref, o_ref, g_ref):
    a = a_ref[...]
    o_ref[...] = jnp.dot(a, w_ref[...], preferred_element_type=F32).astype(o_ref.dtype)

    @pl.when(pl.program_id(1) == 0)
    def _():
        g_ref[...] = jnp.dot(a, wg_ref[...], preferred_element_type=F32)


def _inproj(h, w_main, w_gate, tm=1024, tn=512):
    t, d = h.shape
    n = w_main.shape[1]
    ng = w_gate.shape[1]
    return pl.pallas_call(
        _inproj_kernel,
        out_shape=(jax.ShapeDtypeStruct((t, n), BF16), jax.ShapeDtypeStruct((t, ng), F32)),
        grid=(t // tm, n // tn),
        in_specs=[pl.BlockSpec((tm, d), lambda i, j: (i, 0)),
                  pl.BlockSpec((d, tn), lambda i, j: (0, j)),
                  pl.BlockSpec((d, ng), lambda i, j: (0, 0))],
        out_specs=(pl.BlockSpec((tm, tn), lambda i, j: (i, j)),
                   pl.BlockSpec((tm, ng), lambda i, j: (i, 0))),
        compiler_params=_cp(("arbitrary", "arbitrary")),
        name="inproj",
    )(h, w_main, w_gate)


def _gate_kernel(raw_ref, p_ref, gc_ref, gr_ref):
    x = raw_ref[...]
    lane = lax.broadcasted_iota(I32, x.shape, 1)
    chn = lane & 7
    is_ml = lane >= LANES
    a_log, dt_b, ig_b, fg_b = p_ref[0:1, :], p_ref[1:2, :], p_ref[2:3, :], p_ref[3:4, :]
    beta = _sigmoid(x)
    g = -jnp.exp(a_log) * _softplus(x + dt_b)
    ig = x + ig_b
    lf = -_softplus(-(x + fg_b))
    lo = chn < 2
    val = jnp.where(is_ml, jnp.where(lo, ig, lf), jnp.where(lo, beta, g))
    val = jnp.where(chn < 4, val, 0.0)
    ri = lax.broadcasted_iota(I32, (TL, TL), 0)
    ci = lax.broadcasted_iota(I32, (TL, TL), 1)
    same = (ri >> 6) == (ci >> 6)
    m_f = jnp.where(same & (ri >= ci), 1.0, 0.0)
    m_b = jnp.where(same & (ri <= ci), 1.0, 0.0)
    m_t = jnp.where(same, 1.0, 0.0)
    cum_f = _mm_f32(m_f, val)
    cum_b = _mm_f32(m_b, val)
    tot = pltpu.roll(_mm_f32(m_t, val), 2, axis=1)
    out = jnp.where(chn == 2, cum_f, jnp.where(chn == 3, cum_b, jnp.where((chn == 4) | (chn == 5), tot, val)))
    gr_ref[0] = out.T
    for hh in range(N_HEADS_ALL):
        grp, h = (0, hh) if hh < DN_HEADS else (1, hh - DN_HEADS)
        v = out[:, grp * LANES:(grp + 1) * LANES]
        sh = (LANES - 8 * h) % LANES
        gc_ref[hh] = pltpu.roll(v, sh, axis=1) if sh else v


def _gate_prep(raw, params):
    t = raw.shape[0]
    nt = t // TL
    return pl.pallas_call(
        _gate_kernel,
        out_shape=(jax.ShapeDtypeStruct((N_HEADS_ALL, t, LANES), F32),
                   jax.ShapeDtypeStruct((nt, 2 * LANES, TL), F32)),
        grid=(nt,),
        in_specs=[pl.BlockSpec((TL, 2 * LANES), lambda i: (i, 0)),
                  pl.BlockSpec((8, 2 * LANES), lambda i: (0, 0))],
        out_specs=(pl.BlockSpec((N_HEADS_ALL, TL, LANES), lambda i: (0, i, 0)),
                   pl.BlockSpec((1, 2 * LANES, TL), lambda i: (i, 0, 0))),
        compiler_params=_cp(("arbitrary",)),
        name="gate_prep",
    )(raw, params)


def _conv_centred(x_ref, w_ref, pad_ref, width):
    s = x_ref.shape[0]
    half = width // 2
    pad_ref[0:8, :] = jnp.zeros((8, pad_ref.shape[1]), F32)
    pad_ref[s + 8:s + 16, :] = jnp.zeros((8, pad_ref.shape[1]), F32)
    pad_ref[8:s + 8, :] = x_ref[...].astype(F32)
    acc = None
    for i in range(width):
        term = w_ref[i:i + 1, :] * pad_ref[8 - half + i:8 - half + i + s, :]
        acc = term if acc is None else acc + term
    return acc


def _conv_centred_tiles(x_ref, w_ref, pad_ref, width, cols):
    s = x_ref.shape[0]
    half = width // 2
    pad_ref[0:8, :] = jnp.zeros((8, pad_ref.shape[1]), F32)
    pad_ref[s + 8:s + 16, :] = jnp.zeros((8, pad_ref.shape[1]), F32)
    pad_ref[8:s + 8, :] = x_ref[:, cols].astype(F32)
    for rt in range(s // TL):
        base = 8 - half + rt * TL
        acc = None
        for i in range(width):
            term = w_ref[i:i + 1, cols] * pad_ref[base + i:base + i + TL, :]
            acc = term if acc is None else acc + term
        yield slice(rt * TL, (rt + 1) * TL), acc


def _tile_masks():
    ri = lax.broadcasted_iota(I32, (TL, TL), 0)
    ci = lax.broadcasted_iota(I32, (TL, TL), 1)
    return ri, ci


def _unit_tri_inverse(mats, same16, same32, eye):
    each = lambda f, *ls: [f(*xs) for xs in zip(*ls)]
    a16 = each(lambda a: jnp.where(same16, a, 0.0).astype(BF16), mats)
    b1 = each(lambda x: _mm(x, x).astype(BF16), a16)
    p = each(lambda x: eye - x.astype(F32), a16)
    b2 = each(lambda x: _mm(x, x).astype(BF16), b1)
    p = each(lambda x, b: x + _mm(x, b), p, b1)
    b3 = each(lambda x: _mm(x, x).astype(BF16), b2)
    p = each(lambda x, b: x + _mm(x, b), p, b2)
    p = each(lambda x, b: x + _mm(x, b), p, b3)
    off32 = same32 & jnp.logical_not(same16)
    a32 = each(lambda a: jnp.where(off32, a, 0.0).astype(BF16), mats)
    pb = each(lambda x: x.astype(BF16), p)
    t = each(_mm, pb, a32)
    p = each(lambda x, y, z: x - _mm(y, z), p, t, pb)
    a64 = each(lambda a: jnp.where(same32, 0.0, a).astype(BF16), mats)
    pb = each(lambda x: x.astype(BF16), p)
    t = each(_mm, pb, a64)
    p = each(lambda x, y, z: x - _mm(y, z), p, t, pb)
    return p


def _dn_prep_kernel(x_ref, w_ref, o_ref, pad_ref):
    j = pl.program_id(1)
    n_qk = 2 * DN_WIDTH // x_ref.shape[1]
    qscale = jnp.where(j < n_qk // 2, DN_DIM ** -0.5, 1.0)
    for h in range(x_ref.shape[1] // DN_DIM):
        sl = slice(h * DN_DIM, (h + 1) * DN_DIM)
        for r, y in _conv_centred_tiles(x_ref, w_ref, pad_ref, DN_CONV, sl):
            y = y * _sigmoid(y)
            yn = y * (lax.rsqrt(jnp.sum(y * y, axis=-1, keepdims=True) + EPS) * qscale)
            o_ref[r, sl] = jnp.where(j < n_qk, yn, y).astype(o_ref.dtype)


def _dn_prep(proj, conv_w, batch, seq, wc=256):
    t = proj.shape[0]
    n = 3 * DN_WIDTH
    return pl.pallas_call(
        _dn_prep_kernel,
        out_shape=jax.ShapeDtypeStruct((t, n), BF16),
        grid=(batch, n // wc),
        in_specs=[pl.BlockSpec((seq, wc), lambda b, j: (b, j)), pl.BlockSpec((DN_CONV, wc), lambda b, j: (0, j))],
        out_specs=pl.BlockSpec((seq, wc), lambda b, j: (b, j)),
        scratch_shapes=[pltpu.VMEM((seq + 16, DN_DIM), F32)],
        compiler_params=_cp(("arbitrary", "arbitrary")),
        name="dn_prep",
    )(proj, conv_w)


def _dn_kernel(q_ref, k_ref, v_ref, z_ref, gc_ref, gr_ref, nw_ref, o_ref,
               u_ref, wq_ref, ak_ref, eg_ref, oo_ref):
    s = q_ref.shape[0]
    nt = s // TL
    nc = s // CH
    heads = range(DN_HPS)
    hl = [slice(h * DN_DIM, (h + 1) * DN_DIM) for h in heads]

    def tile_body(j, carry):
        ri, ci = _tile_masks()
        same64 = (ri >> 6) == (ci >> 6)
        same32 = (ri >> 5) == (ci >> 5)
        same16 = (ri >> 4) == (ci >> 4)
        eye = jnp.where(ri == ci, 1.0, 0.0)
        incl = [same64 & (ri >= ci), same64 & (ri <= ci)]
        strict = [same64 & (ri > ci), same64 & (ri < ci)]
        tiles = [j * DN_TPI + t for t in range(DN_TPI)]
        r0 = [pl.multiple_of(i * TL, TL) for i in tiles]
        ht = [(h, t) for h in heads for t in range(DN_TPI)]
        qn = {(h, t): q_ref[pl.ds(r0[t], TL), hl[h]].astype(F32) for h, t in ht}
        kn = {(h, t): k_ref[pl.ds(r0[t], TL), hl[h]].astype(F32) for h, t in ht}
        vv = {(h, t): v_ref[pl.ds(r0[t], TL), hl[h]].astype(F32) for h, t in ht}
        kk = {x: _mm_nt(kn[x], kn[x]) for x in ht}
        qk = {x: _mm_nt(qn[x], kn[x]) for x in ht}
        gc = {(h, t): gc_ref[h, pl.ds(r0[t], TL), :] for h, t in ht}
        gr = {(h, t): gr_ref[tiles[t]][8 * h:8 * h + 8, :] for h, t in ht}
        ch = [(h, t, d) for h, t in ht for d in (0, 1)]
        beta_c = [gc[h, t][:, d:d + 1] for h, t, d in ch]
        g_c = [gc[h, t][:, 2 + d:3 + d] for h, t, d in ch]
        gt_c = [gc[h, t][:, 4 + d:5 + d] for h, t, d in ch]
        g_r = [gr[h, t][2 + d:3 + d, :] for h, t, d in ch]
        dec = [jnp.exp(jnp.where(incl[d], g_c[n] - g_r[n], NEG)) for n, (h, t, d) in enumerate(ch)]
        a = [jnp.where(strict[d], beta_c[n] * kk[h, t] * dec[n], 0.0) for n, (h, t, d) in enumerate(ch)]
        tinv = _unit_tri_inverse(a, same16, same32, eye)
        eg_c = [jnp.exp(g) for g in g_c]
        rhs = [jnp.concatenate([beta_c[n] * vv[h, t], (beta_c[n] * eg_c[n]) * kn[h, t]], axis=1)
               for n, (h, t, d) in enumerate(ch)]
        uw = [_mm(x, y) for x, y in zip(tinv, rhs)]
        for n, (h, t, d) in enumerate(ch):
            i = tiles[t]
            u_ref[h, d, pl.ds(r0[t], TL), :] = uw[n][:, :DN_DIM]
            w = uw[n][:, DN_DIM:]
            qe = qn[h, t] * eg_c[n]
            attn = qk[h, t] * dec[n]
            kdt = (kn[h, t] * jnp.exp(gt_c[n] - g_c[n])).T
            egt = jnp.exp(gt_c[n])
            for c in range(CPT):
                rows = slice(c * CH, (c + 1) * CH)
                wq_ref[h, d, i * CPT + c] = jnp.concatenate([w[rows], qe[rows]], axis=0).astype(BF16)
                ak_ref[h, d, i * CPT + c] = jnp.concatenate([attn[rows, rows], kdt[:, rows]], axis=0).astype(BF16)
                eg_ref[h, d, i * CPT + c] = jnp.broadcast_to(egt[c * CH:c * CH + 8, :], (8, LANES))
        return carry

    lax.fori_loop(0, nt // DN_TPI, tile_body, 0)

    hd = [(h, d) for h in heads for d in (0, 1)]

    def chunk_body(c, carry):
        states = list(carry)
        ws, vnew, m2 = {}, {}, {}
        cc = [c, nc - 1 - c]
        r = [pl.multiple_of(x * CH, CH) for x in cc]
        for n, (h, d) in enumerate(hd):
            ws[n] = jnp.dot(wq_ref[h, d, cc[d]], states[n].astype(BF16), preferred_element_type=F32)
        for n, (h, d) in enumerate(hd):
            vnew[n] = u_ref[h, d, pl.ds(r[d], CH), :] - ws[n][:CH]
        for n, (h, d) in enumerate(hd):
            m2[n] = jnp.dot(ak_ref[h, d, cc[d]], vnew[n].astype(BF16), preferred_element_type=F32)
        for n, (h, d) in enumerate(hd):
            oo_ref[h, d, pl.ds(r[d], CH), :] = ws[n][CH:] + m2[n][:CH]
            states[n] = eg_ref[h, d, cc[d]][0:1, :] * states[n] + m2[n][CH:]
        return tuple(states)

    z0 = jnp.zeros((DN_DIM, DN_DIM), F32)
    lax.fori_loop(0, nc, chunk_body, tuple(z0 for _ in hd))

    for h in heads:
        for rt in range(s // TL):
            r = slice(rt * TL, (rt + 1) * TL)
            o = oo_ref[h, 0, r, :] + oo_ref[h, 1, r, :]
            o = o * lax.rsqrt(jnp.mean(o * o, axis=-1, keepdims=True) + EPS) * nw_ref[...]
            z = z_ref[r, hl[h]].astype(F32)
            o_ref[r, hl[h]] = (o * (z * _sigmoid(z))).astype(o_ref.dtype)


def _deltanet(proj, conv_w, gc, gr, norm_w, batch, seq):
    t = proj.shape[0]
    nc = seq // CH
    nt = seq // TL
    wb = DN_HPS * DN_DIM
    qkv = _dn_prep(proj, conv_w, batch, seq)
    qb, kb, vb, zb = OFF_DN_Q // wb, OFF_DN_K // wb, OFF_DN_V // wb, OFF_DN_Z // wb
    col = lambda off: pl.BlockSpec((seq, wb), lambda b, h, off=off: (b, off + h))
    return pl.pallas_call(
        _dn_kernel,
        out_shape=jax.ShapeDtypeStruct((t, DN_WIDTH), BF16),
        grid=(batch, DN_HEADS // DN_HPS),
        in_specs=[col(qb), col(kb), col(vb), col(zb),
                  pl.BlockSpec((DN_HPS, seq, LANES), lambda b, h: (h, b, 0)),
                  pl.BlockSpec((nt, 8 * DN_HPS, TL), lambda b, h: (b, h, 0)),
                  pl.BlockSpec((1, DN_DIM), lambda b, h: (0, 0))],
        out_specs=pl.BlockSpec((seq, wb), lambda b, h: (b, h)),
        scratch_shapes=[pltpu.VMEM((DN_HPS, 2, seq, DN_DIM), F32),
                        pltpu.VMEM((DN_HPS, 2, nc, 2 * CH, DN_DIM), BF16),
                        pltpu.VMEM((DN_HPS, 2, nc, CH + DN_DIM, CH), BF16),
                        pltpu.VMEM((DN_HPS, 2, nc, 8, LANES), F32),
                        pltpu.VMEM((DN_HPS, 2, seq, DN_DIM), F32)],
        compiler_params=_cp(("arbitrary", "arbitrary")),
        name="deltanet",
    )(qkv, qkv, qkv, proj, gc, gr, norm_w.reshape(1, DN_DIM))


def _ml_kernel(q_ref, k_ref, v_ref, og_ref, gc_ref, gr_ref, nw_ref, o_ref, oo_ref):
    s = q_ref.shape[0]
    nt = s // TL
    scale = ML_QK ** -0.5

    def tile_body(it, carry):
        m_carry = [carry[0], carry[1]]
        c_carry = [carry[2], carry[3]]
        n_carry = [carry[4], carry[5]]
        ri, ci = _tile_masks()
        same64 = (ri >> 6) == (ci >> 6)
        rowc = lax.broadcasted_iota(I32, (TL, 1), 0) >> 6
        colc = lax.broadcasted_iota(I32, (1, TL), 1) >> 6
        dirs = (0, 1)
        order = [list(range(CPT)), list(range(CPT - 1, -1, -1))]
        tile = [it, nt - 1 - it]
        r0 = [pl.multiple_of(i * TL, TL) for i in tile]
        qb = [q_ref[pl.ds(r, TL), :] for r in r0]
        qs = [x.astype(F32) * scale for x in qb]
        kf = [k_ref[pl.ds(r, TL), :].astype(F32) for r in r0]
        vb = [v_ref[pl.ds(r, TL), :] for r in r0]
        qk = [_mm_nt(a, b) for a, b in zip(qs, kf)]
        gc = [gc_ref[0, pl.ds(r, TL), :] for r in r0]
        gr = [gr_ref[i] for i in tile]
        ig_c = [gc[d][:, d:d + 1] for d in dirs]
        b_c = [gc[d][:, 2 + d:3 + d] for d in dirs]
        bt_c = [gc[d][:, 4 + d:5 + d] for d in dirs]
        ig_r = [gr[d][d:d + 1, :] for d in dirs]
        b_r = [gr[d][2 + d:3 + d, :] for d in dirs]
        bt_r = [gr[d][4 + d:5 + d, :] for d in dirs]
        incl = [same64 & (ri >= ci), same64 & (ri <= ci)]
        dmat = [jnp.where(incl[d], b_c[d] - b_r[d] + ig_r[d], NEG) for d in dirs]
        m_intra = [jnp.max(x, axis=1, keepdims=True) for x in dmat]
        d_end_c = [bt_c[d] - b_c[d] + ig_c[d] for d in dirs]
        d_end_r = [bt_r[d] - b_r[d] + ig_r[d] for d in dirs]
        dmax_c = [jnp.max(jnp.where(same64, x, NEG), axis=1, keepdims=True) for x in d_end_r]
        m_in_c, m_out_c, csc = [], [], []
        for d in dirs:
            m_in = jnp.zeros((TL, 1), F32)
            m_out = jnp.zeros((TL, 1), F32)
            m = m_carry[d]
            cs = {}
            for cc in order[d]:
                btot = bt_c[d][cc * CH:cc * CH + 1, :]
                dmx = dmax_c[d][cc * CH:cc * CH + 1, :]
                m_new = jnp.maximum(btot + m, dmx)
                cs[cc] = jnp.exp(btot + m - m_new)
                m_in = jnp.where(rowc == cc, m, m_in)
                m_out = jnp.where(rowc == cc, m_new, m_out)
                m = m_new
            m_carry[d] = m
            m_in_c.append(m_in)
            m_out_c.append(m_out)
            csc.append(cs)
        m_inter = [b_c[d] + m_in_c[d] for d in dirs]
        m_pos = [jnp.maximum(m_intra[d], m_inter[d]) for d in dirs]
        scores = [qk[d] * jnp.exp(dmat[d] - m_pos[d]) for d in dirs]
        inter_scale = [jnp.exp(m_inter[d] - m_pos[d]) for d in dirs]
        rsum = [jnp.sum(x, axis=1, keepdims=True) for x in scores]
        einv = [jnp.exp(-x) for x in m_pos]
        wk = [kf[d] * jnp.exp(d_end_c[d] - m_out_c[d]) for d in dirs]
        wkt = [x.T for x in wk]
        intra = [jnp.dot(scores[d].astype(BF16), vb[d], preferred_element_type=F32) for d in dirs]
        wstack = [jnp.concatenate([jnp.where(colc == c, wkt[d], 0.0) for c in range(CPT)], axis=0).astype(BF16)
                  for d in dirs]
        dcs = [jnp.dot(wstack[d], vb[d], preferred_element_type=F32) for d in dirs]
        c_at, n_row = [], []
        for d in dirs:
            cst, nst = c_carry[d], n_carry[d]
            c_start = {}
            nr = jnp.zeros((TL, ML_QK), F32)
            for cc in order[d]:
                c_start[cc] = cst
                nr = jnp.where(rowc == cc, nst, nr)
                dn = jnp.sum(wk[d][cc * CH:(cc + 1) * CH], axis=0, keepdims=True)
                cst = csc[d][cc] * cst + dcs[d][cc * ML_QK:(cc + 1) * ML_QK]
                nst = csc[d][cc] * nst + dn
            c_carry[d], n_carry[d] = cst, nst
            c_at.append(jnp.concatenate([c_start[c] for c in range(CPT)], axis=0).astype(BF16))
            n_row.append(nr)
        q4 = [jnp.concatenate([jnp.where(rowc == c, qb[d], jnp.zeros_like(qb[d])) for c in range(CPT)], axis=1)
              for d in dirs]
        inter = [jnp.dot(q4[d], c_at[d], preferred_element_type=F32) * scale for d in dirs]
        for d in dirs:
            qn = jnp.sum(qs[d] * n_row[d], axis=1, keepdims=True)
            num = intra[d] + inter_scale[d] * inter[d]
            den = rsum[d] + inter_scale[d] * qn
            oo_ref[d, pl.ds(r0[d], TL), :] = num / jnp.maximum(jnp.abs(den), einv[d])
        return (m_carry[0], m_carry[1], c_carry[0], c_carry[1], n_carry[0], n_carry[1])

    zero11 = jnp.zeros((1, 1), F32)
    c0 = jnp.zeros((ML_QK, ML_V), F32)
    n0 = jnp.zeros((1, ML_QK), F32)
    lax.fori_loop(0, nt, tile_body, (zero11, zero11, c0, c0, n0, n0))

    for rt in range(s // TL):
        r = slice(rt * TL, (rt + 1) * TL)
        o = oo_ref[0, r, :] + oo_ref[1, r, :]
        o = o * lax.rsqrt(jnp.mean(o * o, axis=-1, keepdims=True) + EPS) * nw_ref[...]
        o_ref[r, :] = (o * _sigmoid(og_ref[r, :].astype(F32))).astype(o_ref.dtype)


def _mlstm(proj, gc, gr, norm_w, batch, seq):
    t = proj.shape[0]
    nt = seq // TL
    qb, kb, vb, ob = OFF_ML_Q // ML_QK, OFF_ML_K // ML_QK, OFF_ML_V // ML_V, OFF_ML_O // ML_V
    return pl.pallas_call(
        _ml_kernel,
        out_shape=jax.ShapeDtypeStruct((t, ML_WIDTH), BF16),
        grid=(batch, ML_HEADS),
        in_specs=[pl.BlockSpec((seq, ML_QK), lambda b, h: (b, qb + h)),
                  pl.BlockSpec((seq, ML_QK), lambda b, h: (b, kb + h)),
                  pl.BlockSpec((seq, ML_V), lambda b, h: (b, vb + h)),
                  pl.BlockSpec((seq, ML_V), lambda b, h: (b, ob + h)),
                  pl.BlockSpec((1, seq, LANES), lambda b, h: (DN_HEADS + h, b, 0)),
                  pl.BlockSpec((nt, 8, TL), lambda b, h: (b, LANES // 8 + h, 0)),
                  pl.BlockSpec((1, ML_V), lambda b, h: (0, h))],
        out_specs=pl.BlockSpec((seq, ML_V), lambda b, h: (b, h)),
        scratch_shapes=[pltpu.VMEM((2, seq, ML_V), F32)],
        compiler_params=_cp(("arbitrary", "arbitrary")),
        name="mlstm",
    )(proj, proj, proj, proj, gc, gr, norm_w.reshape(1, ML_WIDTH))


def _sc_kernel(b_ref, c_ref, x_ref, w_ref, o_ref, cx_ref, pad_ref):
    cx_ref[...] = c_ref[...].astype(F32) * x_ref[...].astype(F32)
    y = _conv_centred(cx_ref, w_ref, pad_ref, SC_CONV)
    o_ref[...] = (b_ref[...].astype(F32) * y).astype(o_ref.dtype)


def _shortconv(proj, conv_w, batch, seq, wc=256):
    t = proj.shape[0]
    bb, cb, xb = OFF_SC_B // wc, OFF_SC_C // wc, OFF_SC_X // wc
    col = lambda off: pl.BlockSpec((seq, wc), lambda b, j, off=off: (b, off + j))
    return pl.pallas_call(
        _sc_kernel,
        out_shape=jax.ShapeDtypeStruct((t, SC_WIDTH), BF16),
        grid=(batch, SC_WIDTH // wc),
        in_specs=[col(bb), col(cb), col(xb), pl.BlockSpec((SC_CONV, wc), lambda b, j: (0, j))],
        out_specs=pl.BlockSpec((seq, wc), lambda b, j: (b, j)),
        scratch_shapes=[pltpu.VMEM((seq, wc), F32), pltpu.VMEM((seq + 16, wc), F32)],
        compiler_params=_cp(("arbitrary", "arbitrary")),
        name="shortconv",
    )(proj, proj, proj, conv_w)


def _pack_gate_cols(dn_b, dn_a, ml_i, ml_f):
    r = dn_b.shape[0]

    def grp(lo, hi, nh):
        a = jnp.stack([lo[:, :nh], lo[:, nh:], hi[:, :nh], hi[:, nh:]], axis=-1)
        a = jnp.concatenate([a, jnp.zeros((r, nh, 4), a.dtype)], axis=-1).reshape(r, nh * 8)
        return jnp.concatenate([a, jnp.zeros((r, LANES - nh * 8), a.dtype)], axis=-1)

    return jnp.concatenate([grp(dn_b, dn_a, DN_HEADS), grp(ml_i, ml_f, ML_HEADS)], axis=-1)


def _gate_params(dn_a_log, dn_dt_bias, ml_igate_b, ml_fgate_b):
    z_dn = jnp.zeros((1, 2 * DN_HEADS), F32)
    z_ml = jnp.zeros((1, 2 * ML_HEADS), F32)
    flat = lambda a: a.astype(F32).reshape(1, -1)
    rows = [_pack_gate_cols(z_dn, flat(dn_a_log), z_ml, z_ml),
            _pack_gate_cols(z_dn, flat(dn_dt_bias), z_ml, z_ml),
            _pack_gate_cols(z_dn, z_dn, flat(ml_igate_b), z_ml),
            _pack_gate_cols(z_dn, z_dn, z_ml, flat(ml_fgate_b))]
    return jnp.concatenate(rows + [jnp.zeros((4, 2 * LANES), F32)], axis=0)


def _outproj_kernel(a1_ref, a2_ref, a3_ref, w1_ref, w2_ref, w3_ref, x_ref, o_ref):
    acc = jnp.dot(a1_ref[...], w1_ref[...], preferred_element_type=F32)
    acc += jnp.dot(a2_ref[...], w2_ref[...], preferred_element_type=F32)
    acc += jnp.dot(a3_ref[...], w3_ref[...], preferred_element_type=F32)
    o_ref[...] = x_ref[...] + acc


def _outproj(dn, ml, sc, w_out, x, tm=1024, tn=512):
    t, d = x.shape
    k1, k2, k3 = dn.shape[1], ml.shape[1], sc.shape[1]
    assert k1 == k2 and (k1 + k2) % k3 == 0
    return pl.pallas_call(
        _outproj_kernel,
        out_shape=jax.ShapeDtypeStruct((t, d), F32),
        grid=(t // tm, d // tn),
        in_specs=[pl.BlockSpec((tm, k1), lambda i, j: (i, 0)),
                  pl.BlockSpec((tm, k2), lambda i, j: (i, 0)),
                  pl.BlockSpec((tm, k3), lambda i, j: (i, 0)),
                  pl.BlockSpec((k1, tn), lambda i, j: (0, j)),
                  pl.BlockSpec((k2, tn), lambda i, j: (1, j)),
                  pl.BlockSpec((k3, tn), lambda i, j: ((k1 + k2) // k3, j)),
                  pl.BlockSpec((tm, tn), lambda i, j: (i, j))],
        out_specs=pl.BlockSpec((tm, tn), lambda i, j: (i, j)),
        compiler_params=_cp(("arbitrary", "arbitrary")),
        name="outproj",
    )(dn, ml, sc, w_out, w_out, w_out, x)


ROUTE_TM = 256
EXP_ROW0 = 32
HALF_D = D_MODEL // 2


def _router_kernel(x_ref, nw_ref, wr_ref, br_ref, hp_ref, meta_ref, wc_ref, cnt_ref, base_ref):
    tm = x_ref.shape[0]

    @pl.when(pl.program_id(0) == 0)
    def _():
        base_ref[...] = jnp.zeros(base_ref.shape, F32)

    x = x_ref[...]
    h = x * lax.rsqrt(jnp.mean(x * x, axis=-1, keepdims=True) + EPS) * nw_ref[...]
    bits = pltpu.bitcast(h.astype(BF16).astype(F32), U32)
    hp_ref[...] = (bits[:, :HALF_D] >> 16) | bits[:, HALF_D:]

    lt = (_mm_f32(h, wr_ref[...]) + br_ref[...]).T
    row8 = lax.broadcasted_iota(I32, (8, tm), 0)
    gl = jnp.where(row8 < N_GROUPS, lt[0:8, :], NEG)
    gex = jnp.exp(gl - jnp.max(gl, axis=0, keepdims=True))
    gp = gex / jnp.sum(gex, axis=0, keepdims=True)
    g_w = jnp.max(gp, axis=0, keepdims=True)
    g_idx = jnp.min(jnp.where(gp == g_w, row8, 8), axis=0, keepdims=True)
    el = lt[EXP_ROW0 + 24:EXP_ROW0 + 32, :]
    for g in (2, 1, 0):
        el = jnp.where(g_idx == g, lt[EXP_ROW0 + 8 * g:EXP_ROW0 + 8 * g + 8, :], el)
    ee = jnp.exp(el - jnp.max(el, axis=0, keepdims=True))
    p = ee / jnp.sum(ee, axis=0, keepdims=True)
    p1 = jnp.max(p, axis=0, keepdims=True)
    i1 = jnp.min(jnp.where(p == p1, row8, 8), axis=0, keepdims=True)
    pm = jnp.where(row8 == i1, -1.0, p)
    p2 = jnp.max(pm, axis=0, keepdims=True)
    i2 = jnp.min(jnp.where(pm == p2, row8, 8), axis=0, keepdims=True)
    den = p1 + p2
    w1 = g_w * p1 / den
    w2 = g_w * p2 / den
    e1 = g_idx * EXPERTS_PER_GROUP + i1
    e2 = g_idx * EXPERTS_PER_GROUP + i2

    rowe = lax.broadcasted_iota(I32, (N_EXPERTS, tm), 0)
    oh1 = jnp.where(rowe == e1, 1.0, 0.0)
    oh2 = jnp.where(rowe == e2, 1.0, 0.0)
    oh = (oh1 + oh2).astype(BF16)
    ri = lax.broadcasted_iota(I32, (tm, tm), 0)
    ci = lax.broadcasted_iota(I32, (tm, tm), 1)
    before = jnp.where(ri < ci, 1.0, 0.0).astype(BF16)
    ones = jnp.ones((tm, tm), BF16)
    tot = jnp.dot(oh, before, preferred_element_type=F32) + base_ref[...]
    r1 = jnp.sum(oh1 * tot, axis=0, keepdims=True)
    r2 = jnp.sum(oh2 * tot, axis=0, keepdims=True)
    base_ref[...] = base_ref[...] + jnp.dot(oh, ones, preferred_element_type=F32)
    cnt_ref[...] = base_ref[...]

    meta_ref[...] = jnp.concatenate(
        [e1, e2, r1.astype(I32), r2.astype(I32), jnp.zeros((4, tm), I32)], axis=0)
    wrow = lax.broadcasted_iota(I32, (LANES, tm), 0)
    wc_ref[...] = jnp.where(wrow == 0, w1, jnp.where(wrow == 1, w2, 0.0)).T


def _router(x, norm_w, wr, br):
    t, d = x.shape
    tm = ROUTE_TM
    return pl.pallas_call(
        _router_kernel,
        out_shape=(jax.ShapeDtypeStruct((t, HALF_D), U32),
                   jax.ShapeDtypeStruct((8, t), I32),
                   jax.ShapeDtypeStruct((t, LANES), F32),
                   jax.ShapeDtypeStruct((N_EXPERTS, tm), F32)),
        grid=(t // tm,),
        in_specs=[pl.BlockSpec((tm, d), lambda i: (i, 0)),
                  pl.BlockSpec((1, d), lambda i: (0, 0)),
                  pl.BlockSpec((d, LANES), lambda i: (0, 0)),
                  pl.BlockSpec((1, LANES), lambda i: (0, 0))],
        out_specs=(pl.BlockSpec((tm, HALF_D), lambda i: (i, 0)),
                   pl.BlockSpec((8, tm), lambda i: (0, i)),
                   pl.BlockSpec((tm, LANES), lambda i: (i, 0)),
                   pl.BlockSpec((N_EXPERTS, tm), lambda i: (0, 0))),
        scratch_shapes=[pltpu.VMEM((N_EXPERTS, tm), F32)],
        compiler_params=_cp(("arbitrary",)),
        name="router",
    )(x, norm_w.reshape(1, d), wr, br)


SCAT_TM = 256


def _scatter_kernel(zs_ref, ze_ref, nu_ref, s1_ref, s2_ref, hp_ref, xs_ref, zero_ref, sem):
    i = pl.program_id(0)
    tm = hp_ref.shape[0]
    n_all = xs_ref.shape[0] // MOE_BLK

    @pl.when(i == 0)
    def _():
        zero_ref[...] = jnp.zeros(zero_ref.shape, zero_ref.dtype)

        def zfill(e, c):
            zero_row = lambda r: pltpu.make_async_copy(zero_ref.at[pl.ds(0, 1), :], xs_ref.at[pl.ds(r, 1), :], sem)

            def one(r, c2):
                zero_row(r).start()
                return c2

            def one_wait(r, c2):
                zero_row(r).wait()
                return c2

            lax.fori_loop(zs_ref[e], ze_ref[e], one, 0)
            lax.fori_loop(zs_ref[e], ze_ref[e], one_wait, 0)
            return c

        def zero_block(b):
            r0 = pl.multiple_of(b * MOE_BLK, MOE_BLK)
            return pltpu.make_async_copy(zero_ref, xs_ref.at[pl.ds(r0, MOE_BLK), :], sem)

        def tfill(b, c):
            zero_block(b).start()
            return c

        def twait(b, c):
            zero_block(b).wait()
            return c

        lax.fori_loop(0, N_EXPERTS, zfill, 0)
        lax.fori_loop(nu_ref[0], n_all, tfill, 0)
        lax.fori_loop(nu_ref[0], n_all, twait, 0)

    def row_copies(t):
        src = hp_ref.at[pl.ds(t, 1), :]
        return (pltpu.make_async_copy(src, xs_ref.at[pl.ds(s1_ref[0, 0, t], 1), :], sem),
                pltpu.make_async_copy(src, xs_ref.at[pl.ds(s2_ref[0, 0, t], 1), :], sem))

    def issue(t, c):
        for cp in row_copies(t):
            cp.start()
        return c

    def drain(t, c):
        for cp in row_copies(t):
            cp.wait()
        return c

    lax.fori_loop(0, tm, issue, 0, unroll=8)
    lax.fori_loop(0, tm, drain, 0, unroll=8)


def _scatter_rows(zero_start, zero_end, n_used, slot1, slot2, hp, n_rows):
    t, w = hp.shape
    tm = SCAT_TM
    s3 = lambda a: a.reshape(t // tm, 1, tm)
    grid_spec = pltpu.PrefetchScalarGridSpec(
        num_scalar_prefetch=3,
        grid=(t // tm,),
        in_specs=[pl.BlockSpec((1, 1, tm), lambda i, zs, ze, nu: (i, 0, 0), memory_space=pltpu.SMEM),
                  pl.BlockSpec((1, 1, tm), lambda i, zs, ze, nu: (i, 0, 0), memory_space=pltpu.SMEM),
                  pl.BlockSpec((tm, w), lambda i, zs, ze, nu: (i, 0))],
        out_specs=pl.BlockSpec(memory_space=pl.ANY),
        scratch_shapes=[pltpu.VMEM((MOE_BLK, w), U32), pltpu.SemaphoreType.DMA(())],
    )
    return pl.pallas_call(
        _scatter_kernel,
        out_shape=jax.ShapeDtypeStruct((n_rows, w), U32),
        grid_spec=grid_spec,
        compiler_params=_cp(("arbitrary",)),
        name="scatter_rows",
    )(zero_start, zero_end, n_used, s3(slot1), s3(slot2), hp)


FF_TN = 256


def _unpack_rows(x_ref, xs_ref):
    u = x_ref[...]
    xs_ref[:, :HALF_D] = pltpu.bitcast(u << 16, F32).astype(BF16)
    xs_ref[:, HALF_D:] = pltpu.bitcast(u & jnp.uint32(0xFFFF0000), F32).astype(BF16)


def _ffn_up_kernel(be_ref, nu_ref, x_ref, wg_ref, wu_ref, o_ref, wgb_ref, wub_ref, xs_ref):
    i = pl.program_id(1)

    @pl.when(i < nu_ref[0])
    def _():
        @pl.when((i == 0) | (be_ref[i] != be_ref[jnp.maximum(i - 1, 0)]))
        def _():
            wgb_ref[...] = wg_ref[0, 0].astype(BF16)
            wub_ref[...] = wu_ref[0, 0].astype(BF16)

        _unpack_rows(x_ref, xs_ref)
        x = xs_ref[...]
        g = jnp.dot(x, wgb_ref[...], preferred_element_type=F32)
        u = jnp.dot(x, wub_ref[...], preferred_element_type=F32)
        o_ref[...] = (g * _sigmoid(g) * u).astype(o_ref.dtype)

    @pl.when(i >= nu_ref[0])
    def _():
        o_ref[...] = jnp.zeros(o_ref.shape, o_ref.dtype)


def _ffn_up(block_exp, n_used, xs, w_gate, w_up, layer, n_blocks):
    d, ff = w_gate.shape[2], w_gate.shape[3]
    blk = lambda i, nu: jnp.minimum(i, nu[0] - 1)
    grid_spec = pltpu.PrefetchScalarGridSpec(
        num_scalar_prefetch=2,
        grid=(ff // FF_TN, n_blocks),
        in_specs=[pl.BlockSpec((MOE_BLK, HALF_D), lambda j, i, be, nu: (blk(i, nu), 0)),
                  pl.BlockSpec((1, 1, d, FF_TN), lambda j, i, be, nu: (layer, be[blk(i, nu)], 0, j)),
                  pl.BlockSpec((1, 1, d, FF_TN), lambda j, i, be, nu: (layer, be[blk(i, nu)], 0, j))],
        out_specs=pl.BlockSpec((MOE_BLK, FF_TN), lambda j, i, be, nu: (i, j)),
        scratch_shapes=[pltpu.VMEM((d, FF_TN), BF16), pltpu.VMEM((d, FF_TN), BF16),
                        pltpu.VMEM((MOE_BLK, d), BF16)],
    )
    return pl.pallas_call(
        _ffn_up_kernel,
        out_shape=jax.ShapeDtypeStruct((n_blocks * MOE_BLK, ff), BF16),
        grid_spec=grid_spec,
        compiler_params=_cp(("arbitrary", "arbitrary")),
        name="ffn_up",
    )(block_exp, n_used, xs, w_gate, w_up)


def _ffn_down_kernel(be_ref, nu_ref, h_ref, wd_ref, y_ref, wdb_ref):
    i = pl.program_id(0)

    @pl.when(i < nu_ref[0])
    def _():
        @pl.when((i == 0) | (be_ref[i] != be_ref[jnp.maximum(i - 1, 0)]))
        def _():
            wdb_ref[...] = wd_ref[0, 0].astype(BF16)

        y = jnp.dot(h_ref[...], wdb_ref[...], preferred_element_type=F32)
        bits = pltpu.bitcast(y.astype(BF16).astype(F32), U32)
        half = bits.shape[1] // 2
        y_ref[...] = (bits[:, :half] >> 16) | bits[:, half:]

    @pl.when(i >= nu_ref[0])
    def _():
        y_ref[...] = jnp.zeros(y_ref.shape, y_ref.dtype)


def _ffn_down(block_exp, n_used, hmid, w_down, layer, n_blocks):
    ff, d = w_down.shape[2], w_down.shape[3]
    blk = lambda i, nu: jnp.minimum(i, nu[0] - 1)
    grid_spec = pltpu.PrefetchScalarGridSpec(
        num_scalar_prefetch=2,
        grid=(n_blocks,),
        in_specs=[pl.BlockSpec((MOE_BLK, ff), lambda i, be, nu: (blk(i, nu), 0)),
                  pl.BlockSpec((1, 1, ff, d), lambda i, be, nu: (layer, be[blk(i, nu)], 0, 0))],
        out_specs=pl.BlockSpec((MOE_BLK, d // 2), lambda i, be, nu: (i, 0)),
        scratch_shapes=[pltpu.VMEM((ff, d), BF16)],
    )
    return pl.pallas_call(
        _ffn_down_kernel,
        out_shape=jax.ShapeDtypeStruct((n_blocks * MOE_BLK, d // 2), U32),
        grid_spec=grid_spec,
        compiler_params=_cp(("arbitrary",)),
        name="ffn_down",
    )(block_exp, n_used, hmid, w_down)


COMB_TM = 256


def _combine_kernel(s1_ref, s2_ref, x_ref, wc_ref, fw_ref, y_ref, o_ref, ya_ref, yb_ref, sem, *, final):
    tm = x_ref.shape[0]

    def row_copies(t):
        return (pltpu.make_async_copy(y_ref.at[pl.ds(s1_ref[0, 0, t], 1), :], ya_ref.at[pl.ds(t, 1), :], sem),
                pltpu.make_async_copy(y_ref.at[pl.ds(s2_ref[0, 0, t], 1), :], yb_ref.at[pl.ds(t, 1), :], sem))

    def issue(t, c):
        for cp in row_copies(t):
            cp.start()
        return c

    def drain(t, c):
        for cp in row_copies(t):
            cp.wait()
        return c

    lax.fori_loop(0, tm, issue, 0, unroll=8)
    lax.fori_loop(0, tm, drain, 0, unroll=8)
    w1, w2 = wc_ref[:, 0:1], wc_ref[:, 1:2]
    ua, ub = ya_ref[...], yb_ref[...]
    hi = jnp.uint32(0xFFFF0000)
    lo_sum = w1 * pltpu.bitcast(ua << 16, F32) + w2 * pltpu.bitcast(ub << 16, F32)
    hi_sum = w1 * pltpu.bitcast(ua & hi, F32) + w2 * pltpu.bitcast(ub & hi, F32)
    o = x_ref[...] + jnp.concatenate([lo_sum, hi_sum], axis=1)
    if final:
        o = o * lax.rsqrt(jnp.mean(o * o, axis=-1, keepdims=True) + EPS) * fw_ref[...]
    o_ref[...] = o


def _combine(slot1, slot2, x, wcol, y, final_w, final):
    t, d = x.shape
    tm = COMB_TM
    s3 = lambda a: a.reshape(t // tm, 1, tm)
    return pl.pallas_call(
        functools.partial(_combine_kernel, final=final),
        out_shape=jax.ShapeDtypeStruct((t, d), F32),
        grid=(t // tm,),
        in_specs=[pl.BlockSpec((1, 1, tm), lambda i: (i, 0, 0), memory_space=pltpu.SMEM),
                  pl.BlockSpec((1, 1, tm), lambda i: (i, 0, 0), memory_space=pltpu.SMEM),
                  pl.BlockSpec((tm, d), lambda i: (i, 0)),
                  pl.BlockSpec((tm, LANES), lambda i: (i, 0)),
                  pl.BlockSpec((1, d), lambda i: (0, 0)),
                  pl.BlockSpec(memory_space=pl.ANY)],
        out_specs=pl.BlockSpec((tm, d), lambda i: (i, 0)),
        scratch_shapes=[pltpu.VMEM((tm, d // 2), U32), pltpu.VMEM((tm, d // 2), U32),
                        pltpu.SemaphoreType.DMA(())],
        compiler_params=_cp(("arbitrary",)),
        name="combine",
    )(s3(slot1), s3(slot2), x, wcol, final_w.reshape(1, d), y)


def _mixer_layer(xf, batch, seq, norm_w, w_in, dn_conv_w, dn_a_log, dn_dt_bias, dn_norm_w, ml_igate_b,
                 ml_fgate_b, ml_norm_w, sc_conv_w, w_out):
    c0 = 3 * DN_WIDTH + DN_WIDTH
    c1 = c0 + 4 * DN_HEADS
    c2 = c1 + 2 * ML_HEADS * ML_QK + 2 * ML_WIDTH
    c3 = c2 + 4 * ML_HEADS
    w_main = jnp.concatenate([w_in[:, :c0], w_in[:, c1:c2], w_in[:, c3:]], axis=1).astype(BF16)
    w_gate = _pack_gate_cols(w_in[:, c0:c0 + 2 * DN_HEADS], w_in[:, c0 + 2 * DN_HEADS:c1],
                             w_in[:, c2:c2 + 2 * ML_HEADS], w_in[:, c2 + 2 * ML_HEADS:c3]).astype(BF16)
    h = _rmsnorm(xf, norm_w, BF16)
    proj, graw = _inproj(h, w_main, w_gate)
    gc, gr = _gate_prep(graw, _gate_params(dn_a_log, dn_dt_bias, ml_igate_b, ml_fgate_b))
    dn = _deltanet(proj, dn_conv_w, gc, gr, dn_norm_w, batch, seq)
    ml = _mlstm(proj, gc, gr, ml_norm_w, batch, seq)
    sc = _shortconv(proj, sc_conv_w, batch, seq)
    return _outproj(dn, ml, sc, w_out.astype(BF16), xf)


def _moe_layer(xf, norm_w, rg_w, rg_b, re_w, re_b, w_gate, w_up, w_down, layer, final_w, final):
    t, d = xf.shape
    def lanes(g, e):
        r = g.shape[0]
        return jnp.concatenate([g, jnp.zeros((r, EXP_ROW0 - N_GROUPS), F32), e,
                                jnp.zeros((r, LANES - EXP_ROW0 - N_EXPERTS), F32)], axis=1)

    wr = lanes(rg_w, re_w)
    br = lanes(rg_b.reshape(1, -1), re_b.reshape(1, -1))
    hp, meta, wcol, cnt = _router(xf, norm_w, wr, br)
    counts = cnt[:, 0].astype(I32)
    padded = (counts + MOE_BLK - 1) // MOE_BLK * MOE_BLK
    pad_end = jnp.cumsum(padded)
    pad_start = pad_end - padded
    n_blocks = -(-(2 * t + N_EXPERTS * (MOE_BLK - 1)) // MOE_BLK)
    n_used = (pad_end[-1:] // MOE_BLK).astype(I32)
    blk_row = jnp.arange(n_blocks, dtype=I32)[:, None] * MOE_BLK
    block_exp = jnp.minimum(jnp.sum((pad_end[None, :] <= blk_row).astype(I32), axis=1), N_EXPERTS - 1)
    slot1 = pad_start[meta[0]] + meta[2]
    slot2 = pad_start[meta[1]] + meta[3]
    xs = _scatter_rows(pad_start + counts, pad_end, n_used, slot1, slot2, hp, n_blocks * MOE_BLK)
    hmid = _ffn_up(block_exp, n_used, xs, w_gate, w_up, layer, n_blocks)
    y = _ffn_down(block_exp, n_used, hmid, w_down, layer, n_blocks)
    return _combine(slot1, slot2, xf, wcol, y, final_w, final)


def kernel(x, norm_mix_w, w_in, dn_conv_w, dn_a_log, dn_dt_bias, dn_norm_w, ml_igate_b, ml_fgate_b, ml_norm_w,
           sc_conv_w, w_out, norm_ffn_w, router_group_w, router_group_b, router_expert_w, router_expert_b,
           expert_w_gate, expert_w_up, expert_w_down, final_norm_w):
    batch, seq, d = x.shape
    depth = w_in.shape[0]
    xf = x.reshape(batch * seq, d)
    for l in range(depth):
        xf = _mixer_layer(xf, batch, seq, norm_mix_w[l], w_in[l], dn_conv_w[l], dn_a_log[l], dn_dt_bias[l],
                          dn_norm_w[l], ml_igate_b[l], ml_fgate_b[l], ml_norm_w[l], sc_conv_w[l], w_out[l])
        xf = _moe_layer(xf, norm_ffn_w[l], router_group_w[l], router_group_b[l], router_expert_w[l],
                        router_expert_b[l], expert_w_gate, expert_w_up, expert_w_down, l,
                        final_norm_w, l == depth - 1)
    return xf.reshape(batch, seq, d)
```

```python
import functools

import numpy as np
import jax
import jax.numpy as jnp
from jax import lax
from jax.experimental import pallas as pl
from jax.experimental.pallas import tpu as pltpu

F32 = jnp.float32
BF16 = jnp.bfloat16
I32 = jnp.int32
U32 = jnp.uint32

D_MODEL = 4096
DN_HEADS = 12
DN_DIM = 128
DN_WIDTH = DN_HEADS * DN_DIM
DN_CONV = 5
ML_HEADS = 6
ML_QK = 128
ML_V = 256
ML_WIDTH = ML_HEADS * ML_V
SC_WIDTH = 1024
SC_CONV = 3
N_GROUPS = 4
EXPERTS_PER_GROUP = 8
N_EXPERTS = N_GROUPS * EXPERTS_PER_GROUP
D_FF = 768
EPS = 1e-6

LANES = 128
SUBLANES = 8
VMEM_LIMIT = 52 * 1024 * 1024

CH = 64
TL = 256
CPT = TL // CH
DN_TPI = 1
DN_HPS = 2
N_HEADS_ALL = DN_HEADS + ML_HEADS
MOE_BLK = 256
NEG = -1e30

OFF_DN_Q, OFF_DN_K, OFF_DN_V, OFF_DN_Z = 0, 1536, 3072, 4608
OFF_ML_Q, OFF_ML_K, OFF_ML_V, OFF_ML_O = 6144, 6912, 7680, 9216
OFF_SC_B, OFF_SC_C, OFF_SC_X = 10752, 11776, 12800
N_MAIN = 13824


def _cp(sem, vmem=VMEM_LIMIT):
    return pltpu.CompilerParams(dimension_semantics=sem, vmem_limit_bytes=vmem)


def _mm(a, b):
    return jnp.dot(a.astype(BF16), b.astype(BF16), preferred_element_type=F32)


def _mm_nt(a, b):
    return lax.dot_general(a.astype(BF16), b.astype(BF16), (((1,), (1,)), ((), ())),
                           preferred_element_type=F32)


def _mm_f32(a, b):
    return jnp.dot(a, b, preferred_element_type=F32, precision=lax.Precision.HIGHEST)


def _sigmoid(x):
    return 1.0 / (1.0 + jnp.exp(-x))


def _softplus(x):
    return jnp.maximum(x, 0.0) + jnp.log(1.0 + jnp.exp(-jnp.abs(x)))


def _rms_kernel(x_ref, w_ref, o_ref):
    x = x_ref[...]
    ms = jnp.mean(x * x, axis=-1, keepdims=True)
    o_ref[...] = (x * lax.rsqrt(ms + EPS) * w_ref[...]).astype(o_ref.dtype)


def _rmsnorm(x2d, w, out_dtype, tm=256):
    t, d = x2d.shape
    return pl.pallas_call(
        _rms_kernel,
        out_shape=jax.ShapeDtypeStruct((t, d), out_dtype),
        grid=(t // tm,),
        in_specs=[pl.BlockSpec((tm, d), lambda i: (i, 0)), pl.BlockSpec((1, d), lambda i: (0, 0))],
        out_specs=pl.BlockSpec((tm, d), lambda i: (i, 0)),
        compiler_params=_cp(("arbitrary",)),
        name="rmsnorm",
    )(x2d, w.reshape(1, d))


def _inproj_kernel(a_ref, w_ref, wg_ref, o_ref, g_ref):
    a = a_ref[...]
    o_ref[...] = jnp.dot(a, w_ref[...], preferred_element_type=F32).astype(o_ref.dtype)

    @pl.when(pl.program_id(1) == 0)
    def _():
        g_ref[...] = jnp.dot(a, wg_ref[...], preferred_element_type=F32)


def _inproj(h, w_main, w_gate, tm=1024, tn=512):
    t, d = h.shape
    n = w_main.shape[1]
    ng = w_gate.shape[1]
    return pl.pallas_call(
        _inproj_kernel,
        out_shape=(jax.ShapeDtypeStruct((t, n), BF16), jax.ShapeDtypeStruct((t, ng), F32)),
        grid=(t // tm, n // tn),
        in_specs=[pl.BlockSpec((tm, d), lambda i, j: (i, 0)),
                  pl.BlockSpec((d, tn), lambda i, j: (0, j)),
                  pl.BlockSpec((d, ng), lambda i, j: (0, 0))],
        out_specs=(pl.BlockSpec((tm, tn), lambda i, j: (i, j)),
                   pl.BlockSpec((tm, ng), lambda i, j: (i, 0))),
        compiler_params=_cp(("arbitrary", "arbitrary")),
        name="inproj",
    )(h, w_main, w_gate)


def _gate_kernel(raw_ref, p_ref, gc_ref, gr_ref):
    x = raw_ref[...]
    lane = lax.broadcasted_iota(I32, x.shape, 1)
    chn = lane & 7
    is_ml = lane >= LANES
    a_log, dt_b, ig_b, fg_b = p_ref[0:1, :], p_ref[1:2, :], p_ref[2:3, :], p_ref[3:4, :]
    beta = _sigmoid(x)
    g = -jnp.exp(a_log) * _softplus(x + dt_b)
    ig = x + ig_b
    lf = -_softplus(-(x + fg_b))
    lo = chn < 2
    val = jnp.where(is_ml, jnp.where(lo, ig, lf), jnp.where(lo, beta, g))
    val = jnp.where(chn < 4, val, 0.0)
    ri = lax.broadcasted_iota(I32, (TL, TL), 0)
    ci = lax.broadcasted_iota(I32, (TL, TL), 1)
    same = (ri >> 6) == (ci >> 6)
    m_f = jnp.where(same & (ri >= ci), 1.0, 0.0)
    m_b = jnp.where(same & (ri <= ci), 1.0, 0.0)
    m_t = jnp.where(same, 1.0, 0.0)
    cum_f = _mm_f32(m_f, val)
    cum_b = _mm_f32(m_b, val)
    tot = pltpu.roll(_mm_f32(m_t, val), 2, axis=1)
    out = jnp.where(chn == 2, cum_f, jnp.where(chn == 3, cum_b, jnp.where((chn == 4) | (chn == 5), tot, val)))
    gr_ref[0] = out.T
    for hh in range(N_HEADS_ALL):
        grp, h = (0, hh) if hh < DN_HEADS else (1, hh - DN_HEADS)
        v = out[:, grp * LANES:(grp + 1) * LANES]
        sh = (LANES - 8 * h) % LANES
        gc_ref[hh] = pltpu.roll(v, sh, axis=1) if sh else v


def _gate_prep(raw, params):
    t = raw.shape[0]
    nt = t // TL
    return pl.pallas_call(
        _gate_kernel,
        out_shape=(jax.ShapeDtypeStruct((N_HEADS_ALL, t, LANES), F32),
                   jax.ShapeDtypeStruct((nt, 2 * LANES, TL), F32)),
        grid=(nt,),
        in_specs=[pl.BlockSpec((TL, 2 * LANES), lambda i: (i, 0)),
                  pl.BlockSpec((8, 2 * LANES), lambda i: (0, 0))],
        out_specs=(pl.BlockSpec((N_HEADS_ALL, TL, LANES), lambda i: (0, i, 0)),
                   pl.BlockSpec((1, 2 * LANES, TL), lambda i: (i, 0, 0))),
        compiler_params=_cp(("arbitrary",)),
        name="gate_prep",
    )(raw, params)


def _conv_centred(x_ref, w_ref, pad_ref, width):
    s = x_ref.shape[0]
    half = width // 2
    pad_ref[0:8, :] = jnp.zeros((8, pad_ref.shape[1]), F32)
    pad_ref[s + 8:s + 16, :] = jnp.zeros((8, pad_ref.shape[1]), F32)
    pad_ref[8:s + 8, :] = x_ref[...].astype(F32)
    acc = None
    for i in range(width):
        term = w_ref[i:i + 1, :] * pad_ref[8 - half + i:8 - half + i + s, :]
        acc = term if acc is None else acc + term
    return acc


def _conv_centred_tiles(x_ref, w_ref, pad_ref, width, cols):
    s = x_ref.shape[0]
    half = width // 2
    pad_ref[0:8, :] = jnp.zeros((8, pad_ref.shape[1]), F32)
    pad_ref[s + 8:s + 16, :] = jnp.zeros((8, pad_ref.shape[1]), F32)
    pad_ref[8:s + 8, :] = x_ref[:, cols].astype(F32)
    for rt in range(s // TL):
        base = 8 - half + rt * TL
        acc = None
        for i in range(width):
            term = w_ref[i:i + 1, cols] * pad_ref[base + i:base + i + TL, :]
            acc = term if acc is None else acc + term
        yield slice(rt * TL, (rt + 1) * TL), acc


def _tile_masks():
    ri = lax.broadcasted_iota(I32, (TL, TL), 0)
    ci = lax.broadcasted_iota(I32, (TL, TL), 1)
    return ri, ci


def _unit_tri_inverse(mats, same16, same32, eye):
    each = lambda f, *ls: [f(*xs) for xs in zip(*ls)]
    a16 = each(lambda a: jnp.where(same16, a, 0.0).astype(BF16), mats)
    b1 = each(lambda x: _mm(x, x).astype(BF16), a16)
    p = each(lambda x: eye - x.astype(F32), a16)
    b2 = each(lambda x: _mm(x, x).astype(BF16), b1)
    p = each(lambda x, b: x + _mm(x, b), p, b1)
    b3 = each(lambda x: _mm(x, x).astype(BF16), b2)
    p = each(lambda x, b: x + _mm(x, b), p, b2)
    p = each(lambda x, b: x + _mm(x, b), p, b3)
    off32 = same32 & jnp.logical_not(same16)
    a32 = each(lambda a: jnp.where(off32, a, 0.0).astype(BF16), mats)
    pb = each(lambda x: x.astype(BF16), p)
    t = each(_mm, pb, a32)
    p = each(lambda x, y, z: x - _mm(y, z), p, t, pb)
    a64 = each(lambda a: jnp.where(same32, 0.0, a).astype(BF16), mats)
    pb = each(lambda x: x.astype(BF16), p)
    t = each(_mm, pb, a64)
    p = each(lambda x, y, z: x - _mm(y, z), p, t, pb)
    return p


def _dn_prep_kernel(x_ref, w_ref, o_ref, pad_ref):
    j = pl.program_id(1)
    n_qk = 2 * DN_WIDTH // x_ref.shape[1]
    qscale = jnp.where(j < n_qk // 2, DN_DIM ** -0.5, 1.0)

    def body(normalise):
        for h in range(x_ref.shape[1] // DN_DIM):
            sl = slice(h * DN_DIM, (h + 1) * DN_DIM)
            for r, y in _conv_centred_tiles(x_ref, w_ref, pad_ref, DN_CONV, sl):
                y = y * _sigmoid(y)
                if normalise:
                    y = y * (lax.rsqrt(jnp.sum(y * y, axis=-1, keepdims=True) + EPS) * qscale)
                o_ref[r, sl] = y.astype(o_ref.dtype)

    pl.when(j < n_qk)(functools.partial(body, True))
    pl.when(j >= n_qk)(functools.partial(body, False))


def _dn_prep(proj, conv_w, batch, seq, wc=256):
    t = proj.shape[0]
    n = 3 * DN_WIDTH
    return pl.pallas_call(
        _dn_prep_kernel,
        out_shape=jax.ShapeDtypeStruct((t, n), BF16),
        grid=(batch, n // wc),
        in_specs=[pl.BlockSpec((seq, wc), lambda b, j: (b, j)), pl.BlockSpec((DN_CONV, wc), lambda b, j: (0, j))],
        out_specs=pl.BlockSpec((seq, wc), lambda b, j: (b, j)),
        scratch_shapes=[pltpu.VMEM((seq + 16, DN_DIM), F32)],
        compiler_params=_cp(("arbitrary", "arbitrary")),
        name="dn_prep",
    )(proj, conv_w)


def _dn_kernel(q_ref, k_ref, v_ref, z_ref, gc_ref, gr_ref, nw_ref, o_ref,
               u_ref, wq_ref, ak_ref, eg_ref, oo_ref):
    s = q_ref.shape[0]
    nt = s // TL
    nc = s // CH
    heads = range(DN_HPS)
    hl = [slice(h * DN_DIM, (h + 1) * DN_DIM) for h in heads]

    def tile_body(j, carry):
        ri, ci = _tile_masks()
        same64 = (ri >> 6) == (ci >> 6)
        same32 = (ri >> 5) == (ci >> 5)
        same16 = (ri >> 4) == (ci >> 4)
        eye = jnp.where(ri == ci, 1.0, 0.0)
        incl = [same64 & (ri >= ci), same64 & (ri <= ci)]
        strict = [same64 & (ri > ci), same64 & (ri < ci)]
        tiles = [j * DN_TPI + t for t in range(DN_TPI)]
        r0 = [pl.multiple_of(i * TL, TL) for i in tiles]
        ht = [(h, t) for h in heads for t in range(DN_TPI)]
        qn = {(h, t): q_ref[pl.ds(r0[t], TL), hl[h]].astype(F32) for h, t in ht}
        kn = {(h, t): k_ref[pl.ds(r0[t], TL), hl[h]].astype(F32) for h, t in ht}
        vv = {(h, t): v_ref[pl.ds(r0[t], TL), hl[h]].astype(F32) for h, t in ht}
        kk = {x: _mm_nt(kn[x], kn[x]) for x in ht}
        qk = {x: _mm_nt(qn[x], kn[x]) for x in ht}
        gc = {(h, t): gc_ref[h, pl.ds(r0[t], TL), :] for h, t in ht}
        gr = {(h, t): gr_ref[tiles[t]][8 * h:8 * h + 8, :] for h, t in ht}
        ch = [(h, t, d) for h, t in ht for d in (0, 1)]
        beta_c = [gc[h, t][:, d:d + 1] for h, t, d in ch]
        g_c = [gc[h, t][:, 2 + d:3 + d] for h, t, d in ch]
        gt_c = [gc[h, t][:, 4 + d:5 + d] for h, t, d in ch]
        g_r = [gr[h, t][2 + d:3 + d, :] for h, t, d in ch]
        dec = [jnp.exp(jnp.where(incl[d], g_c[n] - g_r[n], NEG)) for n, (h, t, d) in enumerate(ch)]
        a = [jnp.where(strict[d], beta_c[n] * kk[h, t] * dec[n], 0.0) for n, (h, t, d) in enumerate(ch)]
        tinv = _unit_tri_inverse(a, same16, same32, eye)
        eg_c = [jnp.exp(g) for g in g_c]
        rhs = [jnp.concatenate([beta_c[n] * vv[h, t], (beta_c[n] * eg_c[n]) * kn[h, t]], axis=1)
               for n, (h, t, d) in enumerate(ch)]
        uw = [_mm(x, y) for x, y in zip(tinv, rhs)]
        for n, (h, t, d) in enumerate(ch):
            i = tiles[t]
            u_ref[h, d, pl.ds(r0[t], TL), :] = uw[n][:, :DN_DIM]
            w = uw[n][:, DN_DIM:]
            qe = qn[h, t] * eg_c[n]
            attn = qk[h, t] * dec[n]
            kdt = (kn[h, t] * jnp.exp(gt_c[n] - g_c[n])).T
            egt = jnp.exp(gt_c[n])
            for c in range(CPT):
                rows = slice(c * CH, (c + 1) * CH)
                wq_ref[h, d, i * CPT + c] = jnp.concatenate([w[rows], qe[rows]], axis=0).astype(BF16)
                ak_ref[h, d, i * CPT + c] = jnp.concatenate([attn[rows, rows], kdt[:, rows]], axis=0).astype(BF16)
                eg_ref[h, d, i * CPT + c] = jnp.broadcast_to(egt[c * CH:c * CH + 8, :], (8, LANES))
        return carry

    lax.fori_loop(0, nt // DN_TPI, tile_body, 0)

    hd = [(h, d) for h in heads for d in (0, 1)]

    def chunk_body(c, carry):
        states = list(carry)
        ws, vnew, m2 = {}, {}, {}
        cc = [c, nc - 1 - c]
        r = [pl.multiple_of(x * CH, CH) for x in cc]
        for n, (h, d) in enumerate(hd):
            ws[n] = jnp.dot(wq_ref[h, d, cc[d]], states[n].astype(BF16), preferred_element_type=F32)
        for n, (h, d) in enumerate(hd):
            vnew[n] = u_ref[h, d, pl.ds(r[d], CH), :] - ws[n][:CH]
        for n, (h, d) in enumerate(hd):
            m2[n] = jnp.dot(ak_ref[h, d, cc[d]], vnew[n].astype(BF16), preferred_element_type=F32)
        for n, (h, d) in enumerate(hd):
            oo_ref[h, d, pl.ds(r[d], CH), :] = ws[n][CH:] + m2[n][:CH]
            states[n] = eg_ref[h, d, cc[d]][0:1, :] * states[n] + m2[n][CH:]
        return tuple(states)

    z0 = jnp.zeros((DN_DIM, DN_DIM), F32)
    lax.fori_loop(0, nc, chunk_body, tuple(z0 for _ in hd))

    for h in heads:
        for rt in range(s // TL):
            r = slice(rt * TL, (rt + 1) * TL)
            o = oo_ref[h, 0, r, :] + oo_ref[h, 1, r, :]
            o = o * lax.rsqrt(jnp.mean(o * o, axis=-1, keepdims=True) + EPS) * nw_ref[...]
            z = z_ref[r, hl[h]].astype(F32)
            o_ref[r, hl[h]] = (o * (z * _sigmoid(z))).astype(o_ref.dtype)


def _deltanet(proj, conv_w, gc, gr, norm_w, batch, seq):
    t = proj.shape[0]
    nc = seq // CH
    nt = seq // TL
    wb = DN_HPS * DN_DIM
    qkv = _dn_prep(proj, conv_w, batch, seq)
    qb, kb, vb, zb = OFF_DN_Q // wb, OFF_DN_K // wb, OFF_DN_V // wb, OFF_DN_Z // wb
    col = lambda off: pl.BlockSpec((seq, wb), lambda b, h, off=off: (b, off + h))
    return pl.pallas_call(
        _dn_kernel,
        out_shape=jax.ShapeDtypeStruct((t, DN_WIDTH), BF16),
        grid=(batch, DN_HEADS // DN_HPS),
        in_specs=[col(qb), col(kb), col(vb), col(zb),
                  pl.BlockSpec((DN_HPS, seq, LANES), lambda b, h: (h, b, 0)),
                  pl.BlockSpec((nt, 8 * DN_HPS, TL), lambda b, h: (b, h, 0)),
                  pl.BlockSpec((1, DN_DIM), lambda b, h: (0, 0))],
        out_specs=pl.BlockSpec((seq, wb), lambda b, h: (b, h)),
        scratch_shapes=[pltpu.VMEM((DN_HPS, 2, seq, DN_DIM), F32),
                        pltpu.VMEM((DN_HPS, 2, nc, 2 * CH, DN_DIM), BF16),
                        pltpu.VMEM((DN_HPS, 2, nc, CH + DN_DIM, CH), BF16),
                        pltpu.VMEM((DN_HPS, 2, nc, 8, LANES), F32),
                        pltpu.VMEM((DN_HPS, 2, seq, DN_DIM), F32)],
        compiler_params=_cp(("arbitrary", "arbitrary")),
        name="deltanet",
    )(qkv, qkv, qkv, proj, gc, gr, norm_w.reshape(1, DN_DIM))


def _ml_kernel(q_ref, k_ref, v_ref, og_ref, gc_ref, gr_ref, nw_ref, o_ref, oo_ref):
    s = q_ref.shape[0]
    nt = s // TL
    scale = ML_QK ** -0.5

    def tile_body(it, carry):
        m_carry = [carry[0], carry[1]]
        c_carry = [carry[2], carry[3]]
        n_carry = [carry[4], carry[5]]
        ri, ci = _tile_masks()
        same64 = (ri >> 6) == (ci >> 6)
        rowc = lax.broadcasted_iota(I32, (TL, 1), 0) >> 6
        colc = lax.broadcasted_iota(I32, (1, TL), 1) >> 6
        dirs = (0, 1)
        order = [list(range(CPT)), list(range(CPT - 1, -1, -1))]
        tile = [it, nt - 1 - it]
        r0 = [pl.multiple_of(i * TL, TL) for i in tile]
        qb = [q_ref[pl.ds(r, TL), :] for r in r0]
        qs = [x.astype(F32) * scale for x in qb]
        kf = [k_ref[pl.ds(r, TL), :].astype(F32) for r in r0]
        vb = [v_ref[pl.ds(r, TL), :] for r in r0]
        qk = [_mm_nt(a, b) for a, b in zip(qs, kf)]
        gc = [gc_ref[0, pl.ds(r, TL), :] for r in r0]
        gr = [gr_ref[i] for i in tile]
        ig_c = [gc[d][:, d:d + 1] for d in dirs]
        b_c = [gc[d][:, 2 + d:3 + d] for d in dirs]
        bt_c = [gc[d][:, 4 + d:5 + d] for d in dirs]
        ig_r = [gr[d][d:d + 1, :] for d in dirs]
        b_r = [gr[d][2 + d:3 + d, :] for d in dirs]
        bt_r = [gr[d][4 + d:5 + d, :] for d in dirs]
        incl = [same64 & (ri >= ci), same64 & (ri <= ci)]
        dmat = [jnp.where(incl[d], b_c[d] - b_r[d] + ig_r[d], NEG) for d in dirs]
        m_intra = [jnp.max(x, axis=1, keepdims=True) for x in dmat]
        d_end_c = [bt_c[d] - b_c[d] + ig_c[d] for d in dirs]
        d_end_r = [bt_r[d] - b_r[d] + ig_r[d] for d in dirs]
        dmax_c = [jnp.max(jnp.where(same64, x, NEG), axis=1, keepdims=True) for x in d_end_r]
        m_in_c, m_out_c, csc = [], [], []
        for d in dirs:
            m_in = jnp.zeros((TL, 1), F32)
            m_out = jnp.zeros((TL, 1), F32)
            m = m_carry[d]
            cs = {}
            for cc in order[d]:
                btot = bt_c[d][cc * CH:cc * CH + 1, :]
                dmx = dmax_c[d][cc * CH:cc * CH + 1, :]
                m_new = jnp.maximum(btot + m, dmx)
                cs[cc] = jnp.exp(btot + m - m_new)
                m_in = jnp.where(rowc == cc, m, m_in)
                m_out = jnp.where(rowc == cc, m_new, m_out)
                m = m_new
            m_carry[d] = m
            m_in_c.append(m_in)
            m_out_c.append(m_out)
            csc.append(cs)
        m_inter = [b_c[d] + m_in_c[d] for d in dirs]
        m_pos = [jnp.maximum(m_intra[d], m_inter[d]) for d in dirs]
        scores = [qk[d] * jnp.exp(dmat[d] - m_pos[d]) for d in dirs]
        inter_scale = [jnp.exp(m_inter[d] - m_pos[d]) for d in dirs]
        rsum = [jnp.sum(x, axis=1, keepdims=True) for x in scores]
        einv = [jnp.exp(-x) for x in m_pos]
        wk = [kf[d] * jnp.exp(d_end_c[d] - m_out_c[d]) for d in dirs]
        wkt = [x.T for x in wk]
        intra = [jnp.dot(scores[d].astype(BF16), vb[d], preferred_element_type=F32) for d in dirs]
        wstack = [jnp.concatenate([jnp.where(colc == c, wkt[d], 0.0) for c in range(CPT)], axis=0).astype(BF16)
                  for d in dirs]
        dcs = [jnp.dot(wstack[d], vb[d], preferred_element_type=F32) for d in dirs]
        c_at, n_row = [], []
        for d in dirs:
            cst, nst = c_carry[d], n_carry[d]
            c_start = {}
            nr = jnp.zeros((TL, ML_QK), F32)
            for cc in order[d]:
                c_start[cc] = cst
                nr = jnp.where(rowc == cc, nst, nr)
                dn = jnp.sum(wk[d][cc * CH:(cc + 1) * CH], axis=0, keepdims=True)
                cst = csc[d][cc] * cst + dcs[d][cc * ML_QK:(cc + 1) * ML_QK]
                nst = csc[d][cc] * nst + dn
            c_carry[d], n_carry[d] = cst, nst
            c_at.append(jnp.concatenate([c_start[c] for c in range(CPT)], axis=0).astype(BF16))
            n_row.append(nr)
        q4 = [jnp.concatenate([jnp.where(rowc == c, qb[d], jnp.zeros_like(qb[d])) for c in range(CPT)], axis=1)
              for d in dirs]
        inter = [jnp.dot(q4[d], c_at[d], preferred_element_type=F32) * scale for d in dirs]
        for d in dirs:
            qn = jnp.sum(qs[d] * n_row[d], axis=1, keepdims=True)
            num = intra[d] + inter_scale[d] * inter[d]
            den = rsum[d] + inter_scale[d] * qn
            oo_ref[d, pl.ds(r0[d], TL), :] = num / jnp.maximum(jnp.abs(den), einv[d])
        return (m_carry[0], m_carry[1], c_carry[0], c_carry[1], n_carry[0], n_carry[1])

    zero11 = jnp.zeros((1, 1), F32)
    c0 = jnp.zeros((ML_QK, ML_V), F32)
    n0 = jnp.zeros((1, ML_QK), F32)
    lax.fori_loop(0, nt, tile_body, (zero11, zero11, c0, c0, n0, n0))

    for rt in range(s // TL):
        r = slice(rt * TL, (rt + 1) * TL)
        o = oo_ref[0, r, :] + oo_ref[1, r, :]
        o = o * lax.rsqrt(jnp.mean(o * o, axis=-1, keepdims=True) + EPS) * nw_ref[...]
        o_ref[r, :] = (o * _sigmoid(og_ref[r, :].astype(F32))).astype(o_ref.dtype)


def _mlstm(proj, gc, gr, norm_w, batch, seq):
    t = proj.shape[0]
    nt = seq // TL
    qb, kb, vb, ob = OFF_ML_Q // ML_QK, OFF_ML_K // ML_QK, OFF_ML_V // ML_V, OFF_ML_O // ML_V
    return pl.pallas_call(
        _ml_kernel,
        out_shape=jax.ShapeDtypeStruct((t, ML_WIDTH), BF16),
        grid=(batch, ML_HEADS),
        in_specs=[pl.BlockSpec((seq, ML_QK), lambda b, h: (b, qb + h)),
                  pl.BlockSpec((seq, ML_QK), lambda b, h: (b, kb + h)),
                  pl.BlockSpec((seq, ML_V), lambda b, h: (b, vb + h)),
                  pl.BlockSpec((seq, ML_V), lambda b, h: (b, ob + h)),
                  pl.BlockSpec((1, seq, LANES), lambda b, h: (DN_HEADS + h, b, 0)),
                  pl.BlockSpec((nt, 8, TL), lambda b, h: (b, LANES // 8 + h, 0)),
                  pl.BlockSpec((1, ML_V), lambda b, h: (0, h))],
        out_specs=pl.BlockSpec((seq, ML_V), lambda b, h: (b, h)),
        scratch_shapes=[pltpu.VMEM((2, seq, ML_V), F32)],
        compiler_params=_cp(("arbitrary", "arbitrary")),
        name="mlstm",
    )(proj, proj, proj, proj, gc, gr, norm_w.reshape(1, ML_WIDTH))


def _sc_kernel(b_ref, c_ref, x_ref, w_ref, o_ref, cx_ref, pad_ref):
    cx_ref[...] = c_ref[...].astype(F32) * x_ref[...].astype(F32)
    y = _conv_centred(cx_ref, w_ref, pad_ref, SC_CONV)
    o_ref[...] = (b_ref[...].astype(F32) * y).astype(o_ref.dtype)


def _shortconv(proj, conv_w, batch, seq, wc=256):
    t = proj.shape[0]
    bb, cb, xb = OFF_SC_B // wc, OFF_SC_C // wc, OFF_SC_X // wc
    col = lambda off: pl.BlockSpec((seq, wc), lambda b, j, off=off: (b, off + j))
    return pl.pallas_call(
        _sc_kernel,
        out_shape=jax.ShapeDtypeStruct((t, SC_WIDTH), BF16),
        grid=(batch, SC_WIDTH // wc),
        in_specs=[col(bb), col(cb), col(xb), pl.BlockSpec((SC_CONV, wc), lambda b, j: (0, j))],
        out_specs=pl.BlockSpec((seq, wc), lambda b, j: (b, j)),
        scratch_shapes=[pltpu.VMEM((seq, wc), F32), pltpu.VMEM((seq + 16, wc), F32)],
        compiler_params=_cp(("arbitrary", "arbitrary")),
        name="shortconv",
    )(proj, proj, proj, conv_w)


def _pack_gate_cols(dn_b, dn_a, ml_i, ml_f):
    r = dn_b.shape[0]

    def grp(lo, hi, nh):
        a = jnp.stack([lo[:, :nh], lo[:, nh:], hi[:, :nh], hi[:, nh:]], axis=-1)
        a = jnp.concatenate([a, jnp.zeros((r, nh, 4), a.dtype)], axis=-1).reshape(r, nh * 8)
        return jnp.concatenate([a, jnp.zeros((r, LANES - nh * 8), a.dtype)], axis=-1)

    return jnp.concatenate([grp(dn_b, dn_a, DN_HEADS), grp(ml_i, ml_f, ML_HEADS)], axis=-1)


def _gate_params(dn_a_log, dn_dt_bias, ml_igate_b, ml_fgate_b):
    z_dn = jnp.zeros((1, 2 * DN_HEADS), F32)
    z_ml = jnp.zeros((1, 2 * ML_HEADS), F32)
    flat = lambda a: a.astype(F32).reshape(1, -1)
    rows = [_pack_gate_cols(z_dn, flat(dn_a_log), z_ml, z_ml),
            _pack_gate_cols(z_dn, flat(dn_dt_bias), z_ml, z_ml),
            _pack_gate_cols(z_dn, z_dn, flat(ml_igate_b), z_ml),
            _pack_gate_cols(z_dn, z_dn, z_ml, flat(ml_fgate_b))]
    return jnp.concatenate(rows + [jnp.zeros((4, 2 * LANES), F32)], axis=0)


def _outproj_kernel(a1_ref, a2_ref, a3_ref, w1_ref, w2_ref, w3_ref, x_ref, o_ref):
    acc = jnp.dot(a1_ref[...], w1_ref[...], preferred_element_type=F32)
    acc += jnp.dot(a2_ref[...], w2_ref[...], preferred_element_type=F32)
    acc += jnp.dot(a3_ref[...], w3_ref[...], preferred_element_type=F32)
    o_ref[...] = x_ref[...] + acc


def _outproj(dn, ml, sc, w_out, x, tm=1024, tn=512):
    t, d = x.shape
    k1, k2, k3 = dn.shape[1], ml.shape[1], sc.shape[1]
    assert k1 == k2 and (k1 + k2) % k3 == 0
    return pl.pallas_call(
        _outproj_kernel,
        out_shape=jax.ShapeDtypeStruct((t, d), F32),
        grid=(t // tm, d // tn),
        in_specs=[pl.BlockSpec((tm, k1), lambda i, j: (i, 0)),
                  pl.BlockSpec((tm, k2), lambda i, j: (i, 0)),
                  pl.BlockSpec((tm, k3), lambda i, j: (i, 0)),
                  pl.BlockSpec((k1, tn), lambda i, j: (0, j)),
                  pl.BlockSpec((k2, tn), lambda i, j: (1, j)),
                  pl.BlockSpec((k3, tn), lambda i, j: ((k1 + k2) // k3, j)),
                  pl.BlockSpec((tm, tn), lambda i, j: (i, j))],
        out_specs=pl.BlockSpec((tm, tn), lambda i, j: (i, j)),
        compiler_params=_cp(("arbitrary", "arbitrary")),
        name="outproj",
    )(dn, ml, sc, w_out, w_out, w_out, x)


ROUTE_TM = 256
EXP_ROW0 = 32
HALF_D = D_MODEL // 2


def _router_kernel(x_ref, nw_ref, wr_ref, br_ref, hp_ref, meta_ref, wc_ref, cnt_ref, base_ref,
                   whi_ref, wlo_ref):
    tm = x_ref.shape[0]

    @pl.when(pl.program_id(0) == 0)
    def _():
        base_ref[...] = jnp.zeros(base_ref.shape, F32)
        w = wr_ref[...]
        whi_ref[...] = w.astype(BF16)
        wlo_ref[...] = (w - whi_ref[...].astype(F32)).astype(BF16)

    x = x_ref[...]
    h = x * lax.rsqrt(jnp.mean(x * x, axis=-1, keepdims=True) + EPS) * nw_ref[...]
    bits = pltpu.bitcast(h.astype(BF16).astype(F32), U32)
    hp_ref[...] = (bits[:, :HALF_D] >> 16) | bits[:, HALF_D:]

    h_hi = h.astype(BF16)
    h_lo = (h - h_hi.astype(F32)).astype(BF16)
    logits = (jnp.dot(h_hi, whi_ref[...], preferred_element_type=F32)
              + jnp.dot(h_lo, whi_ref[...], preferred_element_type=F32)
              + jnp.dot(h_hi, wlo_ref[...], preferred_element_type=F32))
    lt = (logits + br_ref[...]).T
    row8 = lax.broadcasted_iota(I32, (8, tm), 0)
    gl = jnp.where(row8 < N_GROUPS, lt[0:8, :], NEG)
    gex = jnp.exp(gl - jnp.max(gl, axis=0, keepdims=True))
    gp = gex / jnp.sum(gex, axis=0, keepdims=True)
    g_w = jnp.max(gp, axis=0, keepdims=True)
    g_idx = jnp.min(jnp.where(gp == g_w, row8, 8), axis=0, keepdims=True)
    el = lt[EXP_ROW0 + 24:EXP_ROW0 + 32, :]
    for g in (2, 1, 0):
        el = jnp.where(g_idx == g, lt[EXP_ROW0 + 8 * g:EXP_ROW0 + 8 * g + 8, :], el)
    ee = jnp.exp(el - jnp.max(el, axis=0, keepdims=True))
    p = ee / jnp.sum(ee, axis=0, keepdims=True)
    p1 = jnp.max(p, axis=0, keepdims=True)
    i1 = jnp.min(jnp.where(p == p1, row8, 8), axis=0, keepdims=True)
    pm = jnp.where(row8 == i1, -1.0, p)
    p2 = jnp.max(pm, axis=0, keepdims=True)
    i2 = jnp.min(jnp.where(pm == p2, row8, 8), axis=0, keepdims=True)
    den = p1 + p2
    w1 = g_w * p1 / den
    w2 = g_w * p2 / den
    e1 = g_idx * EXPERTS_PER_GROUP + i1
    e2 = g_idx * EXPERTS_PER_GROUP + i2

    rowe = lax.broadcasted_iota(I32, (N_EXPERTS, tm), 0)
    oh1 = jnp.where(rowe == e1, 1.0, 0.0)
    oh2 = jnp.where(rowe == e2, 1.0, 0.0)
    oh = (oh1 + oh2).astype(BF16)
    ri = lax.broadcasted_iota(I32, (tm, tm), 0)
    ci = lax.broadcasted_iota(I32, (tm, tm), 1)
    before = jnp.where(ri < ci, 1.0, 0.0).astype(BF16)
    ones = jnp.ones((tm, tm), BF16)
    tot = jnp.dot(oh, before, preferred_element_type=F32) + base_ref[...]
    r1 = jnp.sum(oh1 * tot, axis=0, keepdims=True)
    r2 = jnp.sum(oh2 * tot, axis=0, keepdims=True)
    base_ref[...] = base_ref[...] + jnp.dot(oh, ones, preferred_element_type=F32)
    cnt_ref[...] = base_ref[...]

    meta_ref[...] = jnp.concatenate(
        [e1, e2, r1.astype(I32), r2.astype(I32), jnp.zeros((4, tm), I32)], axis=0)
    wrow = lax.broadcasted_iota(I32, (LANES, tm), 0)
    wc_ref[...] = jnp.where(wrow == 0, w1, jnp.where(wrow == 1, w2, 0.0)).T


def _router(x, norm_w, wr, br):
    t, d = x.shape
    tm = ROUTE_TM
    return pl.pallas_call(
        _router_kernel,
        out_shape=(jax.ShapeDtypeStruct((t, HALF_D), U32),
                   jax.ShapeDtypeStruct((8, t), I32),
                   jax.ShapeDtypeStruct((t, LANES), F32),
                   jax.ShapeDtypeStruct((N_EXPERTS, tm), F32)),
        grid=(t // tm,),
        in_specs=[pl.BlockSpec((tm, d), lambda i: (i, 0)),
                  pl.BlockSpec((1, d), lambda i: (0, 0)),
                  pl.BlockSpec((d, LANES), lambda i: (0, 0)),
                  pl.BlockSpec((1, LANES), lambda i: (0, 0))],
        out_specs=(pl.BlockSpec((tm, HALF_D), lambda i: (i, 0)),
                   pl.BlockSpec((8, tm), lambda i: (0, i)),
                   pl.BlockSpec((tm, LANES), lambda i: (i, 0)),
                   pl.BlockSpec((N_EXPERTS, tm), lambda i: (0, 0))),
        scratch_shapes=[pltpu.VMEM((N_EXPERTS, tm), F32),
                        pltpu.VMEM((d, LANES), BF16), pltpu.VMEM((d, LANES), BF16)],
        compiler_params=_cp(("arbitrary",)),
        name="router",
    )(x, norm_w.reshape(1, d), wr, br)


SCAT_TM = 256


def _scatter_kernel(zs_ref, ze_ref, nu_ref, s1_ref, s2_ref, hp_ref, xs_ref, zero_ref, sem):
    i = pl.program_id(0)
    tm = hp_ref.shape[0]
    n_all = xs_ref.shape[0] // MOE_BLK

    @pl.when(i == 0)
    def _():
        zero_ref[...] = jnp.zeros(zero_ref.shape, zero_ref.dtype)

        def zfill(e, c):
            zero_row = lambda r: pltpu.make_async_copy(zero_ref.at[pl.ds(0, 1), :], xs_ref.at[pl.ds(r, 1), :], sem)

            def one(r, c2):
                zero_row(r).start()
                return c2

            def one_wait(r, c2):
                zero_row(r).wait()
                return c2

            lax.fori_loop(zs_ref[e], ze_ref[e], one, 0)
            lax.fori_loop(zs_ref[e], ze_ref[e], one_wait, 0)
            return c

        def zero_block(b):
            r0 = pl.multiple_of(b * MOE_BLK, MOE_BLK)
            return pltpu.make_async_copy(zero_ref, xs_ref.at[pl.ds(r0, MOE_BLK), :], sem)

        def tfill(b, c):
            zero_block(b).start()
            return c

        def twait(b, c):
            zero_block(b).wait()
            return c

        lax.fori_loop(0, N_EXPERTS, zfill, 0)
        lax.fori_loop(nu_ref[0], n_all, tfill, 0)
        lax.fori_loop(nu_ref[0], n_all, twait, 0)

    def row_copies(t):
        src = hp_ref.at[pl.ds(t, 1), :]
        return (pltpu.make_async_copy(src, xs_ref.at[pl.ds(s1_ref[0, 0, t], 1), :], sem),
                pltpu.make_async_copy(src, xs_ref.at[pl.ds(s2_ref[0, 0, t], 1), :], sem))

    def issue(t, c):
        for cp in row_copies(t):
            cp.start()
        return c

    def drain(t, c):
        for cp in row_copies(t):
            cp.wait()
        return c

    lax.fori_loop(0, tm, issue, 0, unroll=8)
    lax.fori_loop(0, tm, drain, 0, unroll=8)


def _scatter_rows(zero_start, zero_end, n_used, slot1, slot2, hp, n_rows):
    t, w = hp.shape
    tm = SCAT_TM
    s3 = lambda a: a.reshape(t // tm, 1, tm)
    grid_spec = pltpu.PrefetchScalarGridSpec(
        num_scalar_prefetch=3,
        grid=(t // tm,),
        in_specs=[pl.BlockSpec((1, 1, tm), lambda i, zs, ze, nu: (i, 0, 0), memory_space=pltpu.SMEM),
                  pl.BlockSpec((1, 1, tm), lambda i, zs, ze, nu: (i, 0, 0), memory_space=pltpu.SMEM),
                  pl.BlockSpec((tm, w), lambda i, zs, ze, nu: (i, 0))],
        out_specs=pl.BlockSpec(memory_space=pl.ANY),
        scratch_shapes=[pltpu.VMEM((MOE_BLK, w), U32), pltpu.SemaphoreType.DMA(())],
    )
    return pl.pallas_call(
        _scatter_kernel,
        out_shape=jax.ShapeDtypeStruct((n_rows, w), U32),
        grid_spec=grid_spec,
        compiler_params=_cp(("arbitrary",)),
        name="scatter_rows",
    )(zero_start, zero_end, n_used, s3(slot1), s3(slot2), hp)


FF_TN = 256


def _unpack_rows(x_ref, xs_ref):
    u = x_ref[...]
    xs_ref[:, :HALF_D] = pltpu.bitcast(u << 16, F32).astype(BF16)
    xs_ref[:, HALF_D:] = pltpu.bitcast(u & jnp.uint32(0xFFFF0000), F32).astype(BF16)


def _ffn_up_kernel(be_ref, nu_ref, x_ref, wg_ref, wu_ref, o_ref, wgb_ref, wub_ref, xs_ref):
    i = pl.program_id(1)

    @pl.when(i < nu_ref[0])
    def _():
        @pl.when((i == 0) | (be_ref[i] != be_ref[jnp.maximum(i - 1, 0)]))
        def _():
            wgb_ref[...] = wg_ref[0, 0].astype(BF16)
            wub_ref[...] = wu_ref[0, 0].astype(BF16)

        _unpack_rows(x_ref, xs_ref)
        x = xs_ref[...]
        g = jnp.dot(x, wgb_ref[...], preferred_element_type=F32)
        u = jnp.dot(x, wub_ref[...], preferred_element_type=F32)
        o_ref[...] = (g * _sigmoid(g) * u).astype(o_ref.dtype)

    @pl.when(i >= nu_ref[0])
    def _():
        o_ref[...] = jnp.zeros(o_ref.shape, o_ref.dtype)


def _ffn_up(block_exp, n_used, xs, w_gate, w_up, layer, n_blocks):
    d, ff = w_gate.shape[2], w_gate.shape[3]
    blk = lambda i, nu: jnp.minimum(i, nu[0] - 1)
    grid_spec = pltpu.PrefetchScalarGridSpec(
        num_scalar_prefetch=2,
        grid=(ff // FF_TN, n_blocks),
        in_specs=[pl.BlockSpec((MOE_BLK, HALF_D), lambda j, i, be, nu: (blk(i, nu), 0)),
                  pl.BlockSpec((1, 1, d, FF_TN), lambda j, i, be, nu: (layer, be[blk(i, nu)], 0, j)),
                  pl.BlockSpec((1, 1, d, FF_TN), lambda j, i, be, nu: (layer, be[blk(i, nu)], 0, j))],
        out_specs=pl.BlockSpec((MOE_BLK, FF_TN), lambda j, i, be, nu: (i, j)),
        scratch_shapes=[pltpu.VMEM((d, FF_TN), BF16), pltpu.VMEM((d, FF_TN), BF16),
                        pltpu.VMEM((MOE_BLK, d), BF16)],
    )
    return pl.pallas_call(
        _ffn_up_kernel,
        out_shape=jax.ShapeDtypeStruct((n_blocks * MOE_BLK, ff), BF16),
        grid_spec=grid_spec,
        compiler_params=_cp(("arbitrary", "arbitrary")),
        name="ffn_up",
    )(block_exp, n_used, xs, w_gate, w_up)


def _ffn_down_kernel(be_ref, nu_ref, h_ref, wd_ref, y_ref, wdb_ref):
    i = pl.program_id(0)

    @pl.when(i < nu_ref[0])
    def _():
        @pl.when((i == 0) | (be_ref[i] != be_ref[jnp.maximum(i - 1, 0)]))
        def _():
            wdb_ref[...] = wd_ref[0, 0].astype(BF16)

        y = jnp.dot(h_ref[...], wdb_ref[...], preferred_element_type=F32)
        bits = pltpu.bitcast(y.astype(BF16).astype(F32), U32)
        half = bits.shape[1] // 2
        y_ref[...] = (bits[:, :half] >> 16) | bits[:, half:]

    @pl.when(i >= nu_ref[0])
    def _():
        y_ref[...] = jnp.zeros(y_ref.shape, y_ref.dtype)


def _ffn_down(block_exp, n_used, hmid, w_down, layer, n_blocks):
    ff, d = w_down.shape[2], w_down.shape[3]
    blk = lambda i, nu: jnp.minimum(i, nu[0] - 1)
    grid_spec = pltpu.PrefetchScalarGridSpec(
        num_scalar_prefetch=2,
        grid=(n_blocks,),
        in_specs=[pl.BlockSpec((MOE_BLK, ff), lambda i, be, nu: (blk(i, nu), 0)),
                  pl.BlockSpec((1, 1, ff, d), lambda i, be, nu: (layer, be[blk(i, nu)], 0, 0))],
        out_specs=pl.BlockSpec((MOE_BLK, d // 2), lambda i, be, nu: (i, 0)),
        scratch_shapes=[pltpu.VMEM((ff, d), BF16)],
    )
    return pl.pallas_call(
        _ffn_down_kernel,
        out_shape=jax.ShapeDtypeStruct((n_blocks * MOE_BLK, d // 2), U32),
        grid_spec=grid_spec,
        compiler_params=_cp(("arbitrary",)),
        name="ffn_down",
    )(block_exp, n_used, hmid, w_down)


COMB_TM = 256


def _combine_kernel(s1_ref, s2_ref, x_ref, wc_ref, fw_ref, y_ref, *rest, final):
    o_ref, h_ref = (rest[0], None) if final else rest[:2]
    ya_ref, yb_ref, sem = rest[-3:]
    tm = x_ref.shape[0]

    def row_copies(t):
        return (pltpu.make_async_copy(y_ref.at[pl.ds(s1_ref[0, 0, t], 1), :], ya_ref.at[pl.ds(t, 1), :], sem),
                pltpu.make_async_copy(y_ref.at[pl.ds(s2_ref[0, 0, t], 1), :], yb_ref.at[pl.ds(t, 1), :], sem))

    def issue(t, c):
        for cp in row_copies(t):
            cp.start()
        return c

    def drain(t, c):
        for cp in row_copies(t):
            cp.wait()
        return c

    lax.fori_loop(0, tm, issue, 0, unroll=8)
    lax.fori_loop(0, tm, drain, 0, unroll=8)
    w1, w2 = wc_ref[:, 0:1], wc_ref[:, 1:2]
    ua, ub = ya_ref[...], yb_ref[...]
    hi = jnp.uint32(0xFFFF0000)
    lo_sum = w1 * pltpu.bitcast(ua << 16, F32) + w2 * pltpu.bitcast(ub << 16, F32)
    hi_sum = w1 * pltpu.bitcast(ua & hi, F32) + w2 * pltpu.bitcast(ub & hi, F32)
    o = x_ref[...] + jnp.concatenate([lo_sum, hi_sum], axis=1)
    normed = o * lax.rsqrt(jnp.mean(o * o, axis=-1, keepdims=True) + EPS) * fw_ref[...]
    if final:
        o_ref[...] = normed
    else:
        o_ref[...] = o
        h_ref[...] = normed.astype(h_ref.dtype)


def _combine(slot1, slot2, x, wcol, y, norm_w, final):
    t, d = x.shape
    tm = COMB_TM
    s3 = lambda a: a.reshape(t // tm, 1, tm)
    row_blk = pl.BlockSpec((tm, d), lambda i: (i, 0))
    out_f32 = jax.ShapeDtypeStruct((t, d), F32)
    return pl.pallas_call(
        functools.partial(_combine_kernel, final=final),
        out_shape=out_f32 if final else (out_f32, jax.ShapeDtypeStruct((t, d), BF16)),
        grid=(t // tm,),
        in_specs=[pl.BlockSpec((1, 1, tm), lambda i: (i, 0, 0), memory_space=pltpu.SMEM),
                  pl.BlockSpec((1, 1, tm), lambda i: (i, 0, 0), memory_space=pltpu.SMEM),
                  pl.BlockSpec((tm, d), lambda i: (i, 0)),
                  pl.BlockSpec((tm, LANES), lambda i: (i, 0)),
                  pl.BlockSpec((1, d), lambda i: (0, 0)),
                  pl.BlockSpec(memory_space=pl.ANY)],
        out_specs=row_blk if final else (row_blk, row_blk),
        scratch_shapes=[pltpu.VMEM((tm, d // 2), U32), pltpu.VMEM((tm, d // 2), U32),
                        pltpu.SemaphoreType.DMA(())],
        compiler_params=_cp(("arbitrary",)),
        name="combine",
    )(s3(slot1), s3(slot2), x, wcol, norm_w.reshape(1, d), y)


def _mixer_layer(xf, h, batch, seq, w_in, dn_conv_w, dn_a_log, dn_dt_bias, dn_norm_w, ml_igate_b,
                 ml_fgate_b, ml_norm_w, sc_conv_w, w_out):
    c0 = 3 * DN_WIDTH + DN_WIDTH
    c1 = c0 + 4 * DN_HEADS
    c2 = c1 + 2 * ML_HEADS * ML_QK + 2 * ML_WIDTH
    c3 = c2 + 4 * ML_HEADS
    w_main = jnp.concatenate([w_in[:, :c0], w_in[:, c1:c2], w_in[:, c3:]], axis=1).astype(BF16)
    w_gate = _pack_gate_cols(w_in[:, c0:c0 + 2 * DN_HEADS], w_in[:, c0 + 2 * DN_HEADS:c1],
                             w_in[:, c2:c2 + 2 * ML_HEADS], w_in[:, c2 + 2 * ML_HEADS:c3]).astype(BF16)
    proj, graw = _inproj(h, w_main, w_gate)
    gc, gr = _gate_prep(graw, _gate_params(dn_a_log, dn_dt_bias, ml_igate_b, ml_fgate_b))
    dn = _deltanet(proj, dn_conv_w, gc, gr, dn_norm_w, batch, seq)
    ml = _mlstm(proj, gc, gr, ml_norm_w, batch, seq)
    sc = _shortconv(proj, sc_conv_w, batch, seq)
    return _outproj(dn, ml, sc, w_out.astype(BF16), xf)


def _moe_layer(xf, norm_w, rg_w, rg_b, re_w, re_b, w_gate, w_up, w_down, layer, out_norm_w, final):
    t, d = xf.shape
    def lanes(g, e):
        r = g.shape[0]
        return jnp.concatenate([g, jnp.zeros((r, EXP_ROW0 - N_GROUPS), F32), e,
                                jnp.zeros((r, LANES - EXP_ROW0 - N_EXPERTS), F32)], axis=1)

    wr = lanes(rg_w, re_w)
    br = lanes(rg_b.reshape(1, -1), re_b.reshape(1, -1))
    hp, meta, wcol, cnt = _router(xf, norm_w, wr, br)
    counts = cnt[:, 0].astype(I32)
    padded = (counts + MOE_BLK - 1) // MOE_BLK * MOE_BLK
    pad_end = jnp.cumsum(padded)
    pad_start = pad_end - padded
    n_blocks = -(-(2 * t + N_EXPERTS * (MOE_BLK - 1)) // MOE_BLK)
    n_used = (pad_end[-1:] // MOE_BLK).astype(I32)
    blk_row = jnp.arange(n_blocks, dtype=I32)[:, None] * MOE_BLK
    block_exp = jnp.minimum(jnp.sum((pad_end[None, :] <= blk_row).astype(I32), axis=1), N_EXPERTS - 1)
    slot1 = pad_start[meta[0]] + meta[2]
    slot2 = pad_start[meta[1]] + meta[3]
    xs = _scatter_rows(pad_start + counts, pad_end, n_used, slot1, slot2, hp, n_blocks * MOE_BLK)
    hmid = _ffn_up(block_exp, n_used, xs, w_gate, w_up, layer, n_blocks)
    y = _ffn_down(block_exp, n_used, hmid, w_down, layer, n_blocks)
    return _combine(slot1, slot2, xf, wcol, y, out_norm_w, final)


def kernel(x, norm_mix_w, w_in, dn_conv_w, dn_a_log, dn_dt_bias, dn_norm_w, ml_igate_b, ml_fgate_b, ml_norm_w,
           sc_conv_w, w_out, norm_ffn_w, router_group_w, router_group_b, router_expert_w, router_expert_b,
           expert_w_gate, expert_w_up, expert_w_down, final_norm_w):
    batch, seq, d = x.shape
    depth = w_in.shape[0]
    xf = x.reshape(batch * seq, d)
    h = _rmsnorm(xf, norm_mix_w[0], BF16)
    for l in range(depth):
        last = l == depth - 1
        xf = _mixer_layer(xf, h, batch, seq, w_in[l], dn_conv_w[l], dn_a_log[l], dn_dt_bias[l],
                          dn_norm_w[l], ml_igate_b[l], ml_fgate_b[l], ml_norm_w[l], sc_conv_w[l], w_out[l])
        out = _moe_layer(xf, norm_ffn_w[l], router_group_w[l], router_group_b[l], router_expert_w[l],
                         router_expert_b[l], expert_w_gate, expert_w_up, expert_w_down, l,
                         final_norm_w if last else norm_mix_w[l + 1], last)
        xf, h = (out, None) if last else out
    return xf.reshape(batch, seq, d)
```

```python
import functools

import numpy as np
import jax
import jax.numpy as jnp
from jax import lax
from jax.experimental import pallas as pl
from jax.experimental.pallas import tpu as pltpu

F32 = jnp.float32
BF16 = jnp.bfloat16
I32 = jnp.int32
U32 = jnp.uint32

D_MODEL = 4096
DN_HEADS = 12
DN_DIM = 128
DN_WIDTH = DN_HEADS * DN_DIM
DN_CONV = 5
ML_HEADS = 6
ML_QK = 128
ML_V = 256
ML_WIDTH = ML_HEADS * ML_V
SC_WIDTH = 1024
SC_CONV = 3
N_GROUPS = 4
EXPERTS_PER_GROUP = 8
N_EXPERTS = N_GROUPS * EXPERTS_PER_GROUP
D_FF = 768
EPS = 1e-6

LANES = 128
SUBLANES = 8
VMEM_LIMIT = 52 * 1024 * 1024

CH = 64
TL = 256
CPT = TL // CH
DN_TPI = 1
DN_HPS = 2
N_HEADS_ALL = DN_HEADS + ML_HEADS
MOE_BLK = 256
NEG = -1e30

OFF_DN_Q, OFF_DN_K, OFF_DN_V, OFF_DN_Z = 0, 1536, 3072, 4608
OFF_ML_Q, OFF_ML_K, OFF_ML_V, OFF_ML_O = 6144, 6912, 7680, 9216
OFF_SC_B, OFF_SC_C, OFF_SC_X = 10752, 11776, 12800
N_MAIN = 13824


def _cp(sem, vmem=VMEM_LIMIT):
    return pltpu.CompilerParams(dimension_semantics=sem, vmem_limit_bytes=vmem)


def _mm(a, b):
    return jnp.dot(a.astype(BF16), b.astype(BF16), preferred_element_type=F32)


def _mm_nt(a, b):
    return lax.dot_general(a.astype(BF16), b.astype(BF16), (((1,), (1,)), ((), ())),
                           preferred_element_type=F32)


def _mm_f32(a, b):
    return jnp.dot(a, b, preferred_element_type=F32, precision=lax.Precision.HIGHEST)


def _sigmoid(x):
    return 1.0 / (1.0 + jnp.exp(-x))


def _softplus(x):
    return jnp.maximum(x, 0.0) + jnp.log(1.0 + jnp.exp(-jnp.abs(x)))


def _rms_kernel(x_ref, w_ref, o_ref):
    x = x_ref[...]
    ms = jnp.mean(x * x, axis=-1, keepdims=True)
    o_ref[...] = (x * lax.rsqrt(ms + EPS) * w_ref[...]).astype(o_ref.dtype)


def _rmsnorm(x2d, w, out_dtype, tm=256):
    t, d = x2d.shape
    return pl.pallas_call(
        _rms_kernel,
        out_shape=jax.ShapeDtypeStruct((t, d), out_dtype),
        grid=(t // tm,),
        in_specs=[pl.BlockSpec((tm, d), lambda i: (i, 0)), pl.BlockSpec((1, d), lambda i: (0, 0))],
        out_specs=pl.BlockSpec((tm, d), lambda i: (i, 0)),
        compiler_params=_cp(("arbitrary",)),
        name="rmsnorm",
    )(x2d, w.reshape(1, d))


def _inproj_kernel(a_ref, w_ref, wg_ref, o_ref, g_ref):
    a = a_ref[...]
    o_ref[...] = jnp.dot(a, w_ref[...], preferred_element_type=F32).astype(o_ref.dtype)

    @pl.when(pl.program_id(1) == 0)
    def _():
        g_ref[...] = jnp.dot(a, wg_ref[...], preferred_element_type=F32)


def _inproj(h, w_main, w_gate, tm=1024, tn=512):
    t, d = h.shape
    n = w_main.shape[1]
    ng = w_gate.shape[1]
    return pl.pallas_call(
        _inproj_kernel,
        out_shape=(jax.ShapeDtypeStruct((t, n), BF16), jax.ShapeDtypeStruct((t, ng), F32)),
        grid=(t // tm, n // tn),
        in_specs=[pl.BlockSpec((tm, d), lambda i, j: (i, 0)),
                  pl.BlockSpec((d, tn), lambda i, j: (0, j)),
                  pl.BlockSpec((d, ng), lambda i, j: (0, 0))],
        out_specs=(pl.BlockSpec((tm, tn), lambda i, j: (i, j)),
                   pl.BlockSpec((tm, ng), lambda i, j: (i, 0))),
        compiler_params=_cp(("arbitrary", "arbitrary")),
        name="inproj",
    )(h, w_main, w_gate)


def _gate_kernel(raw_ref, p_ref, gc_ref, gr_ref):
    x = raw_ref[...]
    lane = lax.broadcasted_iota(I32, x.shape, 1)
    chn = lane & 7
    is_ml = lane >= LANES
    a_log, dt_b, ig_b, fg_b = p_ref[0:1, :], p_ref[1:2, :], p_ref[2:3, :], p_ref[3:4, :]
    beta = _sigmoid(x)
    g = -jnp.exp(a_log) * _softplus(x + dt_b)
    ig = x + ig_b
    lf = -_softplus(-(x + fg_b))
    lo = chn < 2
    val = jnp.where(is_ml, jnp.where(lo, ig, lf), jnp.where(lo, beta, g))
    val = jnp.where(chn < 4, val, 0.0)
    ri = lax.broadcasted_iota(I32, (TL, TL), 0)
    ci = lax.broadcasted_iota(I32, (TL, TL), 1)
    same = (ri >> 6) == (ci >> 6)
    m_f = jnp.where(same & (ri >= ci), 1.0, 0.0)
    m_b = jnp.where(same & (ri <= ci), 1.0, 0.0)
    m_t = jnp.where(same, 1.0, 0.0)
    cum_f = _mm_f32(m_f, val)
    cum_b = _mm_f32(m_b, val)
    tot = pltpu.roll(_mm_f32(m_t, val), 2, axis=1)
    out = jnp.where(chn == 2, cum_f, jnp.where(chn == 3, cum_b, jnp.where((chn == 4) | (chn == 5), tot, val)))
    gr_ref[0] = out.T
    for hh in range(N_HEADS_ALL):
        grp, h = (0, hh) if hh < DN_HEADS else (1, hh - DN_HEADS)
        v = out[:, grp * LANES:(grp + 1) * LANES]
        sh = (LANES - 8 * h) % LANES
        gc_ref[hh] = pltpu.roll(v, sh, axis=1) if sh else v


def _gate_prep(raw, params):
    t = raw.shape[0]
    nt = t // TL
    return pl.pallas_call(
        _gate_kernel,
        out_shape=(jax.ShapeDtypeStruct((N_HEADS_ALL, t, LANES), F32),
                   jax.ShapeDtypeStruct((nt, 2 * LANES, TL), F32)),
        grid=(nt,),
        in_specs=[pl.BlockSpec((TL, 2 * LANES), lambda i: (i, 0)),
                  pl.BlockSpec((8, 2 * LANES), lambda i: (0, 0))],
        out_specs=(pl.BlockSpec((N_HEADS_ALL, TL, LANES), lambda i: (0, i, 0)),
                   pl.BlockSpec((1, 2 * LANES, TL), lambda i: (i, 0, 0))),
        compiler_params=_cp(("arbitrary",)),
        name="gate_prep",
    )(raw, params)


def _conv_centred(x_ref, w_ref, pad_ref, width):
    s = x_ref.shape[0]
    half = width // 2
    pad_ref[0:8, :] = jnp.zeros((8, pad_ref.shape[1]), F32)
    pad_ref[s + 8:s + 16, :] = jnp.zeros((8, pad_ref.shape[1]), F32)
    pad_ref[8:s + 8, :] = x_ref[...].astype(F32)
    acc = None
    for i in range(width):
        term = w_ref[i:i + 1, :] * pad_ref[8 - half + i:8 - half + i + s, :]
        acc = term if acc is None else acc + term
    return acc


def _conv_centred_tiles(x_ref, w_ref, pad_ref, width, cols):
    s = x_ref.shape[0]
    half = width // 2
    pad_ref[0:8, :] = jnp.zeros((8, pad_ref.shape[1]), F32)
    pad_ref[s + 8:s + 16, :] = jnp.zeros((8, pad_ref.shape[1]), F32)
    pad_ref[8:s + 8, :] = x_ref[:, cols].astype(F32)
    for rt in range(s // TL):
        base = 8 - half + rt * TL
        acc = None
        for i in range(width):
            term = w_ref[i:i + 1, cols] * pad_ref[base + i:base + i + TL, :]
            acc = term if acc is None else acc + term
        yield slice(rt * TL, (rt + 1) * TL), acc


def _tile_masks():
    ri = lax.broadcasted_iota(I32, (TL, TL), 0)
    ci = lax.broadcasted_iota(I32, (TL, TL), 1)
    return ri, ci


def _unit_tri_inverse(mats, same16, same32, eye):
    each = lambda f, *ls: [f(*xs) for xs in zip(*ls)]
    a16 = each(lambda a: jnp.where(same16, a, 0.0).astype(BF16), mats)
    b1 = each(lambda x: _mm(x, x).astype(BF16), a16)
    p = each(lambda x: eye - x.astype(F32), a16)
    b2 = each(lambda x: _mm(x, x).astype(BF16), b1)
    p = each(lambda x, b: x + _mm(x, b), p, b1)
    b3 = each(lambda x: _mm(x, x).astype(BF16), b2)
    p = each(lambda x, b: x + _mm(x, b), p, b2)
    p = each(lambda x, b: x + _mm(x, b), p, b3)
    off32 = same32 & jnp.logical_not(same16)
    a32 = each(lambda a: jnp.where(off32, a, 0.0).astype(BF16), mats)
    pb = each(lambda x: x.astype(BF16), p)
    t = each(_mm, pb, a32)
    p = each(lambda x, y, z: x - _mm(y, z), p, t, pb)
    a64 = each(lambda a: jnp.where(same32, 0.0, a).astype(BF16), mats)
    pb = each(lambda x: x.astype(BF16), p)
    t = each(_mm, pb, a64)
    p = each(lambda x, y, z: x - _mm(y, z), p, t, pb)
    return p


def _dn_prep_kernel(x_ref, w_ref, o_ref, pad_ref):
    j = pl.program_id(1)
    n_qk = 2 * DN_WIDTH // x_ref.shape[1]
    qscale = jnp.where(j < n_qk // 2, DN_DIM ** -0.5, 1.0)

    def body(normalise):
        for h in range(x_ref.shape[1] // DN_DIM):
            sl = slice(h * DN_DIM, (h + 1) * DN_DIM)
            for r, y in _conv_centred_tiles(x_ref, w_ref, pad_ref, DN_CONV, sl):
                y = y * _sigmoid(y)
                if normalise:
                    y = y * (lax.rsqrt(jnp.sum(y * y, axis=-1, keepdims=True) + EPS) * qscale)
                o_ref[r, sl] = y.astype(o_ref.dtype)

    pl.when(j < n_qk)(functools.partial(body, True))
    pl.when(j >= n_qk)(functools.partial(body, False))


def _dn_prep(proj, conv_w, batch, seq, wc=256):
    t = proj.shape[0]
    n = 3 * DN_WIDTH
    return pl.pallas_call(
        _dn_prep_kernel,
        out_shape=jax.ShapeDtypeStruct((t, n), BF16),
        grid=(batch, n // wc),
        in_specs=[pl.BlockSpec((seq, wc), lambda b, j: (b, j)), pl.BlockSpec((DN_CONV, wc), lambda b, j: (0, j))],
        out_specs=pl.BlockSpec((seq, wc), lambda b, j: (b, j)),
        scratch_shapes=[pltpu.VMEM((seq + 16, DN_DIM), F32)],
        compiler_params=_cp(("arbitrary", "arbitrary")),
        name="dn_prep",
    )(proj, conv_w)


def _dn_kernel(q_ref, k_ref, v_ref, z_ref, gc_ref, gr_ref, nw_ref, o_ref,
               u_ref, wq_ref, ak_ref, eg_ref, oo_ref):
    s = q_ref.shape[0]
    nt = s // TL
    nc = s // CH
    heads = range(DN_HPS)
    hl = [slice(h * DN_DIM, (h + 1) * DN_DIM) for h in heads]

    def tile_body(j, carry):
        ri, ci = _tile_masks()
        same64 = (ri >> 6) == (ci >> 6)
        same32 = (ri >> 5) == (ci >> 5)
        same16 = (ri >> 4) == (ci >> 4)
        eye = jnp.where(ri == ci, 1.0, 0.0)
        incl = [same64 & (ri >= ci), same64 & (ri <= ci)]
        strict = [same64 & (ri > ci), same64 & (ri < ci)]
        tiles = [j * DN_TPI + t for t in range(DN_TPI)]
        r0 = [pl.multiple_of(i * TL, TL) for i in tiles]
        ht = [(h, t) for h in heads for t in range(DN_TPI)]
        qn = {(h, t): q_ref[pl.ds(r0[t], TL), hl[h]].astype(F32) for h, t in ht}
        kn = {(h, t): k_ref[pl.ds(r0[t], TL), hl[h]].astype(F32) for h, t in ht}
        vv = {(h, t): v_ref[pl.ds(r0[t], TL), hl[h]].astype(F32) for h, t in ht}
        kk = {x: _mm_nt(kn[x], kn[x]) for x in ht}
        qk = {x: _mm_nt(qn[x], kn[x]) for x in ht}
        gc = {(h, t): gc_ref[h, pl.ds(r0[t], TL), :] for h, t in ht}
        gr = {(h, t): gr_ref[tiles[t]][8 * h:8 * h + 8, :] for h, t in ht}
        ch = [(h, t, d) for h, t in ht for d in (0, 1)]
        beta_c = [gc[h, t][:, d:d + 1] for h, t, d in ch]
        g_c = [gc[h, t][:, 2 + d:3 + d] for h, t, d in ch]
        gt_c = [gc[h, t][:, 4 + d:5 + d] for h, t, d in ch]
        g_r = [gr[h, t][2 + d:3 + d, :] for h, t, d in ch]
        dec = [jnp.exp(jnp.where(incl[d], g_c[n] - g_r[n], NEG)) for n, (h, t, d) in enumerate(ch)]
        a = [jnp.where(strict[d], beta_c[n] * kk[h, t] * dec[n], 0.0) for n, (h, t, d) in enumerate(ch)]
        tinv = _unit_tri_inverse(a, same16, same32, eye)
        eg_c = [jnp.exp(g) for g in g_c]
        rhs = [jnp.concatenate([beta_c[n] * vv[h, t], (beta_c[n] * eg_c[n]) * kn[h, t]], axis=1)
               for n, (h, t, d) in enumerate(ch)]
        uw = [_mm(x, y) for x, y in zip(tinv, rhs)]
        for n, (h, t, d) in enumerate(ch):
            i = tiles[t]
            u_ref[h, d, pl.ds(r0[t], TL), :] = uw[n][:, :DN_DIM]
            w = uw[n][:, DN_DIM:]
            qe = qn[h, t] * eg_c[n]
            attn = qk[h, t] * dec[n]
            kdt = (kn[h, t] * jnp.exp(gt_c[n] - g_c[n])).T
            egt = jnp.exp(gt_c[n])
            for c in range(CPT):
                rows = slice(c * CH, (c + 1) * CH)
                wq_ref[h, d, i * CPT + c] = jnp.concatenate([w[rows], qe[rows]], axis=0).astype(BF16)
                ak_ref[h, d, i * CPT + c] = jnp.concatenate([attn[rows, rows], kdt[:, rows]], axis=0).astype(BF16)
                eg_ref[h, d, i * CPT + c] = jnp.broadcast_to(egt[c * CH:c * CH + 8, :], (8, LANES))
        return carry

    lax.fori_loop(0, nt // DN_TPI, tile_body, 0)

    hd = [(h, d) for h in heads for d in (0, 1)]

    def chunk_body(c, carry):
        states = list(carry)
        ws, vnew, m2 = {}, {}, {}
        cc = [c, nc - 1 - c]
        r = [pl.multiple_of(x * CH, CH) for x in cc]
        for n, (h, d) in enumerate(hd):
            ws[n] = jnp.dot(wq_ref[h, d, cc[d]], states[n].astype(BF16), preferred_element_type=F32)
        for n, (h, d) in enumerate(hd):
            vnew[n] = u_ref[h, d, pl.ds(r[d], CH), :] - ws[n][:CH]
        for n, (h, d) in enumerate(hd):
            m2[n] = jnp.dot(ak_ref[h, d, cc[d]], vnew[n].astype(BF16), preferred_element_type=F32)
        for n, (h, d) in enumerate(hd):
            oo_ref[h, d, pl.ds(r[d], CH), :] = ws[n][CH:] + m2[n][:CH]
            states[n] = eg_ref[h, d, cc[d]][0:1, :] * states[n] + m2[n][CH:]
        return tuple(states)

    z0 = jnp.zeros((DN_DIM, DN_DIM), F32)
    lax.fori_loop(0, nc, chunk_body, tuple(z0 for _ in hd))

    for h in heads:
        for rt in range(s // TL):
            r = slice(rt * TL, (rt + 1) * TL)
            o = oo_ref[h, 0, r, :] + oo_ref[h, 1, r, :]
            o = o * lax.rsqrt(jnp.mean(o * o, axis=-1, keepdims=True) + EPS) * nw_ref[...]
            z = z_ref[r, hl[h]].astype(F32)
            o_ref[r, hl[h]] = (o * (z * _sigmoid(z))).astype(o_ref.dtype)


def _deltanet(proj, conv_w, gc, gr, norm_w, batch, seq):
    t = proj.shape[0]
    nc = seq // CH
    nt = seq // TL
    wb = DN_HPS * DN_DIM
    qkv = _dn_prep(proj, conv_w, batch, seq)
    qb, kb, vb, zb = OFF_DN_Q // wb, OFF_DN_K // wb, OFF_DN_V // wb, OFF_DN_Z // wb
    col = lambda off: pl.BlockSpec((seq, wb), lambda b, h, off=off: (b, off + h))
    return pl.pallas_call(
        _dn_kernel,
        out_shape=jax.ShapeDtypeStruct((t, DN_WIDTH), BF16),
        grid=(batch, DN_HEADS // DN_HPS),
        in_specs=[col(qb), col(kb), col(vb), col(zb),
                  pl.BlockSpec((DN_HPS, seq, LANES), lambda b, h: (h, b, 0)),
                  pl.BlockSpec((nt, 8 * DN_HPS, TL), lambda b, h: (b, h, 0)),
                  pl.BlockSpec((1, DN_DIM), lambda b, h: (0, 0))],
        out_specs=pl.BlockSpec((seq, wb), lambda b, h: (b, h)),
        scratch_shapes=[pltpu.VMEM((DN_HPS, 2, seq, DN_DIM), F32),
                        pltpu.VMEM((DN_HPS, 2, nc, 2 * CH, DN_DIM), BF16),
                        pltpu.VMEM((DN_HPS, 2, nc, CH + DN_DIM, CH), BF16),
                        pltpu.VMEM((DN_HPS, 2, nc, 8, LANES), F32),
                        pltpu.VMEM((DN_HPS, 2, seq, DN_DIM), F32)],
        compiler_params=_cp(("arbitrary", "arbitrary")),
        name="deltanet",
    )(qkv, qkv, qkv, proj, gc, gr, norm_w.reshape(1, DN_DIM))


def _ml_kernel(q_ref, k_ref, v_ref, og_ref, gc_ref, gr_ref, nw_ref, o_ref, oo_ref):
    s = q_ref.shape[0]
    nt = s // TL
    scale = ML_QK ** -0.5

    def tile_body(it, carry):
        m_carry = [carry[0], carry[1]]
        c_carry = [carry[2], carry[3]]
        n_carry = [carry[4], carry[5]]
        ri, ci = _tile_masks()
        same64 = (ri >> 6) == (ci >> 6)
        rowc = lax.broadcasted_iota(I32, (TL, 1), 0) >> 6
        colc = lax.broadcasted_iota(I32, (1, TL), 1) >> 6
        dirs = (0, 1)
        order = [list(range(CPT)), list(range(CPT - 1, -1, -1))]
        tile = [it, nt - 1 - it]
        r0 = [pl.multiple_of(i * TL, TL) for i in tile]
        qb = [q_ref[pl.ds(r, TL), :] for r in r0]
        qs = [x.astype(F32) * scale for x in qb]
        kf = [k_ref[pl.ds(r, TL), :].astype(F32) for r in r0]
        vb = [v_ref[pl.ds(r, TL), :] for r in r0]
        qk = [_mm_nt(a, b) for a, b in zip(qs, kf)]
        gc = [gc_ref[0, pl.ds(r, TL), :] for r in r0]
        gr = [gr_ref[i] for i in tile]
        ig_c = [gc[d][:, d:d + 1] for d in dirs]
        b_c = [gc[d][:, 2 + d:3 + d] for d in dirs]
        bt_c = [gc[d][:, 4 + d:5 + d] for d in dirs]
        ig_r = [gr[d][d:d + 1, :] for d in dirs]
        b_r = [gr[d][2 + d:3 + d, :] for d in dirs]
        bt_r = [gr[d][4 + d:5 + d, :] for d in dirs]
        incl = [same64 & (ri >= ci), same64 & (ri <= ci)]
        dmat = [jnp.where(incl[d], b_c[d] - b_r[d] + ig_r[d], NEG) for d in dirs]
        m_intra = [jnp.max(x, axis=1, keepdims=True) for x in dmat]
        d_end_c = [bt_c[d] - b_c[d] + ig_c[d] for d in dirs]
        d_end_r = [bt_r[d] - b_r[d] + ig_r[d] for d in dirs]
        dmax_c = [jnp.max(jnp.where(same64, x, NEG), axis=1, keepdims=True) for x in d_end_r]
        m_in_c, m_out_c, csc = [], [], []
        for d in dirs:
            m_in = jnp.zeros((TL, 1), F32)
            m_out = jnp.zeros((TL, 1), F32)
            m = m_carry[d]
            cs = {}
            for cc in order[d]:
                btot = bt_c[d][cc * CH:cc * CH + 1, :]
                dmx = dmax_c[d][cc * CH:cc * CH + 1, :]
                m_new = jnp.maximum(btot + m, dmx)
                cs[cc] = jnp.exp(btot + m - m_new)
                m_in = jnp.where(rowc == cc, m, m_in)
                m_out = jnp.where(rowc == cc, m_new, m_out)
                m = m_new
            m_carry[d] = m
            m_in_c.append(m_in)
            m_out_c.append(m_out)
            csc.append(cs)
        m_inter = [b_c[d] + m_in_c[d] for d in dirs]
        m_pos = [jnp.maximum(m_intra[d], m_inter[d]) for d in dirs]
        scores = [qk[d] * jnp.exp(dmat[d] - m_pos[d]) for d in dirs]
        inter_scale = [jnp.exp(m_inter[d] - m_pos[d]) for d in dirs]
        rsum = [jnp.sum(x, axis=1, keepdims=True) for x in scores]
        einv = [jnp.exp(-x) for x in m_pos]
        wk = [kf[d] * jnp.exp(d_end_c[d] - m_out_c[d]) for d in dirs]
        wkt = [x.T for x in wk]
        intra = [jnp.dot(scores[d].astype(BF16), vb[d], preferred_element_type=F32) for d in dirs]
        wstack = [jnp.concatenate([jnp.where(colc == c, wkt[d], 0.0) for c in range(CPT)], axis=0).astype(BF16)
                  for d in dirs]
        dcs = [jnp.dot(wstack[d], vb[d], preferred_element_type=F32) for d in dirs]
        c_at, n_row = [], []
        for d in dirs:
            cst, nst = c_carry[d], n_carry[d]
            c_start = {}
            nr = jnp.zeros((TL, ML_QK), F32)
            for cc in order[d]:
                c_start[cc] = cst
                nr = jnp.where(rowc == cc, nst, nr)
                dn = jnp.sum(wk[d][cc * CH:(cc + 1) * CH], axis=0, keepdims=True)
                cst = csc[d][cc] * cst + dcs[d][cc * ML_QK:(cc + 1) * ML_QK]
                nst = csc[d][cc] * nst + dn
            c_carry[d], n_carry[d] = cst, nst
            c_at.append(jnp.concatenate([c_start[c] for c in range(CPT)], axis=0).astype(BF16))
            n_row.append(nr)
        q4 = [jnp.concatenate([jnp.where(rowc == c, qb[d], jnp.zeros_like(qb[d])) for c in range(CPT)], axis=1)
              for d in dirs]
        inter = [jnp.dot(q4[d], c_at[d], preferred_element_type=F32) * scale for d in dirs]
        for d in dirs:
            qn = jnp.sum(qs[d] * n_row[d], axis=1, keepdims=True)
            num = intra[d] + inter_scale[d] * inter[d]
            den = rsum[d] + inter_scale[d] * qn
            oo_ref[d, pl.ds(r0[d], TL), :] = num / jnp.maximum(jnp.abs(den), einv[d])
        return (m_carry[0], m_carry[1], c_carry[0], c_carry[1], n_carry[0], n_carry[1])

    zero11 = jnp.zeros((1, 1), F32)
    c0 = jnp.zeros((ML_QK, ML_V), F32)
    n0 = jnp.zeros((1, ML_QK), F32)
    lax.fori_loop(0, nt, tile_body, (zero11, zero11, c0, c0, n0, n0))

    for rt in range(s // TL):
        r = slice(rt * TL, (rt + 1) * TL)
        o = oo_ref[0, r, :] + oo_ref[1, r, :]
        o = o * lax.rsqrt(jnp.mean(o * o, axis=-1, keepdims=True) + EPS) * nw_ref[...]
        o_ref[r, :] = (o * _sigmoid(og_ref[r, :].astype(F32))).astype(o_ref.dtype)


def _mlstm(proj, gc, gr, norm_w, batch, seq):
    t = proj.shape[0]
    nt = seq // TL
    qb, kb, vb, ob = OFF_ML_Q // ML_QK, OFF_ML_K // ML_QK, OFF_ML_V // ML_V, OFF_ML_O // ML_V
    return pl.pallas_call(
        _ml_kernel,
        out_shape=jax.ShapeDtypeStruct((t, ML_WIDTH), BF16),
        grid=(batch, ML_HEADS),
        in_specs=[pl.BlockSpec((seq, ML_QK), lambda b, h: (b, qb + h)),
                  pl.BlockSpec((seq, ML_QK), lambda b, h: (b, kb + h)),
                  pl.BlockSpec((seq, ML_V), lambda b, h: (b, vb + h)),
                  pl.BlockSpec((seq, ML_V), lambda b, h: (b, ob + h)),
                  pl.BlockSpec((1, seq, LANES), lambda b, h: (DN_HEADS + h, b, 0)),
                  pl.BlockSpec((nt, 8, TL), lambda b, h: (b, LANES // 8 + h, 0)),
                  pl.BlockSpec((1, ML_V), lambda b, h: (0, h))],
        out_specs=pl.BlockSpec((seq, ML_V), lambda b, h: (b, h)),
        scratch_shapes=[pltpu.VMEM((2, seq, ML_V), F32)],
        compiler_params=_cp(("arbitrary", "arbitrary")),
        name="mlstm",
    )(proj, proj, proj, proj, gc, gr, norm_w.reshape(1, ML_WIDTH))


def _sc_kernel(b_ref, c_ref, x_ref, w_ref, o_ref, cx_ref, pad_ref):
    cx_ref[...] = c_ref[...].astype(F32) * x_ref[...].astype(F32)
    y = _conv_centred(cx_ref, w_ref, pad_ref, SC_CONV)
    o_ref[...] = (b_ref[...].astype(F32) * y).astype(o_ref.dtype)


def _shortconv(proj, conv_w, batch, seq, wc=256):
    t = proj.shape[0]
    bb, cb, xb = OFF_SC_B // wc, OFF_SC_C // wc, OFF_SC_X // wc
    col = lambda off: pl.BlockSpec((seq, wc), lambda b, j, off=off: (b, off + j))
    return pl.pallas_call(
        _sc_kernel,
        out_shape=jax.ShapeDtypeStruct((t, SC_WIDTH), BF16),
        grid=(batch, SC_WIDTH // wc),
        in_specs=[col(bb), col(cb), col(xb), pl.BlockSpec((SC_CONV, wc), lambda b, j: (0, j))],
        out_specs=pl.BlockSpec((seq, wc), lambda b, j: (b, j)),
        scratch_shapes=[pltpu.VMEM((seq, wc), F32), pltpu.VMEM((seq + 16, wc), F32)],
        compiler_params=_cp(("arbitrary", "arbitrary")),
        name="shortconv",
    )(proj, proj, proj, conv_w)


def _pack_gate_cols(dn_b, dn_a, ml_i, ml_f):
    r = dn_b.shape[0]

    def grp(lo, hi, nh):
        a = jnp.stack([lo[:, :nh], lo[:, nh:], hi[:, :nh], hi[:, nh:]], axis=-1)
        a = jnp.concatenate([a, jnp.zeros((r, nh, 4), a.dtype)], axis=-1).reshape(r, nh * 8)
        return jnp.concatenate([a, jnp.zeros((r, LANES - nh * 8), a.dtype)], axis=-1)

    return jnp.concatenate([grp(dn_b, dn_a, DN_HEADS), grp(ml_i, ml_f, ML_HEADS)], axis=-1)


def _gate_params(dn_a_log, dn_dt_bias, ml_igate_b, ml_fgate_b):
    z_dn = jnp.zeros((1, 2 * DN_HEADS), F32)
    z_ml = jnp.zeros((1, 2 * ML_HEADS), F32)
    flat = lambda a: a.astype(F32).reshape(1, -1)
    rows = [_pack_gate_cols(z_dn, flat(dn_a_log), z_ml, z_ml),
            _pack_gate_cols(z_dn, flat(dn_dt_bias), z_ml, z_ml),
            _pack_gate_cols(z_dn, z_dn, flat(ml_igate_b), z_ml),
            _pack_gate_cols(z_dn, z_dn, z_ml, flat(ml_fgate_b))]
    return jnp.concatenate(rows + [jnp.zeros((4, 2 * LANES), F32)], axis=0)


def _outproj_kernel(a1_ref, a2_ref, a3_ref, w1_ref, w2_ref, w3_ref, x_ref, o_ref):
    acc = jnp.dot(a1_ref[...], w1_ref[...], preferred_element_type=F32)
    acc += jnp.dot(a2_ref[...], w2_ref[...], preferred_element_type=F32)
    acc += jnp.dot(a3_ref[...], w3_ref[...], preferred_element_type=F32)
    o_ref[...] = x_ref[...] + acc


def _outproj(dn, ml, sc, w_out, x, tm=1024, tn=512):
    t, d = x.shape
    k1, k2, k3 = dn.shape[1], ml.shape[1], sc.shape[1]
    assert k1 == k2 and (k1 + k2) % k3 == 0
    return pl.pallas_call(
        _outproj_kernel,
        out_shape=jax.ShapeDtypeStruct((t, d), F32),
        grid=(t // tm, d // tn),
        in_specs=[pl.BlockSpec((tm, k1), lambda i, j: (i, 0)),
                  pl.BlockSpec((tm, k2), lambda i, j: (i, 0)),
                  pl.BlockSpec((tm, k3), lambda i, j: (i, 0)),
                  pl.BlockSpec((k1, tn), lambda i, j: (0, j)),
                  pl.BlockSpec((k2, tn), lambda i, j: (1, j)),
                  pl.BlockSpec((k3, tn), lambda i, j: ((k1 + k2) // k3, j)),
                  pl.BlockSpec((tm, tn), lambda i, j: (i, j))],
        out_specs=pl.BlockSpec((tm, tn), lambda i, j: (i, j)),
        compiler_params=_cp(("arbitrary", "arbitrary")),
        name="outproj",
    )(dn, ml, sc, w_out, w_out, w_out, x)


ROUTE_TM = 256
EXP_ROW0 = 32
HALF_D = D_MODEL // 2


def _router_kernel(x_ref, nw_ref, wr_ref, br_ref, hp_ref, meta_ref, wc_ref, cnt_ref, base_ref,
                   whi_ref, wlo_ref):
    tm = x_ref.shape[0]

    @pl.when(pl.program_id(0) == 0)
    def _():
        base_ref[...] = jnp.zeros(base_ref.shape, F32)
        w = wr_ref[...]
        whi_ref[...] = w.astype(BF16)
        wlo_ref[...] = (w - whi_ref[...].astype(F32)).astype(BF16)

    x = x_ref[...]
    h = x * lax.rsqrt(jnp.mean(x * x, axis=-1, keepdims=True) + EPS) * nw_ref[...]
    bits = pltpu.bitcast(h.astype(BF16).astype(F32), U32)
    hp_ref[...] = (bits[:, :HALF_D] >> 16) | bits[:, HALF_D:]

    h_hi = h.astype(BF16)
    h_lo = (h - h_hi.astype(F32)).astype(BF16)
    logits = (jnp.dot(h_hi, whi_ref[...], preferred_element_type=F32)
              + jnp.dot(h_lo, whi_ref[...], preferred_element_type=F32)
              + jnp.dot(h_hi, wlo_ref[...], preferred_element_type=F32))
    lt = (logits + br_ref[...]).T
    row8 = lax.broadcasted_iota(I32, (8, tm), 0)
    gl = jnp.where(row8 < N_GROUPS, lt[0:8, :], NEG)
    gex = jnp.exp(gl - jnp.max(gl, axis=0, keepdims=True))
    gp = gex / jnp.sum(gex, axis=0, keepdims=True)
    g_w = jnp.max(gp, axis=0, keepdims=True)
    g_idx = jnp.min(jnp.where(gp == g_w, row8, 8), axis=0, keepdims=True)
    el = lt[EXP_ROW0 + 24:EXP_ROW0 + 32, :]
    for g in (2, 1, 0):
        el = jnp.where(g_idx == g, lt[EXP_ROW0 + 8 * g:EXP_ROW0 + 8 * g + 8, :], el)
    ee = jnp.exp(el - jnp.max(el, axis=0, keepdims=True))
    p = ee / jnp.sum(ee, axis=0, keepdims=True)
    p1 = jnp.max(p, axis=0, keepdims=True)
    i1 = jnp.min(jnp.where(p == p1, row8, 8), axis=0, keepdims=True)
    pm = jnp.where(row8 == i1, -1.0, p)
    p2 = jnp.max(pm, axis=0, keepdims=True)
    i2 = jnp.min(jnp.where(pm == p2, row8, 8), axis=0, keepdims=True)
    den = p1 + p2
    w1 = g_w * p1 / den
    w2 = g_w * p2 / den
    e1 = g_idx * EXPERTS_PER_GROUP + i1
    e2 = g_idx * EXPERTS_PER_GROUP + i2

    rowe = lax.broadcasted_iota(I32, (N_EXPERTS, tm), 0)
    oh1 = jnp.where(rowe == e1, 1.0, 0.0)
    oh2 = jnp.where(rowe == e2, 1.0, 0.0)
    oh = (oh1 + oh2).astype(BF16)
    ri = lax.broadcasted_iota(I32, (tm, tm), 0)
    ci = lax.broadcasted_iota(I32, (tm, tm), 1)
    before = jnp.where(ri < ci, 1.0, 0.0).astype(BF16)
    ones = jnp.ones((tm, tm), BF16)
    tot = jnp.dot(oh, before, preferred_element_type=F32) + base_ref[...]
    r1 = jnp.sum(oh1 * tot, axis=0, keepdims=True)
    r2 = jnp.sum(oh2 * tot, axis=0, keepdims=True)
    base_ref[...] = base_ref[...] + jnp.dot(oh, ones, preferred_element_type=F32)
    cnt_ref[...] = base_ref[...]

    meta_ref[...] = jnp.concatenate(
        [e1, e2, r1.astype(I32), r2.astype(I32), jnp.zeros((4, tm), I32)], axis=0)
    wrow = lax.broadcasted_iota(I32, (LANES, tm), 0)
    wc_ref[...] = jnp.where(wrow == 0, w1, jnp.where(wrow == 1, w2, 0.0)).T


def _router(x, norm_w, wr, br):
    t, d = x.shape
    tm = ROUTE_TM
    return pl.pallas_call(
        _router_kernel,
        out_shape=(jax.ShapeDtypeStruct((t, HALF_D), U32),
                   jax.ShapeDtypeStruct((8, t), I32),
                   jax.ShapeDtypeStruct((t, LANES), F32),
                   jax.ShapeDtypeStruct((N_EXPERTS, tm), F32)),
        grid=(t // tm,),
        in_specs=[pl.BlockSpec((tm, d), lambda i: (i, 0)),
                  pl.BlockSpec((1, d), lambda i: (0, 0)),
                  pl.BlockSpec((d, LANES), lambda i: (0, 0)),
                  pl.BlockSpec((1, LANES), lambda i: (0, 0))],
        out_specs=(pl.BlockSpec((tm, HALF_D), lambda i: (i, 0)),
                   pl.BlockSpec((8, tm), lambda i: (0, i)),
                   pl.BlockSpec((tm, LANES), lambda i: (i, 0)),
                   pl.BlockSpec((N_EXPERTS, tm), lambda i: (0, 0))),
        scratch_shapes=[pltpu.VMEM((N_EXPERTS, tm), F32),
                        pltpu.VMEM((d, LANES), BF16), pltpu.VMEM((d, LANES), BF16)],
        compiler_params=_cp(("arbitrary",)),
        name="router",
    )(x, norm_w.reshape(1, d), wr, br)


SCAT_TM = 256


def _scatter_kernel(zs_ref, ze_ref, nu_ref, s1_ref, s2_ref, hp_ref, xs_ref, zero_ref, sem):
    i = pl.program_id(0)
    tm = hp_ref.shape[0]
    n_all = xs_ref.shape[0] // MOE_BLK

    @pl.when(i == 0)
    def _():
        zero_ref[...] = jnp.zeros(zero_ref.shape, zero_ref.dtype)

        def zfill(e, c):
            zero_row = lambda r: pltpu.make_async_copy(zero_ref.at[pl.ds(0, 1), :], xs_ref.at[pl.ds(r, 1), :], sem)

            def one(r, c2):
                zero_row(r).start()
                return c2

            def one_wait(r, c2):
                zero_row(r).wait()
                return c2

            lax.fori_loop(zs_ref[e], ze_ref[e], one, 0)
            lax.fori_loop(zs_ref[e], ze_ref[e], one_wait, 0)
            return c

        def zero_block(b):
            r0 = pl.multiple_of(b * MOE_BLK, MOE_BLK)
            return pltpu.make_async_copy(zero_ref, xs_ref.at[pl.ds(r0, MOE_BLK), :], sem)

        def tfill(b, c):
            zero_block(b).start()
            return c

        def twait(b, c):
            zero_block(b).wait()
            return c

        lax.fori_loop(0, N_EXPERTS, zfill, 0)
        lax.fori_loop(nu_ref[0], n_all, tfill, 0)
        lax.fori_loop(nu_ref[0], n_all, twait, 0)

    def row_copies(t):
        src = hp_ref.at[pl.ds(t, 1), :]
        return (pltpu.make_async_copy(src, xs_ref.at[pl.ds(s1_ref[0, 0, t], 1), :], sem),
                pltpu.make_async_copy(src, xs_ref.at[pl.ds(s2_ref[0, 0, t], 1), :], sem))

    def issue(t, c):
        for cp in row_copies(t):
            cp.start()
        return c

    def drain(t, c):
        for cp in row_copies(t):
            cp.wait()
        return c

    lax.fori_loop(0, tm, issue, 0, unroll=8)
    lax.fori_loop(0, tm, drain, 0, unroll=8)


def _scatter_rows(zero_start, zero_end, n_used, slot1, slot2, hp, n_rows):
    t, w = hp.shape
    tm = SCAT_TM
    s3 = lambda a: a.reshape(t // tm, 1, tm)
    grid_spec = pltpu.PrefetchScalarGridSpec(
        num_scalar_prefetch=3,
        grid=(t // tm,),
        in_specs=[pl.BlockSpec((1, 1, tm), lambda i, zs, ze, nu: (i, 0, 0), memory_space=pltpu.SMEM),
                  pl.BlockSpec((1, 1, tm), lambda i, zs, ze, nu: (i, 0, 0), memory_space=pltpu.SMEM),
                  pl.BlockSpec((tm, w), lambda i, zs, ze, nu: (i, 0))],
        out_specs=pl.BlockSpec(memory_space=pl.ANY),
        scratch_shapes=[pltpu.VMEM((MOE_BLK, w), U32), pltpu.SemaphoreType.DMA(())],
    )
    return pl.pallas_call(
        _scatter_kernel,
        out_shape=jax.ShapeDtypeStruct((n_rows, w), U32),
        grid_spec=grid_spec,
        compiler_params=_cp(("arbitrary",)),
        name="scatter_rows",
    )(zero_start, zero_end, n_used, s3(slot1), s3(slot2), hp)


FF_TN = 256


def _unpack_rows(x_ref, xs_ref):
    u = x_ref[...]
    xs_ref[:, :HALF_D] = pltpu.bitcast(u << 16, F32).astype(BF16)
    xs_ref[:, HALF_D:] = pltpu.bitcast(u & jnp.uint32(0xFFFF0000), F32).astype(BF16)


def _ffn_up_kernel(be_ref, nxt_ref, par_ref, nu_ref, x_ref, wg_hbm, wu_hbm, o_ref,
                   wbuf_ref, wgb_ref, wub_ref, xs_ref, sem, *, layer):
    j = pl.program_id(0)
    i = pl.program_id(1)

    def fetch(e, p):
        cols = pl.ds(pl.multiple_of(j * FF_TN, FF_TN), FF_TN)
        return (pltpu.make_async_copy(wg_hbm.at[layer, e, :, cols], wbuf_ref.at[p, 0], sem.at[p, 0]),
                pltpu.make_async_copy(wu_hbm.at[layer, e, :, cols], wbuf_ref.at[p, 1], sem.at[p, 1]))

    @pl.when(i < nu_ref[0])
    def _():
        e = be_ref[i]
        p = par_ref[i]

        @pl.when(i == 0)
        def _():
            for cp in fetch(e, p):
                cp.start()

        @pl.when((i == 0) | (e != be_ref[jnp.maximum(i - 1, 0)]))
        def _():
            for cp in fetch(e, p):
                cp.wait()
            wgb_ref[...] = wbuf_ref[p, 0].astype(BF16)
            wub_ref[...] = wbuf_ref[p, 1].astype(BF16)

            @pl.when(nxt_ref[i] != e)
            def _():
                for cp in fetch(nxt_ref[i], 1 - p):
                    cp.start()

        _unpack_rows(x_ref, xs_ref)
        x = xs_ref[...]
        g = jnp.dot(x, wgb_ref[...], preferred_element_type=F32)
        u = jnp.dot(x, wub_ref[...], preferred_element_type=F32)
        o_ref[...] = (g * _sigmoid(g) * u).astype(o_ref.dtype)

    @pl.when(i >= nu_ref[0])
    def _():
        o_ref[...] = jnp.zeros(o_ref.shape, o_ref.dtype)


def _ffn_up(block_exp, next_exp, slot_par, n_used, xs, w_gate, w_up, layer, n_blocks):
    d, ff = w_gate.shape[2], w_gate.shape[3]
    grid_spec = pltpu.PrefetchScalarGridSpec(
        num_scalar_prefetch=4,
        grid=(ff // FF_TN, n_blocks),
        in_specs=[pl.BlockSpec((MOE_BLK, HALF_D), lambda j, i, be, nx, pr, nu: (jnp.minimum(i, nu[0] - 1), 0)),
                  pl.BlockSpec(memory_space=pl.ANY),
                  pl.BlockSpec(memory_space=pl.ANY)],
        out_specs=pl.BlockSpec((MOE_BLK, FF_TN), lambda j, i, be, nx, pr, nu: (i, j)),
        scratch_shapes=[pltpu.VMEM((2, 2, d, FF_TN), F32),
                        pltpu.VMEM((d, FF_TN), BF16), pltpu.VMEM((d, FF_TN), BF16),
                        pltpu.VMEM((MOE_BLK, d), BF16),
                        pltpu.SemaphoreType.DMA((2, 2))],
    )
    return pl.pallas_call(
        functools.partial(_ffn_up_kernel, layer=layer),
        out_shape=jax.ShapeDtypeStruct((n_blocks * MOE_BLK, ff), BF16),
        grid_spec=grid_spec,
        compiler_params=_cp(("arbitrary", "arbitrary")),
        name="ffn_up",
    )(block_exp, next_exp, slot_par, n_used, xs, w_gate, w_up)


def _ffn_down_kernel(be_ref, nxt_ref, par_ref, nu_ref, h_ref, wd_hbm, y_ref, wbuf_ref, wdb_ref, sem, *, layer):
    i = pl.program_id(0)

    def fetch(e, p):
        return pltpu.make_async_copy(wd_hbm.at[layer, e], wbuf_ref.at[p], sem.at[p])

    @pl.when(i < nu_ref[0])
    def _():
        e = be_ref[i]
        p = par_ref[i]

        @pl.when(i == 0)
        def _():
            fetch(e, p).start()

        @pl.when((i == 0) | (e != be_ref[jnp.maximum(i - 1, 0)]))
        def _():
            fetch(e, p).wait()
            wdb_ref[...] = wbuf_ref[p].astype(BF16)

            @pl.when(nxt_ref[i] != e)
            def _():
                fetch(nxt_ref[i], 1 - p).start()

        y = jnp.dot(h_ref[...], wdb_ref[...], preferred_element_type=F32)
        bits = pltpu.bitcast(y.astype(BF16).astype(F32), U32)
        half = bits.shape[1] // 2
        y_ref[...] = (bits[:, :half] >> 16) | bits[:, half:]

    @pl.when(i >= nu_ref[0])
    def _():
        y_ref[...] = jnp.zeros(y_ref.shape, y_ref.dtype)


def _ffn_down(block_exp, next_exp, slot_par, n_used, hmid, w_down, layer, n_blocks):
    ff, d = w_down.shape[2], w_down.shape[3]
    grid_spec = pltpu.PrefetchScalarGridSpec(
        num_scalar_prefetch=4,
        grid=(n_blocks,),
        in_specs=[pl.BlockSpec((MOE_BLK, ff), lambda i, be, nx, pr, nu: (jnp.minimum(i, nu[0] - 1), 0)),
                  pl.BlockSpec(memory_space=pl.ANY)],
        out_specs=pl.BlockSpec((MOE_BLK, d // 2), lambda i, be, nx, pr, nu: (i, 0)),
        scratch_shapes=[pltpu.VMEM((2, ff, d), F32), pltpu.VMEM((ff, d), BF16),
                        pltpu.SemaphoreType.DMA((2,))],
    )
    return pl.pallas_call(
        functools.partial(_ffn_down_kernel, layer=layer),
        out_shape=jax.ShapeDtypeStruct((n_blocks * MOE_BLK, d // 2), U32),
        grid_spec=grid_spec,
        compiler_params=_cp(("arbitrary",)),
        name="ffn_down",
    )(block_exp, next_exp, slot_par, n_used, hmid, w_down)


COMB_TM = 256


def _combine_kernel(s1_ref, s2_ref, x_ref, wc_ref, fw_ref, y_ref, *rest, final):
    o_ref, h_ref = (rest[0], None) if final else rest[:2]
    ya_ref, yb_ref, sem = rest[-3:]
    tm = x_ref.shape[0]

    def row_copies(t):
        return (pltpu.make_async_copy(y_ref.at[pl.ds(s1_ref[0, 0, t], 1), :], ya_ref.at[pl.ds(t, 1), :], sem),
                pltpu.make_async_copy(y_ref.at[pl.ds(s2_ref[0, 0, t], 1), :], yb_ref.at[pl.ds(t, 1), :], sem))

    def issue(t, c):
        for cp in row_copies(t):
            cp.start()
        return c

    def drain(t, c):
        for cp in row_copies(t):
            cp.wait()
        return c

    lax.fori_loop(0, tm, issue, 0, unroll=8)
    lax.fori_loop(0, tm, drain, 0, unroll=8)
    w1, w2 = wc_ref[:, 0:1], wc_ref[:, 1:2]
    ua, ub = ya_ref[...], yb_ref[...]
    hi = jnp.uint32(0xFFFF0000)
    lo_sum = w1 * pltpu.bitcast(ua << 16, F32) + w2 * pltpu.bitcast(ub << 16, F32)
    hi_sum = w1 * pltpu.bitcast(ua & hi, F32) + w2 * pltpu.bitcast(ub & hi, F32)
    o = x_ref[...] + jnp.concatenate([lo_sum, hi_sum], axis=1)
    normed = o * lax.rsqrt(jnp.mean(o * o, axis=-1, keepdims=True) + EPS) * fw_ref[...]
    if final:
        o_ref[...] = normed
    else:
        o_ref[...] = o
        h_ref[...] = normed.astype(h_ref.dtype)


def _combine(slot1, slot2, x, wcol, y, norm_w, final):
    t, d = x.shape
    tm = COMB_TM
    s3 = lambda a: a.reshape(t // tm, 1, tm)
    row_blk = pl.BlockSpec((tm, d), lambda i: (i, 0))
    out_f32 = jax.ShapeDtypeStruct((t, d), F32)
    return pl.pallas_call(
        functools.partial(_combine_kernel, final=final),
        out_shape=out_f32 if final else (out_f32, jax.ShapeDtypeStruct((t, d), BF16)),
        grid=(t // tm,),
        in_specs=[pl.BlockSpec((1, 1, tm), lambda i: (i, 0, 0), memory_space=pltpu.SMEM),
                  pl.BlockSpec((1, 1, tm), lambda i: (i, 0, 0), memory_space=pltpu.SMEM),
                  pl.BlockSpec((tm, d), lambda i: (i, 0)),
                  pl.BlockSpec((tm, LANES), lambda i: (i, 0)),
                  pl.BlockSpec((1, d), lambda i: (0, 0)),
                  pl.BlockSpec(memory_space=pl.ANY)],
        out_specs=row_blk if final else (row_blk, row_blk),
        scratch_shapes=[pltpu.VMEM((tm, d // 2), U32), pltpu.VMEM((tm, d // 2), U32),
                        pltpu.SemaphoreType.DMA(())],
        compiler_params=_cp(("arbitrary",)),
        name="combine",
    )(s3(slot1), s3(slot2), x, wcol, norm_w.reshape(1, d), y)


def _mixer_layer(xf, h, batch, seq, w_in, dn_conv_w, dn_a_log, dn_dt_bias, dn_norm_w, ml_igate_b,
                 ml_fgate_b, ml_norm_w, sc_conv_w, w_out):
    c0 = 3 * DN_WIDTH + DN_WIDTH
    c1 = c0 + 4 * DN_HEADS
    c2 = c1 + 2 * ML_HEADS * ML_QK + 2 * ML_WIDTH
    c3 = c2 + 4 * ML_HEADS
    w_main = jnp.concatenate([w_in[:, :c0], w_in[:, c1:c2], w_in[:, c3:]], axis=1).astype(BF16)
    w_gate = _pack_gate_cols(w_in[:, c0:c0 + 2 * DN_HEADS], w_in[:, c0 + 2 * DN_HEADS:c1],
                             w_in[:, c2:c2 + 2 * ML_HEADS], w_in[:, c2 + 2 * ML_HEADS:c3]).astype(BF16)
    proj, graw = _inproj(h, w_main, w_gate)
    gc, gr = _gate_prep(graw, _gate_params(dn_a_log, dn_dt_bias, ml_igate_b, ml_fgate_b))
    dn = _deltanet(proj, dn_conv_w, gc, gr, dn_norm_w, batch, seq)
    ml = _mlstm(proj, gc, gr, ml_norm_w, batch, seq)
    sc = _shortconv(proj, sc_conv_w, batch, seq)
    return _outproj(dn, ml, sc, w_out.astype(BF16), xf)


def _moe_layer(xf, norm_w, rg_w, rg_b, re_w, re_b, w_gate, w_up, w_down, layer, out_norm_w, final):
    t, d = xf.shape
    def lanes(g, e):
        r = g.shape[0]
        return jnp.concatenate([g, jnp.zeros((r, EXP_ROW0 - N_GROUPS), F32), e,
                                jnp.zeros((r, LANES - EXP_ROW0 - N_EXPERTS), F32)], axis=1)

    wr = lanes(rg_w, re_w)
    br = lanes(rg_b.reshape(1, -1), re_b.reshape(1, -1))
    hp, meta, wcol, cnt = _router(xf, norm_w, wr, br)
    counts = cnt[:, 0].astype(I32)
    padded = (counts + MOE_BLK - 1) // MOE_BLK * MOE_BLK
    pad_end = jnp.cumsum(padded)
    pad_start = pad_end - padded
    n_blocks = -(-(2 * t + N_EXPERTS * (MOE_BLK - 1)) // MOE_BLK)
    n_used = (pad_end[-1:] // MOE_BLK).astype(I32)
    blk_row = jnp.arange(n_blocks, dtype=I32)[:, None] * MOE_BLK
    block_exp = jnp.minimum(jnp.sum((pad_end[None, :] <= blk_row).astype(I32), axis=1), N_EXPERTS - 1)
    slot1 = pad_start[meta[0]] + meta[2]
    slot2 = pad_start[meta[1]] + meta[3]
    xs = _scatter_rows(pad_start + counts, pad_end, n_used, slot1, slot2, hp, n_blocks * MOE_BLK)
    blk_i = jnp.arange(n_blocks, dtype=I32)
    slot_par = (jnp.cumsum((block_exp != jnp.roll(block_exp, 1)).astype(I32).at[0].set(0)) & 1).astype(I32)
    later = (block_exp[None, :] > block_exp[:, None]) & (blk_i[None, :] < n_used[0])
    next_exp = jnp.min(jnp.where(later, block_exp[None, :], N_EXPERTS), axis=1)
    next_exp = jnp.where(next_exp == N_EXPERTS, block_exp, next_exp).astype(I32)
    hmid = _ffn_up(block_exp, next_exp, slot_par, n_used, xs, w_gate, w_up, layer, n_blocks)
    y = _ffn_down(block_exp, next_exp, slot_par, n_used, hmid, w_down, layer, n_blocks)
    return _combine(slot1, slot2, xf, wcol, y, out_norm_w, final)


def kernel(x, norm_mix_w, w_in, dn_conv_w, dn_a_log, dn_dt_bias, dn_norm_w, ml_igate_b, ml_fgate_b, ml_norm_w,
           sc_conv_w, w_out, norm_ffn_w, router_group_w, router_group_b, router_expert_w, router_expert_b,
           expert_w_gate, expert_w_up, expert_w_down, final_norm_w):
    batch, seq, d = x.shape
    depth = w_in.shape[0]
    xf = x.reshape(batch * seq, d)
    h = _rmsnorm(xf, norm_mix_w[0], BF16)
    for l in range(depth):
        last = l == depth - 1
        xf = _mixer_layer(xf, h, batch, seq, w_in[l], dn_conv_w[l], dn_a_log[l], dn_dt_bias[l],
                          dn_norm_w[l], ml_igate_b[l], ml_fgate_b[l], ml_norm_w[l], sc_conv_w[l], w_out[l])
        out = _moe_layer(xf, norm_ffn_w[l], router_group_w[l], router_group_b[l], router_expert_w[l],
                         router_expert_b[l], expert_w_gate, expert_w_up, expert_w_down, l,
                         final_norm_w if last else norm_mix_w[l + 1], last)
        xf, h = (out, None) if last else out
    return xf.reshape(batch, seq, d)
```

```python
import functools

import numpy as np
import jax
import jax.numpy as jnp
from jax import lax
from jax.experimental import pallas as pl
from jax.experimental.pallas import tpu as pltpu

F32 = jnp.float32
BF16 = jnp.bfloat16
I32 = jnp.int32
U32 = jnp.uint32

D_MODEL = 4096
DN_HEADS = 12
DN_DIM = 128
DN_WIDTH = DN_HEADS * DN_DIM
DN_CONV = 5
ML_HEADS = 6
ML_QK = 128
ML_V = 256
ML_WIDTH = ML_HEADS * ML_V
SC_WIDTH = 1024
SC_CONV = 3
N_GROUPS = 4
EXPERTS_PER_GROUP = 8
N_EXPERTS = N_GROUPS * EXPERTS_PER_GROUP
D_FF = 768
EPS = 1e-6

LANES = 128
SUBLANES = 8
VMEM_LIMIT = 52 * 1024 * 1024

CH = 64
TL = 256
CPT = TL // CH
DN_TPI = 1
DN_HPS = 2
N_HEADS_ALL = DN_HEADS + ML_HEADS
MOE_BLK = 256
NEG = -1e30

OFF_DN_Q, OFF_DN_K, OFF_DN_V, OFF_DN_Z = 0, 1536, 3072, 4608
OFF_ML_Q, OFF_ML_K, OFF_ML_V, OFF_ML_O = 6144, 6912, 7680, 9216
OFF_SC_B, OFF_SC_C, OFF_SC_X = 10752, 11776, 12800
N_MAIN = 13824


def _cp(sem, vmem=VMEM_LIMIT):
    return pltpu.CompilerParams(dimension_semantics=sem, vmem_limit_bytes=vmem)


def _mm(a, b):
    return jnp.dot(a.astype(BF16), b.astype(BF16), preferred_element_type=F32)


def _mm_nt(a, b):
    return lax.dot_general(a.astype(BF16), b.astype(BF16), (((1,), (1,)), ((), ())),
                           preferred_element_type=F32)


def _mm_f32(a, b):
    return jnp.dot(a, b, preferred_element_type=F32, precision=lax.Precision.HIGHEST)


def _sigmoid(x):
    return 1.0 / (1.0 + jnp.exp(-x))


def _softplus(x):
    return jnp.maximum(x, 0.0) + jnp.log(1.0 + jnp.exp(-jnp.abs(x)))


def _rms_kernel(x_ref, w_ref, o_ref):
    x = x_ref[...]
    ms = jnp.mean(x * x, axis=-1, keepdims=True)
    o_ref[...] = (x * lax.rsqrt(ms + EPS) * w_ref[...]).astype(o_ref.dtype)


def _rmsnorm(x2d, w, out_dtype, tm=256):
    t, d = x2d.shape
    return pl.pallas_call(
        _rms_kernel,
        out_shape=jax.ShapeDtypeStruct((t, d), out_dtype),
        grid=(t // tm,),
        in_specs=[pl.BlockSpec((tm, d), lambda i: (i, 0)), pl.BlockSpec((1, d), lambda i: (0, 0))],
        out_specs=pl.BlockSpec((tm, d), lambda i: (i, 0)),
        compiler_params=_cp(("arbitrary",)),
        name="rmsnorm",
    )(x2d, w.reshape(1, d))


def _inproj_kernel(a_ref, w_ref, wg_ref, o_ref, g_ref):
    a = a_ref[...]
    o_ref[...] = jnp.dot(a, w_ref[...], preferred_element_type=F32).astype(o_ref.dtype)

    @pl.when(pl.program_id(1) == 0)
    def _():
        g_ref[...] = jnp.dot(a, wg_ref[...], preferred_element_type=F32)


def _inproj(h, w_main, w_gate, tm=1024, tn=512):
    t, d = h.shape
    n = w_main.shape[1]
    ng = w_gate.shape[1]
    return pl.pallas_call(
        _inproj_kernel,
        out_shape=(jax.ShapeDtypeStruct((t, n), BF16), jax.ShapeDtypeStruct((t, ng), F32)),
        grid=(t // tm, n // tn),
        in_specs=[pl.BlockSpec((tm, d), lambda i, j: (i, 0)),
                  pl.BlockSpec((d, tn), lambda i, j: (0, j)),
                  pl.BlockSpec((d, ng), lambda i, j: (0, 0))],
        out_specs=(pl.BlockSpec((tm, tn), lambda i, j: (i, j)),
                   pl.BlockSpec((tm, ng), lambda i, j: (i, 0))),
        compiler_params=_cp(("arbitrary", "arbitrary")),
        name="inproj",
    )(h, w_main, w_gate)


def _gate_kernel(raw_ref, p_ref, gc_ref, gr_ref):
    x = raw_ref[...]
    lane = lax.broadcasted_iota(I32, x.shape, 1)
    chn = lane & 7
    is_ml = lane >= LANES
    a_log, dt_b, ig_b, fg_b = p_ref[0:1, :], p_ref[1:2, :], p_ref[2:3, :], p_ref[3:4, :]
    beta = _sigmoid(x)
    g = -jnp.exp(a_log) * _softplus(x + dt_b)
    ig = x + ig_b
    lf = -_softplus(-(x + fg_b))
    lo = chn < 2
    val = jnp.where(is_ml, jnp.where(lo, ig, lf), jnp.where(lo, beta, g))
    val = jnp.where(chn < 4, val, 0.0)
    ri = lax.broadcasted_iota(I32, (TL, TL), 0)
    ci = lax.broadcasted_iota(I32, (TL, TL), 1)
    same = (ri >> 6) == (ci >> 6)
    m_f = jnp.where(same & (ri >= ci), 1.0, 0.0)
    m_b = jnp.where(same & (ri <= ci), 1.0, 0.0)
    m_t = jnp.where(same, 1.0, 0.0)
    cum_f = _mm_f32(m_f, val)
    cum_b = _mm_f32(m_b, val)
    tot = pltpu.roll(_mm_f32(m_t, val), 2, axis=1)
    out = jnp.where(chn == 2, cum_f, jnp.where(chn == 3, cum_b, jnp.where((chn == 4) | (chn == 5), tot, val)))
    gr_ref[0] = out.T
    for hh in range(N_HEADS_ALL):
        grp, h = (0, hh) if hh < DN_HEADS else (1, hh - DN_HEADS)
        v = out[:, grp * LANES:(grp + 1) * LANES]
        sh = (LANES - 8 * h) % LANES
        gc_ref[hh] = pltpu.roll(v, sh, axis=1) if sh else v


def _gate_prep(raw, params):
    t = raw.shape[0]
    nt = t // TL
    return pl.pallas_call(
        _gate_kernel,
        out_shape=(jax.ShapeDtypeStruct((N_HEADS_ALL, t, LANES), F32),
                   jax.ShapeDtypeStruct((nt, 2 * LANES, TL), F32)),
        grid=(nt,),
        in_specs=[pl.BlockSpec((TL, 2 * LANES), lambda i: (i, 0)),
                  pl.BlockSpec((8, 2 * LANES), lambda i: (0, 0))],
        out_specs=(pl.BlockSpec((N_HEADS_ALL, TL, LANES), lambda i: (0, i, 0)),
                   pl.BlockSpec((1, 2 * LANES, TL), lambda i: (i, 0, 0))),
        compiler_params=_cp(("arbitrary",)),
        name="gate_prep",
    )(raw, params)


def _conv_centred(x_ref, w_ref, pad_ref, width):
    s = x_ref.shape[0]
    half = width // 2
    pad_ref[0:8, :] = jnp.zeros((8, pad_ref.shape[1]), F32)
    pad_ref[s + 8:s + 16, :] = jnp.zeros((8, pad_ref.shape[1]), F32)
    pad_ref[8:s + 8, :] = x_ref[...].astype(F32)
    acc = None
    for i in range(width):
        term = w_ref[i:i + 1, :] * pad_ref[8 - half + i:8 - half + i + s, :]
        acc = term if acc is None else acc + term
    return acc


def _conv_centred_tiles(x_ref, w_ref, pad_ref, width, cols):
    s = x_ref.shape[0]
    half = width // 2
    pad_ref[0:8, :] = jnp.zeros((8, pad_ref.shape[1]), F32)
    pad_ref[s + 8:s + 16, :] = jnp.zeros((8, pad_ref.shape[1]), F32)
    pad_ref[8:s + 8, :] = x_ref[:, cols].astype(F32)
    for rt in range(s // TL):
        base = 8 - half + rt * TL
        acc = None
        for i in range(width):
            term = w_ref[i:i + 1, cols] * pad_ref[base + i:base + i + TL, :]
            acc = term if acc is None else acc + term
        yield slice(rt * TL, (rt + 1) * TL), acc


def _tile_masks():
    ri = lax.broadcasted_iota(I32, (TL, TL), 0)
    ci = lax.broadcasted_iota(I32, (TL, TL), 1)
    return ri, ci


def _unit_tri_inverse(mats, same16, same32, eye):
    each = lambda f, *ls: [f(*xs) for xs in zip(*ls)]
    a16 = each(lambda a: jnp.where(same16, a, 0.0).astype(BF16), mats)
    b1 = each(lambda x: _mm(x, x).astype(BF16), a16)
    p = each(lambda x: eye - x.astype(F32), a16)
    b2 = each(lambda x: _mm(x, x).astype(BF16), b1)
    p = each(lambda x, b: x + _mm(x, b), p, b1)
    b3 = each(lambda x: _mm(x, x).astype(BF16), b2)
    p = each(lambda x, b: x + _mm(x, b), p, b2)
    p = each(lambda x, b: x + _mm(x, b), p, b3)
    off32 = same32 & jnp.logical_not(same16)
    a32 = each(lambda a: jnp.where(off32, a, 0.0).astype(BF16), mats)
    pb = each(lambda x: x.astype(BF16), p)
    t = each(_mm, pb, a32)
    p = each(lambda x, y, z: x - _mm(y, z), p, t, pb)
    a64 = each(lambda a: jnp.where(same32, 0.0, a).astype(BF16), mats)
    pb = each(lambda x: x.astype(BF16), p)
    t = each(_mm, pb, a64)
    p = each(lambda x, y, z: x - _mm(y, z), p, t, pb)
    return p


def _dn_prep_kernel(x_ref, w_ref, o_ref, pad_ref):
    j = pl.program_id(1)
    n_qk = 2 * DN_WIDTH // x_ref.shape[1]
    qscale = jnp.where(j < n_qk // 2, DN_DIM ** -0.5, 1.0)

    def body(normalise):
        for h in range(x_ref.shape[1] // DN_DIM):
            sl = slice(h * DN_DIM, (h + 1) * DN_DIM)
            for r, y in _conv_centred_tiles(x_ref, w_ref, pad_ref, DN_CONV, sl):
                y = y * _sigmoid(y)
                if normalise:
                    y = y * (lax.rsqrt(jnp.sum(y * y, axis=-1, keepdims=True) + EPS) * qscale)
                o_ref[r, sl] = y.astype(o_ref.dtype)

    pl.when(j < n_qk)(functools.partial(body, True))
    pl.when(j >= n_qk)(functools.partial(body, False))


def _dn_prep(proj, conv_w, batch, seq, wc=256):
    t = proj.shape[0]
    n = 3 * DN_WIDTH
    return pl.pallas_call(
        _dn_prep_kernel,
        out_shape=jax.ShapeDtypeStruct((t, n), BF16),
        grid=(batch, n // wc),
        in_specs=[pl.BlockSpec((seq, wc), lambda b, j: (b, j)), pl.BlockSpec((DN_CONV, wc), lambda b, j: (0, j))],
        out_specs=pl.BlockSpec((seq, wc), lambda b, j: (b, j)),
        scratch_shapes=[pltpu.VMEM((seq + 16, DN_DIM), F32)],
        compiler_params=_cp(("arbitrary", "arbitrary")),
        name="dn_prep",
    )(proj, conv_w)


def _dn_kernel(q_ref, k_ref, v_ref, z_ref, gc_ref, gr_ref, nw_ref, o_ref,
               u_ref, wq_ref, ak_ref, eg_ref, oo_ref):
    s = q_ref.shape[0]
    nt = s // TL
    nc = s // CH
    heads = range(DN_HPS)
    hl = [slice(h * DN_DIM, (h + 1) * DN_DIM) for h in heads]

    def tile_body(j, carry):
        ri, ci = _tile_masks()
        same64 = (ri >> 6) == (ci >> 6)
        same32 = (ri >> 5) == (ci >> 5)
        same16 = (ri >> 4) == (ci >> 4)
        eye = jnp.where(ri == ci, 1.0, 0.0)
        incl = [same64 & (ri >= ci), same64 & (ri <= ci)]
        strict = [same64 & (ri > ci), same64 & (ri < ci)]
        tiles = [j * DN_TPI + t for t in range(DN_TPI)]
        r0 = [pl.multiple_of(i * TL, TL) for i in tiles]
        ht = [(h, t) for h in heads for t in range(DN_TPI)]
        qn = {(h, t): q_ref[pl.ds(r0[t], TL), hl[h]].astype(F32) for h, t in ht}
        kn = {(h, t): k_ref[pl.ds(r0[t], TL), hl[h]].astype(F32) for h, t in ht}
        vv = {(h, t): v_ref[pl.ds(r0[t], TL), hl[h]].astype(F32) for h, t in ht}
        kk = {x: _mm_nt(kn[x], kn[x]) for x in ht}
        qk = {x: _mm_nt(qn[x], kn[x]) for x in ht}
        gc = {(h, t): gc_ref[h, pl.ds(r0[t], TL), :] for h, t in ht}
        gr = {(h, t): gr_ref[tiles[t]][8 * h:8 * h + 8, :] for h, t in ht}
        ch = [(h, t, d) for h, t in ht for d in (0, 1)]
        beta_c = [gc[h, t][:, d:d + 1] for h, t, d in ch]
        g_c = [gc[h, t][:, 2 + d:3 + d] for h, t, d in ch]
        gt_c = [gc[h, t][:, 4 + d:5 + d] for h, t, d in ch]
        g_r = [gr[h, t][2 + d:3 + d, :] for h, t, d in ch]
        dec = [jnp.exp(jnp.where(incl[d], g_c[n] - g_r[n], NEG)) for n, (h, t, d) in enumerate(ch)]
        a = [jnp.where(strict[d], beta_c[n] * kk[h, t] * dec[n], 0.0) for n, (h, t, d) in enumerate(ch)]
        tinv = _unit_tri_inverse(a, same16, same32, eye)
        eg_c = [jnp.exp(g) for g in g_c]
        rhs = [jnp.concatenate([beta_c[n] * vv[h, t], (beta_c[n] * eg_c[n]) * kn[h, t]], axis=1)
               for n, (h, t, d) in enumerate(ch)]
        uw = [_mm(x, y) for x, y in zip(tinv, rhs)]
        for n, (h, t, d) in enumerate(ch):
            i = tiles[t]
            u_ref[h, d, pl.ds(r0[t], TL), :] = uw[n][:, :DN_DIM]
            w = uw[n][:, DN_DIM:]
            qe = qn[h, t] * eg_c[n]
            attn = qk[h, t] * dec[n]
            kdt = (kn[h, t] * jnp.exp(gt_c[n] - g_c[n])).T
            egt = jnp.exp(gt_c[n])
            for c in range(CPT):
                rows = slice(c * CH, (c + 1) * CH)
                wq_ref[h, d, i * CPT + c] = jnp.concatenate([w[rows], qe[rows]], axis=0).astype(BF16)
                ak_ref[h, d, i * CPT + c] = jnp.concatenate([attn[rows, rows], kdt[:, rows]], axis=0).astype(BF16)
                eg_ref[h, d, i * CPT + c] = jnp.broadcast_to(egt[c * CH:c * CH + 8, :], (8, LANES))
        return carry

    lax.fori_loop(0, nt // DN_TPI, tile_body, 0)

    hd = [(h, d) for h in heads for d in (0, 1)]

    def chunk_body(c, carry):
        states = list(carry)
        ws, vnew, m2 = {}, {}, {}
        cc = [c, nc - 1 - c]
        r = [pl.multiple_of(x * CH, CH) for x in cc]
        for n, (h, d) in enumerate(hd):
            ws[n] = jnp.dot(wq_ref[h, d, cc[d]], states[n].astype(BF16), preferred_element_type=F32)
        for n, (h, d) in enumerate(hd):
            vnew[n] = u_ref[h, d, pl.ds(r[d], CH), :] - ws[n][:CH]
        for n, (h, d) in enumerate(hd):
            m2[n] = jnp.dot(ak_ref[h, d, cc[d]], vnew[n].astype(BF16), preferred_element_type=F32)
        for n, (h, d) in enumerate(hd):
            oo_ref[h, d, pl.ds(r[d], CH), :] = ws[n][CH:] + m2[n][:CH]
            states[n] = eg_ref[h, d, cc[d]][0:1, :] * states[n] + m2[n][CH:]
        return tuple(states)

    z0 = jnp.zeros((DN_DIM, DN_DIM), F32)
    lax.fori_loop(0, nc, chunk_body, tuple(z0 for _ in hd))

    for h in heads:
        for rt in range(s // TL):
            r = slice(rt * TL, (rt + 1) * TL)
            o = oo_ref[h, 0, r, :] + oo_ref[h, 1, r, :]
            o = o * lax.rsqrt(jnp.mean(o * o, axis=-1, keepdims=True) + EPS) * nw_ref[...]
            z = z_ref[r, hl[h]].astype(F32)
            o_ref[r, hl[h]] = (o * (z * _sigmoid(z))).astype(o_ref.dtype)


def _deltanet(proj, conv_w, gc, gr, norm_w, batch, seq):
    t = proj.shape[0]
    nc = seq // CH
    nt = seq // TL
    wb = DN_HPS * DN_DIM
    qkv = _dn_prep(proj, conv_w, batch, seq)
    qb, kb, vb, zb = OFF_DN_Q // wb, OFF_DN_K // wb, OFF_DN_V // wb, OFF_DN_Z // wb
    col = lambda off: pl.BlockSpec((seq, wb), lambda b, h, off=off: (b, off + h))
    return pl.pallas_call(
        _dn_kernel,
        out_shape=jax.ShapeDtypeStruct((t, DN_WIDTH), BF16),
        grid=(batch, DN_HEADS // DN_HPS),
        in_specs=[col(qb), col(kb), col(vb), col(zb),
                  pl.BlockSpec((DN_HPS, seq, LANES), lambda b, h: (h, b, 0)),
                  pl.BlockSpec((nt, 8 * DN_HPS, TL), lambda b, h: (b, h, 0)),
                  pl.BlockSpec((1, DN_DIM), lambda b, h: (0, 0))],
        out_specs=pl.BlockSpec((seq, wb), lambda b, h: (b, h)),
        scratch_shapes=[pltpu.VMEM((DN_HPS, 2, seq, DN_DIM), F32),
                        pltpu.VMEM((DN_HPS, 2, nc, 2 * CH, DN_DIM), BF16),
                        pltpu.VMEM((DN_HPS, 2, nc, CH + DN_DIM, CH), BF16),
                        pltpu.VMEM((DN_HPS, 2, nc, 8, LANES), F32),
                        pltpu.VMEM((DN_HPS, 2, seq, DN_DIM), F32)],
        compiler_params=_cp(("arbitrary", "arbitrary")),
        name="deltanet",
    )(qkv, qkv, qkv, proj, gc, gr, norm_w.reshape(1, DN_DIM))


def _ml_kernel(q_ref, k_ref, v_ref, og_ref, gc_ref, gr_ref, nw_ref, o_ref, oo_ref):
    s = q_ref.shape[0]
    nt = s // TL
    scale = ML_QK ** -0.5

    def tile_body(it, carry):
        m_carry = [carry[0], carry[1]]
        c_carry = [carry[2], carry[3]]
        n_carry = [carry[4], carry[5]]
        ri, ci = _tile_masks()
        same64 = (ri >> 6) == (ci >> 6)
        rowc = lax.broadcasted_iota(I32, (TL, 1), 0) >> 6
        colc = lax.broadcasted_iota(I32, (1, TL), 1) >> 6
        dirs = (0, 1)
        order = [list(range(CPT)), list(range(CPT - 1, -1, -1))]
        tile = [it, nt - 1 - it]
        r0 = [pl.multiple_of(i * TL, TL) for i in tile]
        qb = [q_ref[pl.ds(r, TL), :] for r in r0]
        qs = [x.astype(F32) * scale for x in qb]
        kf = [k_ref[pl.ds(r, TL), :].astype(F32) for r in r0]
        vb = [v_ref[pl.ds(r, TL), :] for r in r0]
        qk = [_mm_nt(a, b) for a, b in zip(qs, kf)]
        gc = [gc_ref[0, pl.ds(r, TL), :] for r in r0]
        gr = [gr_ref[i] for i in tile]
        ig_c = [gc[d][:, d:d + 1] for d in dirs]
        b_c = [gc[d][:, 2 + d:3 + d] for d in dirs]
        bt_c = [gc[d][:, 4 + d:5 + d] for d in dirs]
        ig_r = [gr[d][d:d + 1, :] for d in dirs]
        b_r = [gr[d][2 + d:3 + d, :] for d in dirs]
        bt_r = [gr[d][4 + d:5 + d, :] for d in dirs]
        incl = [same64 & (ri >= ci), same64 & (ri <= ci)]
        dmat = [jnp.where(incl[d], b_c[d] - b_r[d] + ig_r[d], NEG) for d in dirs]
        m_intra = [jnp.max(x, axis=1, keepdims=True) for x in dmat]
        d_end_c = [bt_c[d] - b_c[d] + ig_c[d] for d in dirs]
        d_end_r = [bt_r[d] - b_r[d] + ig_r[d] for d in dirs]
        dmax_c = [jnp.max(jnp.where(same64, x, NEG), axis=1, keepdims=True) for x in d_end_r]
        m_in_c, m_out_c, csc = [], [], []
        for d in dirs:
            m_in = jnp.zeros((TL, 1), F32)
            m_out = jnp.zeros((TL, 1), F32)
            m = m_carry[d]
            cs = {}
            for cc in order[d]:
                btot = bt_c[d][cc * CH:cc * CH + 1, :]
                dmx = dmax_c[d][cc * CH:cc * CH + 1, :]
                m_new = jnp.maximum(btot + m, dmx)
                cs[cc] = jnp.exp(btot + m - m_new)
                m_in = jnp.where(rowc == cc, m, m_in)
                m_out = jnp.where(rowc == cc, m_new, m_out)
                m = m_new
            m_carry[d] = m
            m_in_c.append(m_in)
            m_out_c.append(m_out)
            csc.append(cs)
        m_inter = [b_c[d] + m_in_c[d] for d in dirs]
        m_pos = [jnp.maximum(m_intra[d], m_inter[d]) for d in dirs]
        scores = [qk[d] * jnp.exp(dmat[d] - m_pos[d]) for d in dirs]
        inter_scale = [jnp.exp(m_inter[d] - m_pos[d]) for d in dirs]
        rsum = [jnp.sum(x, axis=1, keepdims=True) for x in scores]
        einv = [jnp.exp(-x) for x in m_pos]
        wk = [kf[d] * jnp.exp(d_end_c[d] - m_out_c[d]) for d in dirs]
        wkt = [x.T for x in wk]
        intra = [jnp.dot(scores[d].astype(BF16), vb[d], preferred_element_type=F32) for d in dirs]
        wstack = [jnp.concatenate([jnp.where(colc == c, wkt[d], 0.0) for c in range(CPT)], axis=0).astype(BF16)
                  for d in dirs]
        dcs = [jnp.dot(wstack[d], vb[d], preferred_element_type=F32) for d in dirs]
        c_at, n_row = [], []
        for d in dirs:
            cst, nst = c_carry[d], n_carry[d]
            c_start = {}
            nr = jnp.zeros((TL, ML_QK), F32)
            for cc in order[d]:
                c_start[cc] = cst
                nr = jnp.where(rowc == cc, nst, nr)
                dn = jnp.sum(wk[d][cc * CH:(cc + 1) * CH], axis=0, keepdims=True)
                cst = csc[d][cc] * cst + dcs[d][cc * ML_QK:(cc + 1) * ML_QK]
                nst = csc[d][cc] * nst + dn
            c_carry[d], n_carry[d] = cst, nst
            c_at.append(jnp.concatenate([c_start[c] for c in range(CPT)], axis=0).astype(BF16))
            n_row.append(nr)
        q4 = [jnp.concatenate([jnp.where(rowc == c, qb[d], jnp.zeros_like(qb[d])) for c in range(CPT)], axis=1)
              for d in dirs]
        inter = [jnp.dot(q4[d], c_at[d], preferred_element_type=F32) * scale for d in dirs]
        for d in dirs:
            qn = jnp.sum(qs[d] * n_row[d], axis=1, keepdims=True)
            num = intra[d] + inter_scale[d] * inter[d]
            den = rsum[d] + inter_scale[d] * qn
            oo_ref[d, pl.ds(r0[d], TL), :] = num / jnp.maximum(jnp.abs(den), einv[d])
        return (m_carry[0], m_carry[1], c_carry[0], c_carry[1], n_carry[0], n_carry[1])

    zero11 = jnp.zeros((1, 1), F32)
    c0 = jnp.zeros((ML_QK, ML_V), F32)
    n0 = jnp.zeros((1, ML_QK), F32)
    lax.fori_loop(0, nt, tile_body, (zero11, zero11, c0, c0, n0, n0))

    for rt in range(s // TL):
        r = slice(rt * TL, (rt + 1) * TL)
        o = oo_ref[0, r, :] + oo_ref[1, r, :]
        o = o * lax.rsqrt(jnp.mean(o * o, axis=-1, keepdims=True) + EPS) * nw_ref[...]
        o_ref[r, :] = (o * _sigmoid(og_ref[r, :].astype(F32))).astype(o_ref.dtype)


def _mlstm(proj, gc, gr, norm_w, batch, seq):
    t = proj.shape[0]
    nt = seq // TL
    qb, kb, vb, ob = OFF_ML_Q // ML_QK, OFF_ML_K // ML_QK, OFF_ML_V // ML_V, OFF_ML_O // ML_V
    return pl.pallas_call(
        _ml_kernel,
        out_shape=jax.ShapeDtypeStruct((t, ML_WIDTH), BF16),
        grid=(batch, ML_HEADS),
        in_specs=[pl.BlockSpec((seq, ML_QK), lambda b, h: (b, qb + h)),
                  pl.BlockSpec((seq, ML_QK), lambda b, h: (b, kb + h)),
                  pl.BlockSpec((seq, ML_V), lambda b, h: (b, vb + h)),
                  pl.BlockSpec((seq, ML_V), lambda b, h: (b, ob + h)),
                  pl.BlockSpec((1, seq, LANES), lambda b, h: (DN_HEADS + h, b, 0)),
                  pl.BlockSpec((nt, 8, TL), lambda b, h: (b, LANES // 8 + h, 0)),
                  pl.BlockSpec((1, ML_V), lambda b, h: (0, h))],
        out_specs=pl.BlockSpec((seq, ML_V), lambda b, h: (b, h)),
        scratch_shapes=[pltpu.VMEM((2, seq, ML_V), F32)],
        compiler_params=_cp(("arbitrary", "arbitrary")),
        name="mlstm",
    )(proj, proj, proj, proj, gc, gr, norm_w.reshape(1, ML_WIDTH))


def _sc_kernel(b_ref, c_ref, x_ref, w_ref, o_ref, cx_ref, pad_ref):
    cx_ref[...] = c_ref[...].astype(F32) * x_ref[...].astype(F32)
    y = _conv_centred(cx_ref, w_ref, pad_ref, SC_CONV)
    o_ref[...] = (b_ref[...].astype(F32) * y).astype(o_ref.dtype)


def _shortconv(proj, conv_w, batch, seq, wc=256):
    t = proj.shape[0]
    bb, cb, xb = OFF_SC_B // wc, OFF_SC_C // wc, OFF_SC_X // wc
    col = lambda off: pl.BlockSpec((seq, wc), lambda b, j, off=off: (b, off + j))
    return pl.pallas_call(
        _sc_kernel,
        out_shape=jax.ShapeDtypeStruct((t, SC_WIDTH), BF16),
        grid=(batch, SC_WIDTH // wc),
        in_specs=[col(bb), col(cb), col(xb), pl.BlockSpec((SC_CONV, wc), lambda b, j: (0, j))],
        out_specs=pl.BlockSpec((seq, wc), lambda b, j: (b, j)),
        scratch_shapes=[pltpu.VMEM((seq, wc), F32), pltpu.VMEM((seq + 16, wc), F32)],
        compiler_params=_cp(("arbitrary", "arbitrary")),
        name="shortconv",
    )(proj, proj, proj, conv_w)


def _pack_gate_cols(dn_b, dn_a, ml_i, ml_f):
    r = dn_b.shape[0]

    def grp(lo, hi, nh):
        a = jnp.stack([lo[:, :nh], lo[:, nh:], hi[:, :nh], hi[:, nh:]], axis=-1)
        a = jnp.concatenate([a, jnp.zeros((r, nh, 4), a.dtype)], axis=-1).reshape(r, nh * 8)
        return jnp.concatenate([a, jnp.zeros((r, LANES - nh * 8), a.dtype)], axis=-1)

    return jnp.concatenate([grp(dn_b, dn_a, DN_HEADS), grp(ml_i, ml_f, ML_HEADS)], axis=-1)


def _gate_params(dn_a_log, dn_dt_bias, ml_igate_b, ml_fgate_b):
    z_dn = jnp.zeros((1, 2 * DN_HEADS), F32)
    z_ml = jnp.zeros((1, 2 * ML_HEADS), F32)
    flat = lambda a: a.astype(F32).reshape(1, -1)
    rows = [_pack_gate_cols(z_dn, flat(dn_a_log), z_ml, z_ml),
            _pack_gate_cols(z_dn, flat(dn_dt_bias), z_ml, z_ml),
            _pack_gate_cols(z_dn, z_dn, flat(ml_igate_b), z_ml),
            _pack_gate_cols(z_dn, z_dn, z_ml, flat(ml_fgate_b))]
    return jnp.concatenate(rows + [jnp.zeros((4, 2 * LANES), F32)], axis=0)


def _outproj_kernel(a1_ref, a2_ref, a3_ref, w1_ref, w2_ref, w3_ref, x_ref, o_ref):
    acc = jnp.dot(a1_ref[...], w1_ref[0].astype(BF16), preferred_element_type=F32)
    acc += jnp.dot(a2_ref[...], w2_ref[0].astype(BF16), preferred_element_type=F32)
    acc += jnp.dot(a3_ref[...], w3_ref[0].astype(BF16), preferred_element_type=F32)
    o_ref[...] = x_ref[...] + acc


def _outproj(dn, ml, sc, w_out, layer, x, tm=1024, tn=512):
    t, d = x.shape
    k1, k2, k3 = dn.shape[1], ml.shape[1], sc.shape[1]
    assert k1 == k2 and (k1 + k2) % k3 == 0
    return pl.pallas_call(
        _outproj_kernel,
        out_shape=jax.ShapeDtypeStruct((t, d), F32),
        grid=(t // tm, d // tn),
        in_specs=[pl.BlockSpec((tm, k1), lambda i, j: (i, 0)),
                  pl.BlockSpec((tm, k2), lambda i, j: (i, 0)),
                  pl.BlockSpec((tm, k3), lambda i, j: (i, 0)),
                  pl.BlockSpec((1, k1, tn), lambda i, j: (layer, 0, j)),
                  pl.BlockSpec((1, k2, tn), lambda i, j: (layer, 1, j)),
                  pl.BlockSpec((1, k3, tn), lambda i, j: (layer, (k1 + k2) // k3, j)),
                  pl.BlockSpec((tm, tn), lambda i, j: (i, j))],
        out_specs=pl.BlockSpec((tm, tn), lambda i, j: (i, j)),
        compiler_params=_cp(("arbitrary", "arbitrary")),
        name="outproj",
    )(dn, ml, sc, w_out, w_out, w_out, x)


ROUTE_TM = 256
EXP_ROW0 = 32
HALF_D = D_MODEL // 2


def _router_kernel(x_ref, nw_ref, wr_ref, br_ref, hp_ref, meta_ref, wc_ref, cnt_ref, base_ref,
                   whi_ref, wlo_ref):
    tm = x_ref.shape[0]

    @pl.when(pl.program_id(0) == 0)
    def _():
        base_ref[...] = jnp.zeros(base_ref.shape, F32)
        w = wr_ref[...]
        whi_ref[...] = w.astype(BF16)
        wlo_ref[...] = (w - whi_ref[...].astype(F32)).astype(BF16)

    x = x_ref[...]
    h = x * lax.rsqrt(jnp.mean(x * x, axis=-1, keepdims=True) + EPS) * nw_ref[...]
    bits = pltpu.bitcast(h.astype(BF16).astype(F32), U32)
    hp_ref[...] = (bits[:, :HALF_D] >> 16) | bits[:, HALF_D:]

    h_hi = h.astype(BF16)
    h_lo = (h - h_hi.astype(F32)).astype(BF16)
    logits = (jnp.dot(h_hi, whi_ref[...], preferred_element_type=F32)
              + jnp.dot(h_lo, whi_ref[...], preferred_element_type=F32)
              + jnp.dot(h_hi, wlo_ref[...], preferred_element_type=F32))
    lt = (logits + br_ref[...]).T
    row8 = lax.broadcasted_iota(I32, (8, tm), 0)
    gl = jnp.where(row8 < N_GROUPS, lt[0:8, :], NEG)
    gex = jnp.exp(gl - jnp.max(gl, axis=0, keepdims=True))
    gp = gex / jnp.sum(gex, axis=0, keepdims=True)
    g_w = jnp.max(gp, axis=0, keepdims=True)
    g_idx = jnp.min(jnp.where(gp == g_w, row8, 8), axis=0, keepdims=True)
    el = lt[EXP_ROW0 + 24:EXP_ROW0 + 32, :]
    for g in (2, 1, 0):
        el = jnp.where(g_idx == g, lt[EXP_ROW0 + 8 * g:EXP_ROW0 + 8 * g + 8, :], el)
    ee = jnp.exp(el - jnp.max(el, axis=0, keepdims=True))
    p = ee / jnp.sum(ee, axis=0, keepdims=True)
    p1 = jnp.max(p, axis=0, keepdims=True)
    i1 = jnp.min(jnp.where(p == p1, row8, 8), axis=0, keepdims=True)
    pm = jnp.where(row8 == i1, -1.0, p)
    p2 = jnp.max(pm, axis=0, keepdims=True)
    i2 = jnp.min(jnp.where(pm == p2, row8, 8), axis=0, keepdims=True)
    den = p1 + p2
    w1 = g_w * p1 / den
    w2 = g_w * p2 / den
    e1 = g_idx * EXPERTS_PER_GROUP + i1
    e2 = g_idx * EXPERTS_PER_GROUP + i2

    rowe = lax.broadcasted_iota(I32, (N_EXPERTS, tm), 0)
    oh1 = jnp.where(rowe == e1, 1.0, 0.0)
    oh2 = jnp.where(rowe == e2, 1.0, 0.0)
    oh = (oh1 + oh2).astype(BF16)
    ri = lax.broadcasted_iota(I32, (tm, tm), 0)
    ci = lax.broadcasted_iota(I32, (tm, tm), 1)
    before = jnp.where(ri < ci, 1.0, 0.0).astype(BF16)
    ones = jnp.ones((tm, tm), BF16)
    tot = jnp.dot(oh, before, preferred_element_type=F32) + base_ref[...]
    r1 = jnp.sum(oh1 * tot, axis=0, keepdims=True)
    r2 = jnp.sum(oh2 * tot, axis=0, keepdims=True)
    base_ref[...] = base_ref[...] + jnp.dot(oh, ones, preferred_element_type=F32)
    cnt_ref[...] = base_ref[...]

    meta_ref[...] = jnp.concatenate(
        [e1, e2, r1.astype(I32), r2.astype(I32), jnp.zeros((4, tm), I32)], axis=0)
    wrow = lax.broadcasted_iota(I32, (LANES, tm), 0)
    wc_ref[...] = jnp.where(wrow == 0, w1, jnp.where(wrow == 1, w2, 0.0)).T


def _router(x, norm_w, wr, br):
    t, d = x.shape
    tm = ROUTE_TM
    return pl.pallas_call(
        _router_kernel,
        out_shape=(jax.ShapeDtypeStruct((t, HALF_D), U32),
                   jax.ShapeDtypeStruct((8, t), I32),
                   jax.ShapeDtypeStruct((t, LANES), F32),
                   jax.ShapeDtypeStruct((N_EXPERTS, tm), F32)),
        grid=(t // tm,),
        in_specs=[pl.BlockSpec((tm, d), lambda i: (i, 0)),
                  pl.BlockSpec((1, d), lambda i: (0, 0)),
                  pl.BlockSpec((d, LANES), lambda i: (0, 0)),
                  pl.BlockSpec((1, LANES), lambda i: (0, 0))],
        out_specs=(pl.BlockSpec((tm, HALF_D), lambda i: (i, 0)),
                   pl.BlockSpec((8, tm), lambda i: (0, i)),
                   pl.BlockSpec((tm, LANES), lambda i: (i, 0)),
                   pl.BlockSpec((N_EXPERTS, tm), lambda i: (0, 0))),
        scratch_shapes=[pltpu.VMEM((N_EXPERTS, tm), F32),
                        pltpu.VMEM((d, LANES), BF16), pltpu.VMEM((d, LANES), BF16)],
        compiler_params=_cp(("arbitrary",)),
        name="router",
    )(x, norm_w.reshape(1, d), wr, br)


SCAT_TM = 256


def _scatter_kernel(zs_ref, ze_ref, nu_ref, s1_ref, s2_ref, hp_ref, xs_ref, zero_ref, sem):
    i = pl.program_id(0)
    tm = hp_ref.shape[0]
    n_all = xs_ref.shape[0] // MOE_BLK

    @pl.when(i == 0)
    def _():
        zero_ref[...] = jnp.zeros(zero_ref.shape, zero_ref.dtype)

        def zfill(e, c):
            zero_row = lambda r: pltpu.make_async_copy(zero_ref.at[pl.ds(0, 1), :], xs_ref.at[pl.ds(r, 1), :], sem)

            def one(r, c2):
                zero_row(r).start()
                return c2

            def one_wait(r, c2):
                zero_row(r).wait()
                return c2

            lax.fori_loop(zs_ref[e], ze_ref[e], one, 0)
            lax.fori_loop(zs_ref[e], ze_ref[e], one_wait, 0)
            return c

        def zero_block(b):
            r0 = pl.multiple_of(b * MOE_BLK, MOE_BLK)
            return pltpu.make_async_copy(zero_ref, xs_ref.at[pl.ds(r0, MOE_BLK), :], sem)

        def tfill(b, c):
            zero_block(b).start()
            return c

        def twait(b, c):
            zero_block(b).wait()
            return c

        lax.fori_loop(0, N_EXPERTS, zfill, 0)
        lax.fori_loop(nu_ref[0], n_all, tfill, 0)
        lax.fori_loop(nu_ref[0], n_all, twait, 0)

    def row_copies(t):
        src = hp_ref.at[pl.ds(t, 1), :]
        return (pltpu.make_async_copy(src, xs_ref.at[pl.ds(s1_ref[0, 0, t], 1), :], sem),
                pltpu.make_async_copy(src, xs_ref.at[pl.ds(s2_ref[0, 0, t], 1), :], sem))

    def issue(t, c):
        for cp in row_copies(t):
            cp.start()
        return c

    def drain(t, c):
        for cp in row_copies(t):
            cp.wait()
        return c

    lax.fori_loop(0, tm, issue, 0, unroll=8)
    lax.fori_loop(0, tm, drain, 0, unroll=8)


def _scatter_rows(zero_start, zero_end, n_used, slot1, slot2, hp, n_rows):
    t, w = hp.shape
    tm = SCAT_TM
    s3 = lambda a: a.reshape(t // tm, 1, tm)
    grid_spec = pltpu.PrefetchScalarGridSpec(
        num_scalar_prefetch=3,
        grid=(t // tm,),
        in_specs=[pl.BlockSpec((1, 1, tm), lambda i, zs, ze, nu: (i, 0, 0), memory_space=pltpu.SMEM),
                  pl.BlockSpec((1, 1, tm), lambda i, zs, ze, nu: (i, 0, 0), memory_space=pltpu.SMEM),
                  pl.BlockSpec((tm, w), lambda i, zs, ze, nu: (i, 0))],
        out_specs=pl.BlockSpec(memory_space=pl.ANY),
        scratch_shapes=[pltpu.VMEM((MOE_BLK, w), U32), pltpu.SemaphoreType.DMA(())],
    )
    return pl.pallas_call(
        _scatter_kernel,
        out_shape=jax.ShapeDtypeStruct((n_rows, w), U32),
        grid_spec=grid_spec,
        compiler_params=_cp(("arbitrary",)),
        name="scatter_rows",
    )(zero_start, zero_end, n_used, s3(slot1), s3(slot2), hp)


FF_TN = 256


def _unpack_rows(x_ref, xs_ref):
    u = x_ref[...]
    xs_ref[:, :HALF_D] = pltpu.bitcast(u << 16, F32).astype(BF16)
    xs_ref[:, HALF_D:] = pltpu.bitcast(u & jnp.uint32(0xFFFF0000), F32).astype(BF16)


def _ffn_up_kernel(be_ref, nxt_ref, par_ref, nu_ref, x_ref, wg_hbm, wu_hbm, o_ref,
                   wbuf_ref, wgb_ref, wub_ref, xs_ref, sem, *, layer):
    j = pl.program_id(0)
    i = pl.program_id(1)

    def fetch(e, p):
        cols = pl.ds(pl.multiple_of(j * FF_TN, FF_TN), FF_TN)
        return (pltpu.make_async_copy(wg_hbm.at[layer, e, :, cols], wbuf_ref.at[p, 0], sem.at[p, 0]),
                pltpu.make_async_copy(wu_hbm.at[layer, e, :, cols], wbuf_ref.at[p, 1], sem.at[p, 1]))

    @pl.when(i < nu_ref[0])
    def _():
        e = be_ref[i]
        p = par_ref[i]

        @pl.when(i == 0)
        def _():
            for cp in fetch(e, p):
                cp.start()

        @pl.when((i == 0) | (e != be_ref[jnp.maximum(i - 1, 0)]))
        def _():
            for cp in fetch(e, p):
                cp.wait()
            wgb_ref[...] = wbuf_ref[p, 0].astype(BF16)
            wub_ref[...] = wbuf_ref[p, 1].astype(BF16)

            @pl.when(nxt_ref[i] != e)
            def _():
                for cp in fetch(nxt_ref[i], 1 - p):
                    cp.start()

        _unpack_rows(x_ref, xs_ref)
        x = xs_ref[...]
        g = jnp.dot(x, wgb_ref[...], preferred_element_type=F32)
        u = jnp.dot(x, wub_ref[...], preferred_element_type=F32)
        o_ref[...] = (g * _sigmoid(g) * u).astype(o_ref.dtype)

    @pl.when(i >= nu_ref[0])
    def _():
        o_ref[...] = jnp.zeros(o_ref.shape, o_ref.dtype)


def _ffn_up(block_exp, next_exp, slot_par, n_used, xs, w_gate, w_up, layer, n_blocks):
    d, ff = w_gate.shape[2], w_gate.shape[3]
    grid_spec = pltpu.PrefetchScalarGridSpec(
        num_scalar_prefetch=4,
        grid=(ff // FF_TN, n_blocks),
        in_specs=[pl.BlockSpec((MOE_BLK, HALF_D), lambda j, i, be, nx, pr, nu: (jnp.minimum(i, nu[0] - 1), 0)),
                  pl.BlockSpec(memory_space=pl.ANY),
                  pl.BlockSpec(memory_space=pl.ANY)],
        out_specs=pl.BlockSpec((MOE_BLK, FF_TN), lambda j, i, be, nx, pr, nu: (i, j)),
        scratch_shapes=[pltpu.VMEM((2, 2, d, FF_TN), F32),
                        pltpu.VMEM((d, FF_TN), BF16), pltpu.VMEM((d, FF_TN), BF16),
                        pltpu.VMEM((MOE_BLK, d), BF16),
                        pltpu.SemaphoreType.DMA((2, 2))],
    )
    return pl.pallas_call(
        functools.partial(_ffn_up_kernel, layer=layer),
        out_shape=jax.ShapeDtypeStruct((n_blocks * MOE_BLK, ff), BF16),
        grid_spec=grid_spec,
        compiler_params=_cp(("arbitrary", "arbitrary")),
        name="ffn_up",
    )(block_exp, next_exp, slot_par, n_used, xs, w_gate, w_up)


def _ffn_down_kernel(be_ref, nxt_ref, par_ref, nu_ref, h_ref, wd_hbm, y_ref, wbuf_ref, wdb_ref, sem, *, layer):
    i = pl.program_id(0)

    def fetch(e, p):
        return pltpu.make_async_copy(wd_hbm.at[layer, e], wbuf_ref.at[p], sem.at[p])

    @pl.when(i < nu_ref[0])
    def _():
        e = be_ref[i]
        p = par_ref[i]

        @pl.when(i == 0)
        def _():
            fetch(e, p).start()

        @pl.when((i == 0) | (e != be_ref[jnp.maximum(i - 1, 0)]))
        def _():
            fetch(e, p).wait()
            wdb_ref[...] = wbuf_ref[p].astype(BF16)

            @pl.when(nxt_ref[i] != e)
            def _():
                fetch(nxt_ref[i], 1 - p).start()

        y = jnp.dot(h_ref[...], wdb_ref[...], preferred_element_type=F32)
        bits = pltpu.bitcast(y.astype(BF16).astype(F32), U32)
        half = bits.shape[1] // 2
        y_ref[...] = (bits[:, :half] >> 16) | bits[:, half:]

    @pl.when(i >= nu_ref[0])
    def _():
        y_ref[...] = jnp.zeros(y_ref.shape, y_ref.dtype)


def _ffn_down(block_exp, next_exp, slot_par, n_used, hmid, w_down, layer, n_blocks):
    ff, d = w_down.shape[2], w_down.shape[3]
    grid_spec = pltpu.PrefetchScalarGridSpec(
        num_scalar_prefetch=4,
        grid=(n_blocks,),
        in_specs=[pl.BlockSpec((MOE_BLK, ff), lambda i, be, nx, pr, nu: (jnp.minimum(i, nu[0] - 1), 0)),
                  pl.BlockSpec(memory_space=pl.ANY)],
        out_specs=pl.BlockSpec((MOE_BLK, d // 2), lambda i, be, nx, pr, nu: (i, 0)),
        scratch_shapes=[pltpu.VMEM((2, ff, d), F32), pltpu.VMEM((ff, d), BF16),
                        pltpu.SemaphoreType.DMA((2,))],
    )
    return pl.pallas_call(
        functools.partial(_ffn_down_kernel, layer=layer),
        out_shape=jax.ShapeDtypeStruct((n_blocks * MOE_BLK, d // 2), U32),
        grid_spec=grid_spec,
        compiler_params=_cp(("arbitrary",)),
        name="ffn_down",
    )(block_exp, next_exp, slot_par, n_used, hmid, w_down)


COMB_TM = 256


def _combine_kernel(s1_ref, s2_ref, x_ref, wc_ref, fw_ref, y_ref, *rest, final):
    o_ref, h_ref = (rest[0], None) if final else rest[:2]
    ya_ref, yb_ref, sem = rest[-3:]
    tm = x_ref.shape[0]

    def row_copies(t):
        return (pltpu.make_async_copy(y_ref.at[pl.ds(s1_ref[0, 0, t], 1), :], ya_ref.at[pl.ds(t, 1), :], sem),
                pltpu.make_async_copy(y_ref.at[pl.ds(s2_ref[0, 0, t], 1), :], yb_ref.at[pl.ds(t, 1), :], sem))

    def issue(t, c):
        for cp in row_copies(t):
            cp.start()
        return c

    def drain(t, c):
        for cp in row_copies(t):
            cp.wait()
        return c

    lax.fori_loop(0, tm, issue, 0, unroll=8)
    lax.fori_loop(0, tm, drain, 0, unroll=8)
    w1, w2 = wc_ref[:, 0:1], wc_ref[:, 1:2]
    ua, ub = ya_ref[...], yb_ref[...]
    hi = jnp.uint32(0xFFFF0000)
    lo_sum = w1 * pltpu.bitcast(ua << 16, F32) + w2 * pltpu.bitcast(ub << 16, F32)
    hi_sum = w1 * pltpu.bitcast(ua & hi, F32) + w2 * pltpu.bitcast(ub & hi, F32)
    o = x_ref[...] + jnp.concatenate([lo_sum, hi_sum], axis=1)
    normed = o * lax.rsqrt(jnp.mean(o * o, axis=-1, keepdims=True) + EPS) * fw_ref[...]
    if final:
        o_ref[...] = normed
    else:
        o_ref[...] = o
        h_ref[...] = normed.astype(h_ref.dtype)


def _combine(slot1, slot2, x, wcol, y, norm_w, final):
    t, d = x.shape
    tm = COMB_TM
    s3 = lambda a: a.reshape(t // tm, 1, tm)
    row_blk = pl.BlockSpec((tm, d), lambda i: (i, 0))
    out_f32 = jax.ShapeDtypeStruct((t, d), F32)
    return pl.pallas_call(
        functools.partial(_combine_kernel, final=final),
        out_shape=out_f32 if final else (out_f32, jax.ShapeDtypeStruct((t, d), BF16)),
        grid=(t // tm,),
        in_specs=[pl.BlockSpec((1, 1, tm), lambda i: (i, 0, 0), memory_space=pltpu.SMEM),
                  pl.BlockSpec((1, 1, tm), lambda i: (i, 0, 0), memory_space=pltpu.SMEM),
                  pl.BlockSpec((tm, d), lambda i: (i, 0)),
                  pl.BlockSpec((tm, LANES), lambda i: (i, 0)),
                  pl.BlockSpec((1, d), lambda i: (0, 0)),
                  pl.BlockSpec(memory_space=pl.ANY)],
        out_specs=row_blk if final else (row_blk, row_blk),
        scratch_shapes=[pltpu.VMEM((tm, d // 2), U32), pltpu.VMEM((tm, d // 2), U32),
                        pltpu.SemaphoreType.DMA(())],
        compiler_params=_cp(("arbitrary",)),
        name="combine",
    )(s3(slot1), s3(slot2), x, wcol, norm_w.reshape(1, d), y)


def _mixer_layer(xf, h, batch, seq, w_in, dn_conv_w, dn_a_log, dn_dt_bias, dn_norm_w, ml_igate_b,
                 ml_fgate_b, ml_norm_w, sc_conv_w, w_out, layer):
    c0 = 3 * DN_WIDTH + DN_WIDTH
    c1 = c0 + 4 * DN_HEADS
    c2 = c1 + 2 * ML_HEADS * ML_QK + 2 * ML_WIDTH
    c3 = c2 + 4 * ML_HEADS
    w_main = jnp.concatenate([w_in[:, :c0], w_in[:, c1:c2], w_in[:, c3:]], axis=1).astype(BF16)
    w_gate = _pack_gate_cols(w_in[:, c0:c0 + 2 * DN_HEADS], w_in[:, c0 + 2 * DN_HEADS:c1],
                             w_in[:, c2:c2 + 2 * ML_HEADS], w_in[:, c2 + 2 * ML_HEADS:c3]).astype(BF16)
    proj, graw = _inproj(h, w_main, w_gate)
    gc, gr = _gate_prep(graw, _gate_params(dn_a_log, dn_dt_bias, ml_igate_b, ml_fgate_b))
    dn = _deltanet(proj, dn_conv_w, gc, gr, dn_norm_w, batch, seq)
    ml = _mlstm(proj, gc, gr, ml_norm_w, batch, seq)
    sc = _shortconv(proj, sc_conv_w, batch, seq)
    return _outproj(dn, ml, sc, w_out, layer, xf)


def _moe_layer(xf, norm_w, rg_w, rg_b, re_w, re_b, w_gate, w_up, w_down, layer, out_norm_w, final):
    t, d = xf.shape
    def lanes(g, e):
        r = g.shape[0]
        return jnp.concatenate([g, jnp.zeros((r, EXP_ROW0 - N_GROUPS), F32), e,
                                jnp.zeros((r, LANES - EXP_ROW0 - N_EXPERTS), F32)], axis=1)

    wr = lanes(rg_w, re_w)
    br = lanes(rg_b.reshape(1, -1), re_b.reshape(1, -1))
    hp, meta, wcol, cnt = _router(xf, norm_w, wr, br)
    counts = cnt[:, 0].astype(I32)
    padded = (counts + MOE_BLK - 1) // MOE_BLK * MOE_BLK
    pad_end = jnp.cumsum(padded)
    pad_start = pad_end - padded
    n_blocks = -(-(2 * t + N_EXPERTS * (MOE_BLK - 1)) // MOE_BLK)
    n_used = (pad_end[-1:] // MOE_BLK).astype(I32)
    blk_row = jnp.arange(n_blocks, dtype=I32)[:, None] * MOE_BLK
    block_exp = jnp.minimum(jnp.sum((pad_end[None, :] <= blk_row).astype(I32), axis=1), N_EXPERTS - 1)
    slot1 = pad_start[meta[0]] + meta[2]
    slot2 = pad_start[meta[1]] + meta[3]
    xs = _scatter_rows(pad_start + counts, pad_end, n_used, slot1, slot2, hp, n_blocks * MOE_BLK)
    blk_i = jnp.arange(n_blocks, dtype=I32)
    slot_par = (jnp.cumsum((block_exp != jnp.roll(block_exp, 1)).astype(I32).at[0].set(0)) & 1).astype(I32)
    later = (block_exp[None, :] > block_exp[:, None]) & (blk_i[None, :] < n_used[0])
    next_exp = jnp.min(jnp.where(later, block_exp[None, :], N_EXPERTS), axis=1)
    next_exp = jnp.where(next_exp == N_EXPERTS, block_exp, next_exp).astype(I32)
    hmid = _ffn_up(block_exp, next_exp, slot_par, n_used, xs, w_gate, w_up, layer, n_blocks)
    y = _ffn_down(block_exp, next_exp, slot_par, n_used, hmid, w_down, layer, n_blocks)
    return _combine(slot1, slot2, xf, wcol, y, out_norm_w, final)


def kernel(x, norm_mix_w, w_in, dn_conv_w, dn_a_log, dn_dt_bias, dn_norm_w, ml_igate_b, ml_fgate_b, ml_norm_w,
           sc_conv_w, w_out, norm_ffn_w, router_group_w, router_group_b, router_expert_w, router_expert_b,
           expert_w_gate, expert_w_up, expert_w_down, final_norm_w):
    batch, seq, d = x.shape
    depth = w_in.shape[0]
    xf = x.reshape(batch * seq, d)
    h = _rmsnorm(xf, norm_mix_w[0], BF16)
    for l in range(depth):
        last = l == depth - 1
        xf = _mixer_layer(xf, h, batch, seq, w_in[l], dn_conv_w[l], dn_a_log[l], dn_dt_bias[l],
                          dn_norm_w[l], ml_igate_b[l], ml_fgate_b[l], ml_norm_w[l], sc_conv_w[l], w_out, l)
        out = _moe_layer(xf, norm_ffn_w[l], router_group_w[l], router_group_b[l], router_expert_w[l],
                         router_expert_b[l], expert_w_gate, expert_w_up, expert_w_down, l,
                         final_norm_w if last else norm_mix_w[l + 1], last)
        xf, h = (out, None) if last else out
    return xf.reshape(batch, seq, d)
```

```python
import functools

import numpy as np
import jax
import jax.numpy as jnp
from jax import lax
from jax.experimental import pallas as pl
from jax.experimental.pallas import tpu as pltpu

F32 = jnp.float32
BF16 = jnp.bfloat16
I32 = jnp.int32
U32 = jnp.uint32

D_MODEL = 4096
DN_HEADS = 12
DN_DIM = 128
DN_WIDTH = DN_HEADS * DN_DIM
DN_CONV = 5
ML_HEADS = 6
ML_QK = 128
ML_V = 256
ML_WIDTH = ML_HEADS * ML_V
SC_WIDTH = 1024
SC_CONV = 3
N_GROUPS = 4
EXPERTS_PER_GROUP = 8
N_EXPERTS = N_GROUPS * EXPERTS_PER_GROUP
D_FF = 768
EPS = 1e-6

LANES = 128
SUBLANES = 8
VMEM_LIMIT = 52 * 1024 * 1024

CH = 64
TL = 256
CPT = TL // CH
DN_TPI = 1
DN_HPS = 2
N_HEADS_ALL = DN_HEADS + ML_HEADS
MOE_BLK = 256
NEG = -1e30

OFF_DN_Q, OFF_DN_K, OFF_DN_V, OFF_DN_Z = 0, 1536, 3072, 4608
OFF_ML_Q, OFF_ML_K, OFF_ML_V, OFF_ML_O = 6144, 6912, 7680, 9216
OFF_SC_B, OFF_SC_C, OFF_SC_X = 10752, 11776, 12800
N_MAIN = 13824


def _cp(sem, vmem=VMEM_LIMIT):
    return pltpu.CompilerParams(dimension_semantics=sem, vmem_limit_bytes=vmem)


def _mm(a, b):
    return jnp.dot(a.astype(BF16), b.astype(BF16), preferred_element_type=F32)


def _mm_nt(a, b):
    return lax.dot_general(a.astype(BF16), b.astype(BF16), (((1,), (1,)), ((), ())),
                           preferred_element_type=F32)


def _mm_f32(a, b):
    return jnp.dot(a, b, preferred_element_type=F32, precision=lax.Precision.HIGHEST)


def _sigmoid(x):
    return 1.0 / (1.0 + jnp.exp(-x))


def _softplus(x):
    return jnp.maximum(x, 0.0) + jnp.log(1.0 + jnp.exp(-jnp.abs(x)))


def _rms_kernel(x_ref, w_ref, o_ref):
    x = x_ref[...]
    ms = jnp.mean(x * x, axis=-1, keepdims=True)
    o_ref[...] = (x * lax.rsqrt(ms + EPS) * w_ref[...]).astype(o_ref.dtype)


def _rmsnorm(x2d, w, out_dtype, tm=256):
    t, d = x2d.shape
    return pl.pallas_call(
        _rms_kernel,
        out_shape=jax.ShapeDtypeStruct((t, d), out_dtype),
        grid=(t // tm,),
        in_specs=[pl.BlockSpec((tm, d), lambda i: (i, 0)), pl.BlockSpec((1, d), lambda i: (0, 0))],
        out_specs=pl.BlockSpec((tm, d), lambda i: (i, 0)),
        compiler_params=_cp(("arbitrary",)),
        name="rmsnorm",
    )(x2d, w.reshape(1, d))


def _inproj_kernel(a_ref, w_ref, wg_ref, o_ref, g_ref):
    a = a_ref[...]
    o_ref[...] = jnp.dot(a, w_ref[...], preferred_element_type=F32).astype(o_ref.dtype)

    @pl.when(pl.program_id(1) == 0)
    def _():
        g_ref[...] = jnp.dot(a, wg_ref[...], preferred_element_type=F32)


def _inproj(h, w_main, w_gate, tm=1024, tn=512):
    t, d = h.shape
    n = w_main.shape[1]
    ng = w_gate.shape[1]
    return pl.pallas_call(
        _inproj_kernel,
        out_shape=(jax.ShapeDtypeStruct((t, n), BF16), jax.ShapeDtypeStruct((t, ng), F32)),
        grid=(t // tm, n // tn),
        in_specs=[pl.BlockSpec((tm, d), lambda i, j: (i, 0)),
                  pl.BlockSpec((d, tn), lambda i, j: (0, j)),
                  pl.BlockSpec((d, ng), lambda i, j: (0, 0))],
        out_specs=(pl.BlockSpec((tm, tn), lambda i, j: (i, j)),
                   pl.BlockSpec((tm, ng), lambda i, j: (i, 0))),
        compiler_params=_cp(("arbitrary", "arbitrary")),
        name="inproj",
    )(h, w_main, w_gate)


def _gate_kernel(raw_ref, p_ref, gc_ref, gr_ref):
    x = raw_ref[...]
    lane = lax.broadcasted_iota(I32, x.shape, 1)
    chn = lane & 7
    is_ml = lane >= LANES
    a_log, dt_b, ig_b, fg_b = p_ref[0:1, :], p_ref[1:2, :], p_ref[2:3, :], p_ref[3:4, :]
    beta = _sigmoid(x)
    g = -jnp.exp(a_log) * _softplus(x + dt_b)
    ig = x + ig_b
    lf = -_softplus(-(x + fg_b))
    lo = chn < 2
    val = jnp.where(is_ml, jnp.where(lo, ig, lf), jnp.where(lo, beta, g))
    val = jnp.where(chn < 4, val, 0.0)
    ri = lax.broadcasted_iota(I32, (TL, TL), 0)
    ci = lax.broadcasted_iota(I32, (TL, TL), 1)
    same = (ri >> 6) == (ci >> 6)
    m_f = jnp.where(same & (ri >= ci), 1.0, 0.0)
    m_b = jnp.where(same & (ri <= ci), 1.0, 0.0)
    m_t = jnp.where(same, 1.0, 0.0)
    cum_f = _mm_f32(m_f, val)
    cum_b = _mm_f32(m_b, val)
    tot = pltpu.roll(_mm_f32(m_t, val), 2, axis=1)
    out = jnp.where(chn == 2, cum_f, jnp.where(chn == 3, cum_b, jnp.where((chn == 4) | (chn == 5), tot, val)))
    gr_ref[0] = out.T
    for hh in range(N_HEADS_ALL):
        grp, h = (0, hh) if hh < DN_HEADS else (1, hh - DN_HEADS)
        v = out[:, grp * LANES:(grp + 1) * LANES]
        sh = (LANES - 8 * h) % LANES
        gc_ref[hh] = pltpu.roll(v, sh, axis=1) if sh else v


def _gate_prep(raw, params):
    t = raw.shape[0]
    nt = t // TL
    return pl.pallas_call(
        _gate_kernel,
        out_shape=(jax.ShapeDtypeStruct((N_HEADS_ALL, t, LANES), F32),
                   jax.ShapeDtypeStruct((nt, 2 * LANES, TL), F32)),
        grid=(nt,),
        in_specs=[pl.BlockSpec((TL, 2 * LANES), lambda i: (i, 0)),
                  pl.BlockSpec((8, 2 * LANES), lambda i: (0, 0))],
        out_specs=(pl.BlockSpec((N_HEADS_ALL, TL, LANES), lambda i: (0, i, 0)),
                   pl.BlockSpec((1, 2 * LANES, TL), lambda i: (i, 0, 0))),
        compiler_params=_cp(("arbitrary",)),
        name="gate_prep",
    )(raw, params)


def _conv_centred(x_ref, w_ref, pad_ref, width):
    s = x_ref.shape[0]
    half = width // 2
    pad_ref[0:8, :] = jnp.zeros((8, pad_ref.shape[1]), F32)
    pad_ref[s + 8:s + 16, :] = jnp.zeros((8, pad_ref.shape[1]), F32)
    pad_ref[8:s + 8, :] = x_ref[...].astype(F32)
    acc = None
    for i in range(width):
        term = w_ref[i:i + 1, :] * pad_ref[8 - half + i:8 - half + i + s, :]
        acc = term if acc is None else acc + term
    return acc


def _conv_centred_tiles(x_ref, w_ref, pad_ref, width, cols):
    s = x_ref.shape[0]
    half = width // 2
    pad_ref[0:8, :] = jnp.zeros((8, pad_ref.shape[1]), F32)
    pad_ref[s + 8:s + 16, :] = jnp.zeros((8, pad_ref.shape[1]), F32)
    pad_ref[8:s + 8, :] = x_ref[:, cols].astype(F32)
    for rt in range(s // TL):
        base = 8 - half + rt * TL
        acc = None
        for i in range(width):
            term = w_ref[i:i + 1, cols] * pad_ref[base + i:base + i + TL, :]
            acc = term if acc is None else acc + term
        yield slice(rt * TL, (rt + 1) * TL), acc


def _tile_masks():
    ri = lax.broadcasted_iota(I32, (TL, TL), 0)
    ci = lax.broadcasted_iota(I32, (TL, TL), 1)
    return ri, ci


def _unit_tri_inverse(mats, same16, same32, eye):
    each = lambda f, *ls: [f(*xs) for xs in zip(*ls)]
    a16 = each(lambda a: jnp.where(same16, a, 0.0).astype(BF16), mats)
    b1 = each(lambda x: _mm(x, x).astype(BF16), a16)
    p = each(lambda x: eye - x.astype(F32), a16)
    b2 = each(lambda x: _mm(x, x).astype(BF16), b1)
    p = each(lambda x, b: x + _mm(x, b), p, b1)
    b3 = each(lambda x: _mm(x, x).astype(BF16), b2)
    p = each(lambda x, b: x + _mm(x, b), p, b2)
    p = each(lambda x, b: x + _mm(x, b), p, b3)
    off32 = same32 & jnp.logical_not(same16)
    a32 = each(lambda a: jnp.where(off32, a, 0.0).astype(BF16), mats)
    pb = each(lambda x: x.astype(BF16), p)
    t = each(_mm, pb, a32)
    p = each(lambda x, y, z: x - _mm(y, z), p, t, pb)
    a64 = each(lambda a: jnp.where(same32, 0.0, a).astype(BF16), mats)
    pb = each(lambda x: x.astype(BF16), p)
    t = each(_mm, pb, a64)
    p = each(lambda x, y, z: x - _mm(y, z), p, t, pb)
    return p


def _dn_prep_kernel(x_ref, w_ref, o_ref, pad_ref):
    j = pl.program_id(1)
    n_qk = 2 * DN_WIDTH // x_ref.shape[1]
    qscale = jnp.where(j < n_qk // 2, DN_DIM ** -0.5, 1.0)

    def body(normalise):
        for h in range(x_ref.shape[1] // DN_DIM):
            sl = slice(h * DN_DIM, (h + 1) * DN_DIM)
            for r, y in _conv_centred_tiles(x_ref, w_ref, pad_ref, DN_CONV, sl):
                y = y * _sigmoid(y)
                if normalise:
                    y = y * (lax.rsqrt(jnp.sum(y * y, axis=-1, keepdims=True) + EPS) * qscale)
                o_ref[r, sl] = y.astype(o_ref.dtype)

    pl.when(j < n_qk)(functools.partial(body, True))
    pl.when(j >= n_qk)(functools.partial(body, False))


def _dn_prep(proj, conv_w, batch, seq, wc=256):
    t = proj.shape[0]
    n = 3 * DN_WIDTH
    return pl.pallas_call(
        _dn_prep_kernel,
        out_shape=jax.ShapeDtypeStruct((t, n), BF16),
        grid=(batch, n // wc),
        in_specs=[pl.BlockSpec((seq, wc), lambda b, j: (b, j)), pl.BlockSpec((DN_CONV, wc), lambda b, j: (0, j))],
        out_specs=pl.BlockSpec((seq, wc), lambda b, j: (b, j)),
        scratch_shapes=[pltpu.VMEM((seq + 16, DN_DIM), F32)],
        compiler_params=_cp(("arbitrary", "arbitrary")),
        name="dn_prep",
    )(proj, conv_w)


def _dn_kernel(q_ref, k_ref, v_ref, z_ref, gc_ref, gr_ref, nw_ref, o_ref,
               u_ref, wq_ref, ak_ref, eg_ref, oo_ref):
    s = q_ref.shape[0]
    nt = s // TL
    nc = s // CH
    heads = range(DN_HPS)
    hl = [slice(h * DN_DIM, (h + 1) * DN_DIM) for h in heads]

    def tile_body(j, carry):
        ri, ci = _tile_masks()
        same64 = (ri >> 6) == (ci >> 6)
        same32 = (ri >> 5) == (ci >> 5)
        same16 = (ri >> 4) == (ci >> 4)
        eye = jnp.where(ri == ci, 1.0, 0.0)
        incl = [same64 & (ri >= ci), same64 & (ri <= ci)]
        strict = [same64 & (ri > ci), same64 & (ri < ci)]
        tiles = [j * DN_TPI + t for t in range(DN_TPI)]
        r0 = [pl.multiple_of(i * TL, TL) for i in tiles]
        ht = [(h, t) for h in heads for t in range(DN_TPI)]
        qn = {(h, t): q_ref[pl.ds(r0[t], TL), hl[h]].astype(F32) for h, t in ht}
        kn = {(h, t): k_ref[pl.ds(r0[t], TL), hl[h]].astype(F32) for h, t in ht}
        vv = {(h, t): v_ref[pl.ds(r0[t], TL), hl[h]].astype(F32) for h, t in ht}
        kk = {x: _mm_nt(kn[x], kn[x]) for x in ht}
        qk = {x: _mm_nt(qn[x], kn[x]) for x in ht}
        gc = {(h, t): gc_ref[h, pl.ds(r0[t], TL), :] for h, t in ht}
        gr = {(h, t): gr_ref[tiles[t]][8 * h:8 * h + 8, :] for h, t in ht}
        ch = [(h, t, d) for h, t in ht for d in (0, 1)]
        beta_c = [gc[h, t][:, d:d + 1] for h, t, d in ch]
        g_c = [gc[h, t][:, 2 + d:3 + d] for h, t, d in ch]
        gt_c = [gc[h, t][:, 4 + d:5 + d] for h, t, d in ch]
        g_r = [gr[h, t][2 + d:3 + d, :] for h, t, d in ch]
        dec = [jnp.exp(jnp.where(incl[d], g_c[n] - g_r[n], NEG)) for n, (h, t, d) in enumerate(ch)]
        a = [jnp.where(strict[d], beta_c[n] * kk[h, t] * dec[n], 0.0) for n, (h, t, d) in enumerate(ch)]
        tinv = _unit_tri_inverse(a, same16, same32, eye)
        eg_c = [jnp.exp(g) for g in g_c]
        rhs = [jnp.concatenate([beta_c[n] * vv[h, t], (beta_c[n] * eg_c[n]) * kn[h, t]], axis=1)
               for n, (h, t, d) in enumerate(ch)]
        uw = [_mm(x, y) for x, y in zip(tinv, rhs)]
        for n, (h, t, d) in enumerate(ch):
            i = tiles[t]
            u_ref[h, d, pl.ds(r0[t], TL), :] = uw[n][:, :DN_DIM]
            w = uw[n][:, DN_DIM:]
            qe = qn[h, t] * eg_c[n]
            attn = qk[h, t] * dec[n]
            kdt = (kn[h, t] * jnp.exp(gt_c[n] - g_c[n])).T
            egt = jnp.exp(gt_c[n])
            for c in range(CPT):
                rows = slice(c * CH, (c + 1) * CH)
                wq_ref[h, d, i * CPT + c] = jnp.concatenate([w[rows], qe[rows]], axis=0).astype(BF16)
                ak_ref[h, d, i * CPT + c] = jnp.concatenate([attn[rows, rows], kdt[:, rows]], axis=0).astype(BF16)
                eg_ref[h, d, i * CPT + c] = jnp.broadcast_to(egt[c * CH:c * CH + 8, :], (8, LANES))
        return carry

    lax.fori_loop(0, nt // DN_TPI, tile_body, 0)

    hd = [(h, d) for h in heads for d in (0, 1)]

    def chunk_body(c, carry):
        states = list(carry)
        ws, vnew, m2 = {}, {}, {}
        cc = [c, nc - 1 - c]
        r = [pl.multiple_of(x * CH, CH) for x in cc]
        for n, (h, d) in enumerate(hd):
            ws[n] = jnp.dot(wq_ref[h, d, cc[d]], states[n].astype(BF16), preferred_element_type=F32)
        for n, (h, d) in enumerate(hd):
            vnew[n] = u_ref[h, d, pl.ds(r[d], CH), :] - ws[n][:CH]
        for n, (h, d) in enumerate(hd):
            m2[n] = jnp.dot(ak_ref[h, d, cc[d]], vnew[n].astype(BF16), preferred_element_type=F32)
        for n, (h, d) in enumerate(hd):
            oo_ref[h, d, pl.ds(r[d], CH), :] = ws[n][CH:] + m2[n][:CH]
            states[n] = eg_ref[h, d, cc[d]][0:1, :] * states[n] + m2[n][CH:]
        return tuple(states)

    z0 = jnp.zeros((DN_DIM, DN_DIM), F32)
    lax.fori_loop(0, nc, chunk_body, tuple(z0 for _ in hd))

    for h in heads:
        for rt in range(s // TL):
            r = slice(rt * TL, (rt + 1) * TL)
            o = oo_ref[h, 0, r, :] + oo_ref[h, 1, r, :]
            o = o * lax.rsqrt(jnp.mean(o * o, axis=-1, keepdims=True) + EPS) * nw_ref[...]
            z = z_ref[r, hl[h]].astype(F32)
            o_ref[r, hl[h]] = (o * (z * _sigmoid(z))).astype(o_ref.dtype)


def _deltanet(proj, conv_w, gc, gr, norm_w, batch, seq):
    t = proj.shape[0]
    nc = seq // CH
    nt = seq // TL
    wb = DN_HPS * DN_DIM
    qkv = _dn_prep(proj, conv_w, batch, seq)
    qb, kb, vb, zb = OFF_DN_Q // wb, OFF_DN_K // wb, OFF_DN_V // wb, OFF_DN_Z // wb
    col = lambda off: pl.BlockSpec((seq, wb), lambda b, h, off=off: (b, off + h))
    return pl.pallas_call(
        _dn_kernel,
        out_shape=jax.ShapeDtypeStruct((t, DN_WIDTH), BF16),
        grid=(batch, DN_HEADS // DN_HPS),
        in_specs=[col(qb), col(kb), col(vb), col(zb),
                  pl.BlockSpec((DN_HPS, seq, LANES), lambda b, h: (h, b, 0)),
                  pl.BlockSpec((nt, 8 * DN_HPS, TL), lambda b, h: (b, h, 0)),
                  pl.BlockSpec((1, DN_DIM), lambda b, h: (0, 0))],
        out_specs=pl.BlockSpec((seq, wb), lambda b, h: (b, h)),
        scratch_shapes=[pltpu.VMEM((DN_HPS, 2, seq, DN_DIM), F32),
                        pltpu.VMEM((DN_HPS, 2, nc, 2 * CH, DN_DIM), BF16),
                        pltpu.VMEM((DN_HPS, 2, nc, CH + DN_DIM, CH), BF16),
                        pltpu.VMEM((DN_HPS, 2, nc, 8, LANES), F32),
                        pltpu.VMEM((DN_HPS, 2, seq, DN_DIM), F32)],
        compiler_params=_cp(("arbitrary", "arbitrary")),
        name="deltanet",
    )(qkv, qkv, qkv, proj, gc, gr, norm_w.reshape(1, DN_DIM))


def _ml_kernel(q_ref, k_ref, v_ref, og_ref, gc_ref, gr_ref, nw_ref, o_ref, oo_ref):
    s = q_ref.shape[0]
    nt = s // TL
    scale = ML_QK ** -0.5

    def tile_body(it, carry):
        m_carry = [carry[0], carry[1]]
        c_carry = [carry[2], carry[3]]
        n_carry = [carry[4], carry[5]]
        ri, ci = _tile_masks()
        same64 = (ri >> 6) == (ci >> 6)
        rowc = lax.broadcasted_iota(I32, (TL, 1), 0) >> 6
        colc = lax.broadcasted_iota(I32, (1, TL), 1) >> 6
        dirs = (0, 1)
        order = [list(range(CPT)), list(range(CPT - 1, -1, -1))]
        tile = [it, nt - 1 - it]
        r0 = [pl.multiple_of(i * TL, TL) for i in tile]
        qb = [q_ref[pl.ds(r, TL), :] for r in r0]
        qs = [x.astype(F32) * scale for x in qb]
        kf = [k_ref[pl.ds(r, TL), :].astype(F32) for r in r0]
        vb = [v_ref[pl.ds(r, TL), :] for r in r0]
        qk = [_mm_nt(a, b) for a, b in zip(qs, kf)]
        gc = [gc_ref[0, pl.ds(r, TL), :] for r in r0]
        gr = [gr_ref[i] for i in tile]
        ig_c = [gc[d][:, d:d + 1] for d in dirs]
        b_c = [gc[d][:, 2 + d:3 + d] for d in dirs]
        bt_c = [gc[d][:, 4 + d:5 + d] for d in dirs]
        ig_r = [gr[d][d:d + 1, :] for d in dirs]
        b_r = [gr[d][2 + d:3 + d, :] for d in dirs]
        bt_r = [gr[d][4 + d:5 + d, :] for d in dirs]
        incl = [same64 & (ri >= ci), same64 & (ri <= ci)]
        dmat = [jnp.where(incl[d], b_c[d] - b_r[d] + ig_r[d], NEG) for d in dirs]
        m_intra = [jnp.max(x, axis=1, keepdims=True) for x in dmat]
        d_end_c = [bt_c[d] - b_c[d] + ig_c[d] for d in dirs]
        d_end_r = [bt_r[d] - b_r[d] + ig_r[d] for d in dirs]
        dmax_c = [jnp.max(jnp.where(same64, x, NEG), axis=1, keepdims=True) for x in d_end_r]
        m_in_c, m_out_c, csc = [], [], []
        for d in dirs:
            m_in = jnp.zeros((TL, 1), F32)
            m_out = jnp.zeros((TL, 1), F32)
            m = m_carry[d]
            cs = {}
            for cc in order[d]:
                btot = bt_c[d][cc * CH:cc * CH + 1, :]
                dmx = dmax_c[d][cc * CH:cc * CH + 1, :]
                m_new = jnp.maximum(btot + m, dmx)
                cs[cc] = jnp.exp(btot + m - m_new)
                m_in = jnp.where(rowc == cc, m, m_in)
                m_out = jnp.where(rowc == cc, m_new, m_out)
                m = m_new
            m_carry[d] = m
            m_in_c.append(m_in)
            m_out_c.append(m_out)
            csc.append(cs)
        m_inter = [b_c[d] + m_in_c[d] for d in dirs]
        m_pos = [jnp.maximum(m_intra[d], m_inter[d]) for d in dirs]
        scores = [qk[d] * jnp.exp(dmat[d] - m_pos[d]) for d in dirs]
        inter_scale = [jnp.exp(m_inter[d] - m_pos[d]) for d in dirs]
        rsum = [jnp.sum(x, axis=1, keepdims=True) for x in scores]
        einv = [jnp.exp(-x) for x in m_pos]
        wk = [kf[d] * jnp.exp(d_end_c[d] - m_out_c[d]) for d in dirs]
        wkt = [x.T for x in wk]
        intra = [jnp.dot(scores[d].astype(BF16), vb[d], preferred_element_type=F32) for d in dirs]
        wstack = [jnp.concatenate([jnp.where(colc == c, wkt[d], 0.0) for c in range(CPT)], axis=0).astype(BF16)
                  for d in dirs]
        dcs = [jnp.dot(wstack[d], vb[d], preferred_element_type=F32) for d in dirs]
        c_at, n_row = [], []
        for d in dirs:
            cst, nst = c_carry[d], n_carry[d]
            c_start = {}
            nr = jnp.zeros((TL, ML_QK), F32)
            for cc in order[d]:
                c_start[cc] = cst
                nr = jnp.where(rowc == cc, nst, nr)
                dn = jnp.sum(wk[d][cc * CH:(cc + 1) * CH], axis=0, keepdims=True)
                cst = csc[d][cc] * cst + dcs[d][cc * ML_QK:(cc + 1) * ML_QK]
                nst = csc[d][cc] * nst + dn
            c_carry[d], n_carry[d] = cst, nst
            c_at.append(jnp.concatenate([c_start[c] for c in range(CPT)], axis=0).astype(BF16))
            n_row.append(nr)
        q4 = [jnp.concatenate([jnp.where(rowc == c, qb[d], jnp.zeros_like(qb[d])) for c in range(CPT)], axis=1)
              for d in dirs]
        inter = [jnp.dot(q4[d], c_at[d], preferred_element_type=F32) * scale for d in dirs]
        for d in dirs:
            qn = jnp.sum(qs[d] * n_row[d], axis=1, keepdims=True)
            num = intra[d] + inter_scale[d] * inter[d]
            den = rsum[d] + inter_scale[d] * qn
            oo_ref[d, pl.ds(r0[d], TL), :] = num / jnp.maximum(jnp.abs(den), einv[d])
        return (m_carry[0], m_carry[1], c_carry[0], c_carry[1], n_carry[0], n_carry[1])

    zero11 = jnp.zeros((1, 1), F32)
    c0 = jnp.zeros((ML_QK, ML_V), F32)
    n0 = jnp.zeros((1, ML_QK), F32)
    lax.fori_loop(0, nt, tile_body, (zero11, zero11, c0, c0, n0, n0))

    for rt in range(s // TL):
        r = slice(rt * TL, (rt + 1) * TL)
        o = oo_ref[0, r, :] + oo_ref[1, r, :]
        o = o * lax.rsqrt(jnp.mean(o * o, axis=-1, keepdims=True) + EPS) * nw_ref[...]
        o_ref[r, :] = (o * _sigmoid(og_ref[r, :].astype(F32))).astype(o_ref.dtype)


def _mlstm(proj, gc, gr, norm_w, batch, seq):
    t = proj.shape[0]
    nt = seq // TL
    qb, kb, vb, ob = OFF_ML_Q // ML_QK, OFF_ML_K // ML_QK, OFF_ML_V // ML_V, OFF_ML_O // ML_V
    return pl.pallas_call(
        _ml_kernel,
        out_shape=jax.ShapeDtypeStruct((t, ML_WIDTH), BF16),
        grid=(batch, ML_HEADS),
        in_specs=[pl.BlockSpec((seq, ML_QK), lambda b, h: (b, qb + h)),
                  pl.BlockSpec((seq, ML_QK), lambda b, h: (b, kb + h)),
                  pl.BlockSpec((seq, ML_V), lambda b, h: (b, vb + h)),
                  pl.BlockSpec((seq, ML_V), lambda b, h: (b, ob + h)),
                  pl.BlockSpec((1, seq, LANES), lambda b, h: (DN_HEADS + h, b, 0)),
                  pl.BlockSpec((nt, 8, TL), lambda b, h: (b, LANES // 8 + h, 0)),
                  pl.BlockSpec((1, ML_V), lambda b, h: (0, h))],
        out_specs=pl.BlockSpec((seq, ML_V), lambda b, h: (b, h)),
        scratch_shapes=[pltpu.VMEM((2, seq, ML_V), F32)],
        compiler_params=_cp(("arbitrary", "arbitrary")),
        name="mlstm",
    )(proj, proj, proj, proj, gc, gr, norm_w.reshape(1, ML_WIDTH))


def _sc_kernel(b_ref, c_ref, x_ref, w_ref, o_ref, cx_ref, pad_ref):
    cx_ref[...] = c_ref[...].astype(F32) * x_ref[...].astype(F32)
    y = _conv_centred(cx_ref, w_ref, pad_ref, SC_CONV)
    o_ref[...] = (b_ref[...].astype(F32) * y).astype(o_ref.dtype)


def _shortconv(proj, conv_w, batch, seq, wc=256):
    t = proj.shape[0]
    bb, cb, xb = OFF_SC_B // wc, OFF_SC_C // wc, OFF_SC_X // wc
    col = lambda off: pl.BlockSpec((seq, wc), lambda b, j, off=off: (b, off + j))
    return pl.pallas_call(
        _sc_kernel,
        out_shape=jax.ShapeDtypeStruct((t, SC_WIDTH), BF16),
        grid=(batch, SC_WIDTH // wc),
        in_specs=[col(bb), col(cb), col(xb), pl.BlockSpec((SC_CONV, wc), lambda b, j: (0, j))],
        out_specs=pl.BlockSpec((seq, wc), lambda b, j: (b, j)),
        scratch_shapes=[pltpu.VMEM((seq, wc), F32), pltpu.VMEM((seq + 16, wc), F32)],
        compiler_params=_cp(("arbitrary", "arbitrary")),
        name="shortconv",
    )(proj, proj, proj, conv_w)


def _pack_gate_cols(dn_b, dn_a, ml_i, ml_f):
    r = dn_b.shape[0]

    def grp(lo, hi, nh):
        a = jnp.stack([lo[:, :nh], lo[:, nh:], hi[:, :nh], hi[:, nh:]], axis=-1)
        a = jnp.concatenate([a, jnp.zeros((r, nh, 4), a.dtype)], axis=-1).reshape(r, nh * 8)
        return jnp.concatenate([a, jnp.zeros((r, LANES - nh * 8), a.dtype)], axis=-1)

    return jnp.concatenate([grp(dn_b, dn_a, DN_HEADS), grp(ml_i, ml_f, ML_HEADS)], axis=-1)


def _gate_params(dn_a_log, dn_dt_bias, ml_igate_b, ml_fgate_b):
    z_dn = jnp.zeros((1, 2 * DN_HEADS), F32)
    z_ml = jnp.zeros((1, 2 * ML_HEADS), F32)
    flat = lambda a: a.astype(F32).reshape(1, -1)
    rows = [_pack_gate_cols(z_dn, flat(dn_a_log), z_ml, z_ml),
            _pack_gate_cols(z_dn, flat(dn_dt_bias), z_ml, z_ml),
            _pack_gate_cols(z_dn, z_dn, flat(ml_igate_b), z_ml),
            _pack_gate_cols(z_dn, z_dn, z_ml, flat(ml_fgate_b))]
    return jnp.concatenate(rows + [jnp.zeros((4, 2 * LANES), F32)], axis=0)


def _outproj_kernel(a1_ref, a2_ref, a3_ref, w1_ref, w2_ref, w3_ref, x_ref, o_ref):
    acc = jnp.dot(a1_ref[...], w1_ref[0].astype(BF16), preferred_element_type=F32)
    acc += jnp.dot(a2_ref[...], w2_ref[0].astype(BF16), preferred_element_type=F32)
    acc += jnp.dot(a3_ref[...], w3_ref[0].astype(BF16), preferred_element_type=F32)
    o_ref[...] = x_ref[...] + acc


def _outproj(dn, ml, sc, w_out, layer, x, tm=1024, tn=512):
    t, d = x.shape
    k1, k2, k3 = dn.shape[1], ml.shape[1], sc.shape[1]
    assert k1 == k2 and (k1 + k2) % k3 == 0
    return pl.pallas_call(
        _outproj_kernel,
        out_shape=jax.ShapeDtypeStruct((t, d), F32),
        grid=(t // tm, d // tn),
        in_specs=[pl.BlockSpec((tm, k1), lambda i, j: (i, 0)),
                  pl.BlockSpec((tm, k2), lambda i, j: (i, 0)),
                  pl.BlockSpec((tm, k3), lambda i, j: (i, 0)),
                  pl.BlockSpec((1, k1, tn), lambda i, j: (layer, 0, j)),
                  pl.BlockSpec((1, k2, tn), lambda i, j: (layer, 1, j)),
                  pl.BlockSpec((1, k3, tn), lambda i, j: (layer, (k1 + k2) // k3, j)),
                  pl.BlockSpec((tm, tn), lambda i, j: (i, j))],
        out_specs=pl.BlockSpec((tm, tn), lambda i, j: (i, j)),
        compiler_params=_cp(("arbitrary", "arbitrary")),
        name="outproj",
    )(dn, ml, sc, w_out, w_out, w_out, x)


ROUTE_TM = 256
EXP_ROW0 = 32
HALF_D = D_MODEL // 2


def _router_kernel(x_ref, nw_ref, wr_ref, br_ref, hp_ref, meta_ref, wc_ref, cnt_ref, base_ref,
                   whi_ref, wlo_ref):
    tm = x_ref.shape[0]

    @pl.when(pl.program_id(0) == 0)
    def _():
        base_ref[...] = jnp.zeros(base_ref.shape, F32)
        w = wr_ref[...]
        whi_ref[...] = w.astype(BF16)
        wlo_ref[...] = (w - whi_ref[...].astype(F32)).astype(BF16)

    x = x_ref[...]
    h = x * lax.rsqrt(jnp.mean(x * x, axis=-1, keepdims=True) + EPS) * nw_ref[...]
    bits = pltpu.bitcast(h.astype(BF16).astype(F32), U32)
    hp_ref[...] = (bits[:, :HALF_D] >> 16) | bits[:, HALF_D:]

    h_hi = h.astype(BF16)
    h_lo = (h - h_hi.astype(F32)).astype(BF16)
    logits = (jnp.dot(h_hi, whi_ref[...], preferred_element_type=F32)
              + jnp.dot(h_lo, whi_ref[...], preferred_element_type=F32)
              + jnp.dot(h_hi, wlo_ref[...], preferred_element_type=F32))
    lt = (logits + br_ref[...]).T
    row8 = lax.broadcasted_iota(I32, (8, tm), 0)
    gl = jnp.where(row8 < N_GROUPS, lt[0:8, :], NEG)
    gex = jnp.exp(gl - jnp.max(gl, axis=0, keepdims=True))
    gp = gex / jnp.sum(gex, axis=0, keepdims=True)
    g_w = jnp.max(gp, axis=0, keepdims=True)
    g_idx = jnp.min(jnp.where(gp == g_w, row8, 8), axis=0, keepdims=True)
    el = lt[EXP_ROW0 + 24:EXP_ROW0 + 32, :]
    for g in (2, 1, 0):
        el = jnp.where(g_idx == g, lt[EXP_ROW0 + 8 * g:EXP_ROW0 + 8 * g + 8, :], el)
    ee = jnp.exp(el - jnp.max(el, axis=0, keepdims=True))
    p = ee / jnp.sum(ee, axis=0, keepdims=True)
    p1 = jnp.max(p, axis=0, keepdims=True)
    i1 = jnp.min(jnp.where(p == p1, row8, 8), axis=0, keepdims=True)
    pm = jnp.where(row8 == i1, -1.0, p)
    p2 = jnp.max(pm, axis=0, keepdims=True)
    i2 = jnp.min(jnp.where(pm == p2, row8, 8), axis=0, keepdims=True)
    den = p1 + p2
    w1 = g_w * p1 / den
    w2 = g_w * p2 / den
    e1 = g_idx * EXPERTS_PER_GROUP + i1
    e2 = g_idx * EXPERTS_PER_GROUP + i2

    rowe = lax.broadcasted_iota(I32, (N_EXPERTS, tm), 0)
    oh1 = jnp.where(rowe == e1, 1.0, 0.0)
    oh2 = jnp.where(rowe == e2, 1.0, 0.0)
    oh = (oh1 + oh2).astype(BF16)
    ri = lax.broadcasted_iota(I32, (tm, tm), 0)
    ci = lax.broadcasted_iota(I32, (tm, tm), 1)
    before = jnp.where(ri < ci, 1.0, 0.0).astype(BF16)
    ones = jnp.ones((tm, tm), BF16)
    tot = jnp.dot(oh, before, preferred_element_type=F32) + base_ref[...]
    r1 = jnp.sum(oh1 * tot, axis=0, keepdims=True)
    r2 = jnp.sum(oh2 * tot, axis=0, keepdims=True)
    base_ref[...] = base_ref[...] + jnp.dot(oh, ones, preferred_element_type=F32)
    cnt_ref[...] = base_ref[...]

    meta_ref[...] = jnp.concatenate(
        [e1, e2, r1.astype(I32), r2.astype(I32), jnp.zeros((4, tm), I32)], axis=0)
    wrow = lax.broadcasted_iota(I32, (LANES, tm), 0)
    wc_ref[...] = jnp.where(wrow == 0, w1, jnp.where(wrow == 1, w2, 0.0)).T


def _router(x, norm_w, wr, br):
    t, d = x.shape
    tm = ROUTE_TM
    return pl.pallas_call(
        _router_kernel,
        out_shape=(jax.ShapeDtypeStruct((t, HALF_D), U32),
                   jax.ShapeDtypeStruct((8, t), I32),
                   jax.ShapeDtypeStruct((t, LANES), F32),
                   jax.ShapeDtypeStruct((N_EXPERTS, tm), F32)),
        grid=(t // tm,),
        in_specs=[pl.BlockSpec((tm, d), lambda i: (i, 0)),
                  pl.BlockSpec((1, d), lambda i: (0, 0)),
                  pl.BlockSpec((d, LANES), lambda i: (0, 0)),
                  pl.BlockSpec((1, LANES), lambda i: (0, 0))],
        out_specs=(pl.BlockSpec((tm, HALF_D), lambda i: (i, 0)),
                   pl.BlockSpec((8, tm), lambda i: (0, i)),
                   pl.BlockSpec((tm, LANES), lambda i: (i, 0)),
                   pl.BlockSpec((N_EXPERTS, tm), lambda i: (0, 0))),
        scratch_shapes=[pltpu.VMEM((N_EXPERTS, tm), F32),
                        pltpu.VMEM((d, LANES), BF16), pltpu.VMEM((d, LANES), BF16)],
        compiler_params=_cp(("arbitrary",)),
        name="router",
    )(x, norm_w.reshape(1, d), wr, br)


SCAT_TM = 256


def _scatter_kernel(zs_ref, ze_ref, nu_ref, s1_ref, s2_ref, hp_ref, xs_ref, zero_ref, sem):
    i = pl.program_id(0)
    tm = hp_ref.shape[0]
    n_all = xs_ref.shape[0] // MOE_BLK

    @pl.when(i == 0)
    def _():
        zero_ref[...] = jnp.zeros(zero_ref.shape, zero_ref.dtype)

        def zfill(e, c):
            zero_row = lambda r: pltpu.make_async_copy(zero_ref.at[pl.ds(0, 1), :], xs_ref.at[pl.ds(r, 1), :], sem)

            def one(r, c2):
                zero_row(r).start()
                return c2

            def one_wait(r, c2):
                zero_row(r).wait()
                return c2

            lax.fori_loop(zs_ref[e], ze_ref[e], one, 0)
            lax.fori_loop(zs_ref[e], ze_ref[e], one_wait, 0)
            return c

        def zero_block(b):
            r0 = pl.multiple_of(b * MOE_BLK, MOE_BLK)
            return pltpu.make_async_copy(zero_ref, xs_ref.at[pl.ds(r0, MOE_BLK), :], sem)

        def tfill(b, c):
            zero_block(b).start()
            return c

        def twait(b, c):
            zero_block(b).wait()
            return c

        lax.fori_loop(0, N_EXPERTS, zfill, 0)
        lax.fori_loop(nu_ref[0], n_all, tfill, 0)
        lax.fori_loop(nu_ref[0], n_all, twait, 0)

    def row_copies(t):
        src = hp_ref.at[pl.ds(t, 1), :]
        return (pltpu.make_async_copy(src, xs_ref.at[pl.ds(s1_ref[0, 0, t], 1), :], sem),
                pltpu.make_async_copy(src, xs_ref.at[pl.ds(s2_ref[0, 0, t], 1), :], sem))

    def issue(t, c):
        for prio, cp in enumerate(row_copies(t)):
            cp.start(priority=prio)
        return c

    def drain(t, c):
        for cp in row_copies(t):
            cp.wait()
        return c

    lax.fori_loop(0, tm, issue, 0, unroll=8)
    lax.fori_loop(0, tm, drain, 0, unroll=8)


def _scatter_rows(zero_start, zero_end, n_used, slot1, slot2, hp, n_rows):
    t, w = hp.shape
    tm = SCAT_TM
    s3 = lambda a: a.reshape(t // tm, 1, tm)
    grid_spec = pltpu.PrefetchScalarGridSpec(
        num_scalar_prefetch=3,
        grid=(t // tm,),
        in_specs=[pl.BlockSpec((1, 1, tm), lambda i, zs, ze, nu: (i, 0, 0), memory_space=pltpu.SMEM),
                  pl.BlockSpec((1, 1, tm), lambda i, zs, ze, nu: (i, 0, 0), memory_space=pltpu.SMEM),
                  pl.BlockSpec((tm, w), lambda i, zs, ze, nu: (i, 0))],
        out_specs=pl.BlockSpec(memory_space=pl.ANY),
        scratch_shapes=[pltpu.VMEM((MOE_BLK, w), U32), pltpu.SemaphoreType.DMA(())],
    )
    return pl.pallas_call(
        _scatter_kernel,
        out_shape=jax.ShapeDtypeStruct((n_rows, w), U32),
        grid_spec=grid_spec,
        compiler_params=_cp(("arbitrary",)),
        name="scatter_rows",
    )(zero_start, zero_end, n_used, s3(slot1), s3(slot2), hp)


FF_TN = 256


def _unpack_rows(x_ref, xs_ref):
    u = x_ref[...]
    xs_ref[:, :HALF_D] = pltpu.bitcast(u << 16, F32).astype(BF16)
    xs_ref[:, HALF_D:] = pltpu.bitcast(u & jnp.uint32(0xFFFF0000), F32).astype(BF16)


def _ffn_up_kernel(be_ref, nxt_ref, par_ref, nu_ref, x_ref, wg_hbm, wu_hbm, o_ref,
                   wbuf_ref, wgb_ref, wub_ref, xs_ref, sem, *, layer):
    j = pl.program_id(0)
    i = pl.program_id(1)

    def fetch(e, p):
        cols = pl.ds(pl.multiple_of(j * FF_TN, FF_TN), FF_TN)
        return (pltpu.make_async_copy(wg_hbm.at[layer, e, :, cols], wbuf_ref.at[p, 0], sem.at[p, 0]),
                pltpu.make_async_copy(wu_hbm.at[layer, e, :, cols], wbuf_ref.at[p, 1], sem.at[p, 1]))

    @pl.when(i < nu_ref[0])
    def _():
        e = be_ref[i]
        p = par_ref[i]

        @pl.when(i == 0)
        def _():
            for cp in fetch(e, p):
                cp.start()

        @pl.when((i == 0) | (e != be_ref[jnp.maximum(i - 1, 0)]))
        def _():
            for cp in fetch(e, p):
                cp.wait()
            wgb_ref[...] = wbuf_ref[p, 0].astype(BF16)
            wub_ref[...] = wbuf_ref[p, 1].astype(BF16)

            @pl.when(nxt_ref[i] != e)
            def _():
                for cp in fetch(nxt_ref[i], 1 - p):
                    cp.start()

        _unpack_rows(x_ref, xs_ref)
        x = xs_ref[...]
        g = jnp.dot(x, wgb_ref[...], preferred_element_type=F32)
        u = jnp.dot(x, wub_ref[...], preferred_element_type=F32)
        o_ref[...] = (g * _sigmoid(g) * u).astype(o_ref.dtype)

    @pl.when(i >= nu_ref[0])
    def _():
        o_ref[...] = jnp.zeros(o_ref.shape, o_ref.dtype)


def _ffn_up(block_exp, next_exp, slot_par, n_used, xs, w_gate, w_up, layer, n_blocks):
    d, ff = w_gate.shape[2], w_gate.shape[3]
    grid_spec = pltpu.PrefetchScalarGridSpec(
        num_scalar_prefetch=4,
        grid=(ff // FF_TN, n_blocks),
        in_specs=[pl.BlockSpec((MOE_BLK, HALF_D), lambda j, i, be, nx, pr, nu: (jnp.minimum(i, nu[0] - 1), 0)),
                  pl.BlockSpec(memory_space=pl.ANY),
                  pl.BlockSpec(memory_space=pl.ANY)],
        out_specs=pl.BlockSpec((MOE_BLK, FF_TN), lambda j, i, be, nx, pr, nu: (i, j)),
        scratch_shapes=[pltpu.VMEM((2, 2, d, FF_TN), F32),
                        pltpu.VMEM((d, FF_TN), BF16), pltpu.VMEM((d, FF_TN), BF16),
                        pltpu.VMEM((MOE_BLK, d), BF16),
                        pltpu.SemaphoreType.DMA((2, 2))],
    )
    return pl.pallas_call(
        functools.partial(_ffn_up_kernel, layer=layer),
        out_shape=jax.ShapeDtypeStruct((n_blocks * MOE_BLK, ff), BF16),
        grid_spec=grid_spec,
        compiler_params=_cp(("arbitrary", "arbitrary")),
        name="ffn_up",
    )(block_exp, next_exp, slot_par, n_used, xs, w_gate, w_up)


def _ffn_down_kernel(be_ref, nxt_ref, par_ref, nu_ref, h_ref, wd_hbm, y_ref, wbuf_ref, wdb_ref, sem, *, layer):
    i = pl.program_id(0)

    def fetch(e, p):
        return pltpu.make_async_copy(wd_hbm.at[layer, e], wbuf_ref.at[p], sem.at[p])

    @pl.when(i < nu_ref[0])
    def _():
        e = be_ref[i]
        p = par_ref[i]

        @pl.when(i == 0)
        def _():
            fetch(e, p).start()

        @pl.when((i == 0) | (e != be_ref[jnp.maximum(i - 1, 0)]))
        def _():
            fetch(e, p).wait()
            wdb_ref[...] = wbuf_ref[p].astype(BF16)

            @pl.when(nxt_ref[i] != e)
            def _():
                fetch(nxt_ref[i], 1 - p).start()

        y = jnp.dot(h_ref[...], wdb_ref[...], preferred_element_type=F32)
        bits = pltpu.bitcast(y.astype(BF16).astype(F32), U32)
        half = bits.shape[1] // 2
        y_ref[...] = (bits[:, :half] >> 16) | bits[:, half:]

    @pl.when(i >= nu_ref[0])
    def _():
        y_ref[...] = jnp.zeros(y_ref.shape, y_ref.dtype)


def _ffn_down(block_exp, next_exp, slot_par, n_used, hmid, w_down, layer, n_blocks):
    ff, d = w_down.shape[2], w_down.shape[3]
    grid_spec = pltpu.PrefetchScalarGridSpec(
        num_scalar_prefetch=4,
        grid=(n_blocks,),
        in_specs=[pl.BlockSpec((MOE_BLK, ff), lambda i, be, nx, pr, nu: (jnp.minimum(i, nu[0] - 1), 0)),
                  pl.BlockSpec(memory_space=pl.ANY)],
        out_specs=pl.BlockSpec((MOE_BLK, d // 2), lambda i, be, nx, pr, nu: (i, 0)),
        scratch_shapes=[pltpu.VMEM((2, ff, d), F32), pltpu.VMEM((ff, d), BF16),
                        pltpu.SemaphoreType.DMA((2,))],
    )
    return pl.pallas_call(
        functools.partial(_ffn_down_kernel, layer=layer),
        out_shape=jax.ShapeDtypeStruct((n_blocks * MOE_BLK, d // 2), U32),
        grid_spec=grid_spec,
        compiler_params=_cp(("arbitrary",)),
        name="ffn_down",
    )(block_exp, next_exp, slot_par, n_used, hmid, w_down)


COMB_TM = 256


def _combine_kernel(s1_ref, s2_ref, x_ref, wc_ref, fw_ref, y_ref, *rest, final):
    o_ref, h_ref = (rest[0], None) if final else rest[:2]
    ya_ref, yb_ref, sem = rest[-3:]
    tm = x_ref.shape[0]

    def row_copies(t):
        return (pltpu.make_async_copy(y_ref.at[pl.ds(s1_ref[0, 0, t], 1), :], ya_ref.at[pl.ds(t, 1), :], sem),
                pltpu.make_async_copy(y_ref.at[pl.ds(s2_ref[0, 0, t], 1), :], yb_ref.at[pl.ds(t, 1), :], sem))

    def issue(t, c):
        for prio, cp in enumerate(row_copies(t)):
            cp.start(priority=prio)
        return c

    def drain(t, c):
        for cp in row_copies(t):
            cp.wait()
        return c

    lax.fori_loop(0, tm, issue, 0, unroll=8)
    lax.fori_loop(0, tm, drain, 0, unroll=8)
    w1, w2 = wc_ref[:, 0:1], wc_ref[:, 1:2]
    ua, ub = ya_ref[...], yb_ref[...]
    hi = jnp.uint32(0xFFFF0000)
    lo_sum = w1 * pltpu.bitcast(ua << 16, F32) + w2 * pltpu.bitcast(ub << 16, F32)
    hi_sum = w1 * pltpu.bitcast(ua & hi, F32) + w2 * pltpu.bitcast(ub & hi, F32)
    o = x_ref[...] + jnp.concatenate([lo_sum, hi_sum], axis=1)
    normed = o * lax.rsqrt(jnp.mean(o * o, axis=-1, keepdims=True) + EPS) * fw_ref[...]
    if final:
        o_ref[...] = normed
    else:
        o_ref[...] = o
        h_ref[...] = normed.astype(h_ref.dtype)


def _combine(slot1, slot2, x, wcol, y, norm_w, final):
    t, d = x.shape
    tm = COMB_TM
    s3 = lambda a: a.reshape(t // tm, 1, tm)
    row_blk = pl.BlockSpec((tm, d), lambda i: (i, 0))
    out_f32 = jax.ShapeDtypeStruct((t, d), F32)
    return pl.pallas_call(
        functools.partial(_combine_kernel, final=final),
        out_shape=out_f32 if final else (out_f32, jax.ShapeDtypeStruct((t, d), BF16)),
        grid=(t // tm,),
        in_specs=[pl.BlockSpec((1, 1, tm), lambda i: (i, 0, 0), memory_space=pltpu.SMEM),
                  pl.BlockSpec((1, 1, tm), lambda i: (i, 0, 0), memory_space=pltpu.SMEM),
                  pl.BlockSpec((tm, d), lambda i: (i, 0)),
                  pl.BlockSpec((tm, LANES), lambda i: (i, 0)),
                  pl.BlockSpec((1, d), lambda i: (0, 0)),
                  pl.BlockSpec(memory_space=pl.ANY)],
        out_specs=row_blk if final else (row_blk, row_blk),
        scratch_shapes=[pltpu.VMEM((tm, d // 2), U32), pltpu.VMEM((tm, d // 2), U32),
                        pltpu.SemaphoreType.DMA(())],
        compiler_params=_cp(("arbitrary",)),
        name="combine",
    )(s3(slot1), s3(slot2), x, wcol, norm_w.reshape(1, d), y)


def _mixer_layer(xf, h, batch, seq, w_in, dn_conv_w, dn_a_log, dn_dt_bias, dn_norm_w, ml_igate_b,
                 ml_fgate_b, ml_norm_w, sc_conv_w, w_out, layer):
    c0 = 3 * DN_WIDTH + DN_WIDTH
    c1 = c0 + 4 * DN_HEADS
    c2 = c1 + 2 * ML_HEADS * ML_QK + 2 * ML_WIDTH
    c3 = c2 + 4 * ML_HEADS
    w_main = jnp.concatenate([w_in[:, :c0], w_in[:, c1:c2], w_in[:, c3:]], axis=1).astype(BF16)
    w_gate = _pack_gate_cols(w_in[:, c0:c0 + 2 * DN_HEADS], w_in[:, c0 + 2 * DN_HEADS:c1],
                             w_in[:, c2:c2 + 2 * ML_HEADS], w_in[:, c2 + 2 * ML_HEADS:c3]).astype(BF16)
    proj, graw = _inproj(h, w_main, w_gate)
    gc, gr = _gate_prep(graw, _gate_params(dn_a_log, dn_dt_bias, ml_igate_b, ml_fgate_b))
    dn = _deltanet(proj, dn_conv_w, gc, gr, dn_norm_w, batch, seq)
    ml = _mlstm(proj, gc, gr, ml_norm_w, batch, seq)
    sc = _shortconv(proj, sc_conv_w, batch, seq)
    return _outproj(dn, ml, sc, w_out, layer, xf)


def _moe_layer(xf, norm_w, rg_w, rg_b, re_w, re_b, w_gate, w_up, w_down, layer, out_norm_w, final):
    t, d = xf.shape
    def lanes(g, e):
        r = g.shape[0]
        return jnp.concatenate([g, jnp.zeros((r, EXP_ROW0 - N_GROUPS), F32), e,
                                jnp.zeros((r, LANES - EXP_ROW0 - N_EXPERTS), F32)], axis=1)

    wr = lanes(rg_w, re_w)
    br = lanes(rg_b.reshape(1, -1), re_b.reshape(1, -1))
    hp, meta, wcol, cnt = _router(xf, norm_w, wr, br)
    counts = cnt[:, 0].astype(I32)
    padded = (counts + MOE_BLK - 1) // MOE_BLK * MOE_BLK
    pad_end = jnp.cumsum(padded)
    pad_start = pad_end - padded
    n_blocks = -(-(2 * t + N_EXPERTS * (MOE_BLK - 1)) // MOE_BLK)
    n_used = (pad_end[-1:] // MOE_BLK).astype(I32)
    blk_row = jnp.arange(n_blocks, dtype=I32)[:, None] * MOE_BLK
    block_exp = jnp.minimum(jnp.sum((pad_end[None, :] <= blk_row).astype(I32), axis=1), N_EXPERTS - 1)
    slot1 = pad_start[meta[0]] + meta[2]
    slot2 = pad_start[meta[1]] + meta[3]
    xs = _scatter_rows(pad_start + counts, pad_end, n_used, slot1, slot2, hp, n_blocks * MOE_BLK)
    blk_i = jnp.arange(n_blocks, dtype=I32)
    slot_par = (jnp.cumsum((block_exp != jnp.roll(block_exp, 1)).astype(I32).at[0].set(0)) & 1).astype(I32)
    later = (block_exp[None, :] > block_exp[:, None]) & (blk_i[None, :] < n_used[0])
    next_exp = jnp.min(jnp.where(later, block_exp[None, :], N_EXPERTS), axis=1)
    next_exp = jnp.where(next_exp == N_EXPERTS, block_exp, next_exp).astype(I32)
    hmid = _ffn_up(block_exp, next_exp, slot_par, n_used, xs, w_gate, w_up, layer, n_blocks)
    y = _ffn_down(block_exp, next_exp, slot_par, n_used, hmid, w_down, layer, n_blocks)
    return _combine(slot1, slot2, xf, wcol, y, out_norm_w, final)


def kernel(x, norm_mix_w, w_in, dn_conv_w, dn_a_log, dn_dt_bias, dn_norm_w, ml_igate_b, ml_fgate_b, ml_norm_w,
           sc_conv_w, w_out, norm_ffn_w, router_group_w, router_group_b, router_expert_w, router_expert_b,
           expert_w_gate, expert_w_up, expert_w_down, final_norm_w):
    batch, seq, d = x.shape
    depth = w_in.shape[0]
    xf = x.reshape(batch * seq, d)
    h = _rmsnorm(xf, norm_mix_w[0], BF16)
    for l in range(depth):
        last = l == depth - 1
        xf = _mixer_layer(xf, h, batch, seq, w_in[l], dn_conv_w[l], dn_a_log[l], dn_dt_bias[l],
                          dn_norm_w[l], ml_igate_b[l], ml_fgate_b[l], ml_norm_w[l], sc_conv_w[l], w_out, l)
        out = _moe_layer(xf, norm_ffn_w[l], router_group_w[l], router_group_b[l], router_expert_w[l],
                         router_expert_b[l], expert_w_gate, expert_w_up, expert_w_down, l,
                         final_norm_w if last else norm_mix_w[l + 1], last)
        xf, h = (out, None) if last else out
    return xf.reshape(batch, seq, d)
```
